```python
import jax
import jax.numpy as jnp
from jax import lax
import numpy as np

D_MODEL = 1024
BATCH = 1
SEQ = 16384
DEPTH = 4

HEAD_DIM = 64
N_RET_HEADS = D_MODEL // (2 * HEAD_DIM)
N_DIL_HEADS = D_MODEL // (2 * HEAD_DIM)
N_NA_HEADS = D_MODEL // HEAD_DIM
RET_W = N_RET_HEADS * HEAD_DIM
DIL_W = N_DIL_HEADS * HEAD_DIM
NA_W = N_NA_HEADS * HEAD_DIM
EVEN_IN = 5 * RET_W + 3 * DIL_W
ODD_IN = 3 * NA_W
RET_CHUNK = 128
RET_DECAY_EXP0 = 5.0
ROPE_THETA = 10000.0
DIL_CONFIGS = ((128, 1), (512, 4), (2048, 16))
GRID_W = 64
NA_KH_MAX = 8
NA_KW = 16
N_EXPERTS = 32
TOP_K = 4
D_FF = D_MODEL
SWIGLU_LIMIT = 7.0
SWIGLU_ALPHA = 1.702
MOE_BLOCK = 128
LN_EPS = 1e-5
GN_EPS = 1e-6
NEG_INF = -1e30
DEEPNORM_ALPHA = (2.0 * DEPTH) ** 0.25
DEEPNORM_BETA = (8.0 * DEPTH) ** -0.25
N_EVEN = (DEPTH + 1) // 2
N_ODD = DEPTH // 2

kernel_name = "hybrid_retnet_longnet_natten_moe_encoder"


def layer_norm(x, g, b):
    xf = x.astype(jnp.float32)
    mu = xf.mean(-1, keepdims=True)
    var = jnp.square(xf - mu).mean(-1, keepdims=True)
    return ((xf - mu) * lax.rsqrt(var + LN_EPS) * g + b).astype(x.dtype)


def head_norm(x):
    xf = x.astype(jnp.float32)
    mu = xf.mean(-1, keepdims=True)
    var = jnp.square(xf - mu).mean(-1, keepdims=True)
    return (xf - mu) * lax.rsqrt(var + GN_EPS)


def rope_tables(T):
    pos = jnp.arange(T, dtype=jnp.float32)
    inv = ROPE_THETA ** (-jnp.arange(0, HEAD_DIM, 2, dtype=jnp.float32) / HEAD_DIM)
    ang = pos[:, None] * inv[None, :]
    return jnp.cos(ang), jnp.sin(ang)


def apply_rope(x, cos, sin):
    x1, x2 = jnp.split(x, 2, axis=-1)
    c = cos[None, :, None, :].astype(x.dtype)
    s = sin[None, :, None, :].astype(x.dtype)
    return jnp.concatenate([x1 * c - x2 * s, x2 * c + x1 * s], axis=-1)


def retention_one_direction(q, k, v, log_gamma, strict):
    b_, h_, t_, d_ = q.shape
    C = RET_CHUNK
    nc = t_ // C
    qc = q.reshape(b_, h_, nc, C, d_)
    kc = k.reshape(b_, h_, nc, C, d_)
    vc = v.reshape(b_, h_, nc, C, d_)
    idx = jnp.arange(C, dtype=jnp.float32)
    diff = idx[:, None] - idx[None, :]
    keep = (diff > 0) if strict else (diff >= 0)
    decay = jnp.where(keep, jnp.exp(log_gamma[:, None, None] * jnp.maximum(diff, 0.0)), 0.0)
    scores = jnp.einsum('bhncd,bhnmd->bhncm', qc, kc) * decay[None, :, None]
    intra = jnp.einsum('bhncm,bhnme->bhnce', scores, vc)
    zeta = jnp.exp(log_gamma[:, None] * (C - 1.0 - idx)[None, :])
    chunk_state = jnp.einsum('bhnmd,hm,bhnme->nbhde', kc, zeta, vc)
    chunk_decay = jnp.exp(log_gamma * C)[None, :, None, None]

    def step(R, S):
        return chunk_decay * R + S, R

    _, r_prev = lax.scan(step, jnp.zeros_like(chunk_state[0]), chunk_state)
    xi = jnp.exp(log_gamma[:, None] * (idx + 1.0)[None, :])
    cross = jnp.einsum('bhncd,nbhde->bhnce', qc, r_prev) * xi[None, :, None, :, None]
    return (intra + cross).reshape(b_, h_, t_, d_)


def retention_mixer(q, k, v, g_f, g_b, decay_logits, cos, sin):
    b_, t_, h_, d_ = q.shape
    q = apply_rope(q, cos, sin)
    k = apply_rope(k, cos, sin) * (HEAD_DIM ** -0.5)
    qf, kf, vf = [a.astype(jnp.float32).transpose(0, 2, 1, 3) for a in (q, k, v)]
    log_gamma = jax.nn.log_sigmoid(decay_logits.astype(jnp.float32))
    o_f = retention_one_direction(qf, kf, vf, log_gamma[0], strict=False)
    o_b = jnp.flip(retention_one_direction(jnp.flip(qf, 2), jnp.flip(kf, 2), jnp.flip(vf, 2),
                                           log_gamma[1], strict=True), 2)
    o_f = head_norm(o_f.transpose(0, 2, 1, 3))
    o_b = head_norm(o_b.transpose(0, 2, 1, 3))
    y = jax.nn.silu(g_f.astype(jnp.float32)) * o_f + jax.nn.silu(g_b.astype(jnp.float32)) * o_b
    return y.reshape(b_, t_, h_ * d_).astype(q.dtype)


def dilated_branch(q, k, v, dil, half):
    b_, h_, t_, d_ = q.shape
    L = t_ // dil
    nb = -(-L // half)
    Lp = nb * half

    def classes(a):
        return a.reshape(b_, h_, L, dil, d_).transpose(0, 1, 3, 2, 4)

    qr, kr, vr = classes(q), classes(k), classes(v)
    qb = jnp.pad(qr, ((0, 0), (0, 0), (0, 0), (0, Lp - L), (0, 0))).reshape(b_, h_, dil, nb, half, d_)

    def windows(a):
        ap = jnp.pad(a, ((0, 0), (0, 0), (0, 0), (half, Lp - L + half), (0, 0)))
        ap = ap.reshape(b_, h_, dil, nb + 2, half, d_)
        return jnp.concatenate([ap[:, :, :, :-2], ap[:, :, :, 1:-1], ap[:, :, :, 2:]], axis=4)

    kw, vw = windows(kr), windows(vr)
    qi = jnp.arange(nb)[:, None, None] * half + jnp.arange(half)[None, :, None]
    ki = jnp.arange(nb)[:, None, None] * half - half + jnp.arange(3 * half)[None, None, :]
    valid = (jnp.abs(ki - qi) <= half) & (ki >= 0) & (ki < L)
    s = jnp.einsum('bhrnqd,bhrnkd->bhrnqk', qb, kw).astype(jnp.float32)
    s = jnp.where(valid, s, NEG_INF)
    m = s.max(-1, keepdims=True)
    p = jnp.exp(s - m)
    den = p.sum(-1, keepdims=True)
    o = jnp.einsum('bhrnqk,bhrnkd->bhrnqd', (p / den).astype(v.dtype), vw)
    lse = (m + jnp.log(den))[..., 0]
    o = o.reshape(b_, h_, dil, Lp, d_)[:, :, :, :L].transpose(0, 1, 3, 2, 4).reshape(b_, h_, t_, d_)
    lse = lse.reshape(b_, h_, dil, Lp)[..., :L].transpose(0, 1, 3, 2).reshape(b_, h_, t_)
    return o, lse


def dilated_mixer(q, k, v, cos, sin):
    b_, t_, h_, d_ = q.shape
    q = (apply_rope(q, cos, sin) * (HEAD_DIM ** -0.5)).transpose(0, 2, 1, 3)
    k = apply_rope(k, cos, sin).transpose(0, 2, 1, 3)
    v = v.transpose(0, 2, 1, 3)
    outs, lses = [], []
    for window, dil in DIL_CONFIGS:
        o, lse = dilated_branch(q, k, v, dil, window // (2 * dil))
        outs.append(o)
        lses.append(lse)
    wts = jax.nn.softmax(jnp.stack(lses, 0), axis=0).astype(v.dtype)
    o = jnp.einsum('gbht,gbhtd->bthd', wts, jnp.stack(outs, 0))
    return o.reshape(b_, t_, h_ * d_)


def neighbourhood_attention(q, k, v, rpb):
    b_, t_, h_, d_ = q.shape
    rows = t_ // GRID_W
    kh = min(NA_KH_MAX, rows)
    qg = q.reshape(b_, rows, GRID_W, h_, d_) * (HEAD_DIM ** -0.5)
    kg = k.reshape(b_, rows, GRID_W, h_, d_)
    vg = v.reshape(b_, rows, GRID_W, h_, d_)
    cols = jnp.arange(GRID_W)
    col_start = jnp.clip(cols - NA_KW // 2, 0, GRID_W - NA_KW)
    col_idx = col_start[:, None] + jnp.arange(NA_KW)[None, :]
    col_off = col_idx - cols[:, None] + (NA_KW - 1)
    rpb_cols = rpb[:, :, col_off]

    def one_row(r):
        r0 = jnp.clip(r - kh // 2, 0, rows - kh)
        kb = lax.dynamic_slice_in_dim(kg, r0, kh, axis=1)[:, :, col_idx]
        vb = lax.dynamic_slice_in_dim(vg, r0, kh, axis=1)[:, :, col_idx]
        row_off = r0 + jnp.arange(kh) - r + (NA_KH_MAX - 1)
        bias = rpb_cols[:, row_off].transpose(0, 2, 1, 3)
        q_r = lax.dynamic_index_in_dim(qg, r, axis=1, keepdims=False)
        s = jnp.einsum('bqhd,biqjhd->bhqij', q_r, kb).astype(jnp.float32) + bias[None]
        p = jax.nn.softmax(s.reshape(b_, h_, GRID_W, kh * NA_KW), axis=-1)
        p = p.reshape(s.shape).astype(v.dtype)
        return jnp.einsum('bhqij,biqjhd->bqhd', p, vb)

    out = lax.map(one_row, jnp.arange(rows))
    return out.transpose(1, 0, 2, 3, 4).reshape(b_, t_, h_ * d_)


def even_mixer(x, w_in, w_out, decay_logits, cos, sin):
    b_, t_, _ = x.shape
    h = jnp.einsum('btd,de->bte', x, w_in)
    cuts = [RET_W, 2 * RET_W, 3 * RET_W, 4 * RET_W, 5 * RET_W, 5 * RET_W + DIL_W, 5 * RET_W + 2 * DIL_W]
    rq, rk, rv, rgf, rgb, dq, dk, dv = jnp.split(h, cuts, axis=-1)
    ret_heads = lambda a: a.reshape(b_, t_, N_RET_HEADS, HEAD_DIM)
    dil_heads = lambda a: a.reshape(b_, t_, N_DIL_HEADS, HEAD_DIM)
    y_ret = retention_mixer(ret_heads(rq), ret_heads(rk), ret_heads(rv), ret_heads(rgf), ret_heads(rgb),
                            decay_logits, cos, sin)
    y_dil = dilated_mixer(dil_heads(dq), dil_heads(dk), dil_heads(dv), cos, sin)
    return jnp.einsum('bte,ed->btd', jnp.concatenate([y_ret, y_dil], axis=-1), w_out)


def odd_mixer(x, w_in, w_out, rpb):
    b_, t_, _ = x.shape
    h = jnp.einsum('btd,de->bte', x, w_in)
    q, k, v = [a.reshape(b_, t_, N_NA_HEADS, HEAD_DIM) for a in jnp.split(h, 3, axis=-1)]
    return jnp.einsum('bte,ed->btd', neighbourhood_attention(q, k, v, rpb), w_out)


def moe(x2d, router_w, router_b, w_up, b_up, w_down, b_down):
    n_tok, d_ = x2d.shape
    logits = (x2d @ router_w + router_b).astype(jnp.float32)
    top_v, top_i = lax.top_k(logits, TOP_K)
    gates = jax.nn.softmax(top_v, axis=-1).astype(x2d.dtype)
    n_slot = n_tok * TOP_K
    flat_e = top_i.reshape(-1).astype(jnp.int32)
    flat_g = gates.reshape(-1)
    flat_tok = jnp.arange(n_slot, dtype=jnp.int32) // TOP_K
    order = jnp.argsort(flat_e)
    se = flat_e[order]
    counts = jnp.bincount(flat_e, length=N_EXPERTS).astype(jnp.int32)
    start = jnp.cumsum(counts) - counts
    pcounts = (counts + MOE_BLOCK - 1) // MOE_BLOCK * MOE_BLOCK
    pend = jnp.cumsum(pcounts)
    pstart = pend - pcounts
    dest = pstart[se] + (jnp.arange(n_slot, dtype=jnp.int32) - start[se])
    n_pad = n_slot + N_EXPERTS * MOE_BLOCK
    n_blk = n_pad // MOE_BLOCK
    slot_tok = jnp.full((n_pad,), n_tok, dtype=jnp.int32).at[dest].set(flat_tok[order])
    slot_gate = jnp.zeros((n_pad,), x2d.dtype).at[dest].set(flat_g[order])
    blk_e = jnp.minimum(jnp.searchsorted(pend, jnp.arange(n_blk, dtype=jnp.int32) * MOE_BLOCK, side='right'),
                        N_EXPERTS - 1)
    x_pad = jnp.concatenate([x2d, jnp.zeros((1, d_), x2d.dtype)], axis=0)

    def block(args):
        tok, g, e = args
        h = x_pad[tok] @ w_up[e] + b_up[e]
        x_glu = jnp.minimum(h[:, 0::2], SWIGLU_LIMIT)
        x_lin = jnp.clip(h[:, 1::2], -SWIGLU_LIMIT, SWIGLU_LIMIT)
        act = x_glu * jax.nn.sigmoid(SWIGLU_ALPHA * x_glu) * (x_lin + 1.0)
        return (act @ w_down[e] + b_down[e]) * g[:, None]

    out = lax.map(block, (slot_tok.reshape(n_blk, MOE_BLOCK), slot_gate.reshape(n_blk, MOE_BLOCK), blk_e))
    y = jnp.zeros((n_tok + 1, d_), x2d.dtype).at[slot_tok].add(out.reshape(n_pad, d_))
    return y[:n_tok]


def setup_inputs(seed: int = 0) -> dict:
    key = jax.random.key(seed)
    ks = jax.random.split(key, 16)
    nrm = lambda k, shape, scale: jax.random.normal(k, shape, jnp.float32) * scale
    base_logit = jnp.log(2.0 ** (RET_DECAY_EXP0 + jnp.arange(N_RET_HEADS, dtype=jnp.float32)) - 1.0)
    return {
        'x': nrm(ks[0], (BATCH, SEQ, D_MODEL), 1.0),
        'ab_w_in': nrm(ks[1], (N_EVEN, D_MODEL, EVEN_IN), D_MODEL ** -0.5),
        'ab_w_out': nrm(ks[2], (N_EVEN, RET_W + DIL_W, D_MODEL), DEEPNORM_BETA * (RET_W + DIL_W) ** -0.5),
        'ret_decay': base_logit[None, None, :] + nrm(ks[3], (N_EVEN, 2, N_RET_HEADS), 0.1),
        'c_w_in': nrm(ks[4], (N_ODD, D_MODEL, ODD_IN), D_MODEL ** -0.5),
        'c_w_out': nrm(ks[5], (N_ODD, NA_W, D_MODEL), DEEPNORM_BETA * NA_W ** -0.5),
        'c_rpb': nrm(ks[6], (N_ODD, N_NA_HEADS, 2 * NA_KH_MAX - 1, 2 * NA_KW - 1), 0.02),
        'ln_g': 1.0 + nrm(ks[7], (DEPTH, 2, D_MODEL), 0.02),
        'ln_b': nrm(ks[8], (DEPTH, 2, D_MODEL), 0.02),
        'router_w': nrm(ks[9], (DEPTH, D_MODEL, N_EXPERTS), D_MODEL ** -0.5),
        'router_b': nrm(ks[10], (DEPTH, N_EXPERTS), 0.01),
        'exp_w_up': nrm(ks[11], (DEPTH, N_EXPERTS, D_MODEL, 2 * D_FF), D_MODEL ** -0.5),
        'exp_b_up': nrm(ks[12], (DEPTH, N_EXPERTS, 2 * D_FF), 0.01),
        'exp_w_down': nrm(ks[13], (DEPTH, N_EXPERTS, D_FF, D_MODEL), DEEPNORM_BETA * D_FF ** -0.5),
        'exp_b_down': nrm(ks[14], (DEPTH, N_EXPERTS, D_MODEL), 0.01),
    }


def reference(x, ab_w_in, ab_w_out, ret_decay, c_w_in, c_w_out, c_rpb, ln_g, ln_b,
              router_w, router_b, exp_w_up, exp_b_up, exp_w_down, exp_b_down):
    b_, t_, d_ = x.shape
    cos, sin = rope_tables(t_)
    for layer in range(DEPTH):
        j = layer // 2
        if layer % 2 == 0:
            mix = even_mixer(x, ab_w_in[j], ab_w_out[j], ret_decay[j], cos, sin)
        else:
            mix = odd_mixer(x, c_w_in[j], c_w_out[j], c_rpb[j])
        x = layer_norm(DEEPNORM_ALPHA * x + mix, ln_g[layer, 0], ln_b[layer, 0])
        ffn = moe(x.reshape(b_ * t_, d_), router_w[layer], router_b[layer], exp_w_up[layer],
                  exp_b_up[layer], exp_w_down[layer], exp_b_down[layer]).reshape(b_, t_, d_)
        x = layer_norm(DEEPNORM_ALPHA * x + ffn, ln_g[layer, 1], ln_b[layer, 1])
    return x
```

```python
import functools

import jax
import jax.numpy as jnp
import numpy as np
from jax import lax
from jax.experimental import pallas as pl
from jax.experimental.pallas import tpu as pltpu

F32 = jnp.float32
BF16 = jnp.bfloat16

D_MODEL = 1024
HEAD_DIM = 64
LANES = 128
HEADS_PER_GROUP = LANES // HEAD_DIM
N_RET_HEADS = 8
N_DIL_HEADS = 8
N_NA_HEADS = 16
RET_W = N_RET_HEADS * HEAD_DIM
DIL_W = N_DIL_HEADS * HEAD_DIM
EVEN_IN = 5 * RET_W + 3 * DIL_W
ODD_IN = 3 * D_MODEL
ROPE_THETA = 10000.0
DIL_CONFIGS = ((128, 1), (512, 4), (2048, 16))
DIL_HALF = 64
GRID_W = 64
NA_KH = 8
NA_KW = 16
N_EXPERTS = 32
TOP_K = 4
D_FF = D_MODEL
SWIGLU_LIMIT = 7.0
SWIGLU_ALPHA = 1.702
LN_EPS = 1e-5
GN_EPS = 1e-6
NEG_INF = -1e30
DEPTH = 4
DEEPNORM_ALPHA = (2.0 * DEPTH) ** 0.25
QK_SCALE = HEAD_DIM ** -0.5

VMEM_LIMIT_BYTES = 56 * 1024 * 1024

PROJ_TM = 512
RET_CHUNK = 256
DIL_TQ = 128
DIL_TK = DIL_TQ + 2 * DIL_HALF
NA_ROWS = 2
NA_KROWS = NA_ROWS + NA_KH - 1
ROUTER_TM = 512
MOE_BM = 256
DISPATCH_TM = 512
COMBINE_TM = 256


def _params(sem, **kw):
    return pltpu.CompilerParams(dimension_semantics=sem, vmem_limit_bytes=VMEM_LIMIT_BYTES, **kw)


def _nt_dot(a, b):
    return lax.dot_general(a, b, (((1,), (1,)), ((), ())), preferred_element_type=F32)


def _dot(a, b):
    return jnp.dot(a, b, preferred_element_type=F32)


def _layer_norm(z, g, b):
    mu = jnp.mean(z, axis=-1, keepdims=True)
    d = z - mu
    var = jnp.mean(d * d, axis=-1, keepdims=True)
    return d * lax.rsqrt(var + LN_EPS) * g + b


def _first_head_lanes(shape):
    return lax.broadcasted_iota(jnp.int32, shape, len(shape) - 1) < HEAD_DIM


def _rope_tables(t):
    pos = jnp.arange(t, dtype=F32)
    inv = ROPE_THETA ** (-jnp.arange(0, HEAD_DIM, 2, dtype=F32) / HEAD_DIM)
    ang = pos[:, None] * inv[None, :]
    cos, sin = jnp.cos(ang), jnp.sin(ang)
    return (jnp.concatenate([cos, cos, cos, cos], axis=1),
            jnp.concatenate([-sin, sin, -sin, sin], axis=1))


def _rope(h, c, s):
    half = HEAD_DIM // 2
    lane = lax.broadcasted_iota(jnp.int32, h.shape, 1)
    first_half = (lane // half) % 2 == 0
    partner = jnp.where(first_half, pltpu.roll(h, LANES - half, 1), pltpu.roll(h, half, 1))
    return h * c + partner * s


def _proj_even_kernel(x_ref, w_ref, c_ref, s_ref, o_ref):
    xb = x_ref[...].astype(BF16)
    c = c_ref[...]
    s = s_ref[...]
    rope_groups = {0: 1.0, 1: QK_SCALE, 5: QK_SCALE, 6: 1.0}
    for g in range(EVEN_IN // RET_W):
        h = _dot(xb, w_ref[:, g * RET_W:(g + 1) * RET_W])
        for j in range(RET_W // LANES):
            hj = h[:, j * LANES:(j + 1) * LANES]
            if g in rope_groups:
                hj = _rope(hj, c, s) * rope_groups[g]
            col = g * RET_W + j * LANES
            o_ref[:, col:col + LANES] = hj.astype(BF16)


def _proj_even(x, w_bf, cos_t, sin_t):
    t = x.shape[0]
    tm = PROJ_TM
    return pl.pallas_call(
        _proj_even_kernel,
        out_shape=jax.ShapeDtypeStruct((t, EVEN_IN), BF16),
        grid=(t // tm,),
        in_specs=[
            pl.BlockSpec((tm, D_MODEL), lambda i: (i, 0)),
            pl.BlockSpec((D_MODEL, EVEN_IN), lambda i: (0, 0)),
            pl.BlockSpec((tm, LANES), lambda i: (i, 0)),
            pl.BlockSpec((tm, LANES), lambda i: (i, 0)),
        ],
        out_specs=pl.BlockSpec((tm, EVEN_IN), lambda i: (i, 0)),
        compiler_params=_params(("parallel",)),
        name="proj_even",
    )(x, w_bf, cos_t, sin_t)


def _proj_odd_kernel(x_ref, w_ref, o_ref):
    xb = x_ref[...].astype(BF16)
    for g in range(3):
        h = _dot(xb, w_ref[:, g * D_MODEL:(g + 1) * D_MODEL])
        if g == 0:
            h = h * QK_SCALE
        o_ref[:, g * D_MODEL:(g + 1) * D_MODEL] = h.astype(BF16)


def _proj_odd(x, w_bf):
    t = x.shape[0]
    tm = PROJ_TM
    return pl.pallas_call(
        _proj_odd_kernel,
        out_shape=jax.ShapeDtypeStruct((t, ODD_IN), BF16),
        grid=(t // tm,),
        in_specs=[
            pl.BlockSpec((tm, D_MODEL), lambda i: (i, 0)),
            pl.BlockSpec((D_MODEL, ODD_IN), lambda i: (0, 0)),
        ],
        out_specs=pl.BlockSpec((tm, ODD_IN), lambda i: (i, 0)),
        compiler_params=_params(("parallel",)),
        name="proj_odd",
    )(x, w_bf)


def _ret_kernel(lg_ref, q_ref, k_ref, v_ref, g_ref, *rest, rev):
    if rev:
        o_ref, r_ref, d0_ref, d1_ref, z_ref, xi_ref, dc_ref = rest
        prev_ref = None
    else:
        prev_ref, o_ref, r_ref, d0_ref, d1_ref, z_ref, xi_ref, dc_ref = rest
    C = RET_CHUNK
    p = pl.program_id(0)
    c = pl.program_id(1)
    first = _first_head_lanes((1, LANES))
    direction = 1 if rev else 0

    @pl.when(c == 0)
    def _init():
        lg0 = lg_ref[direction, 2 * p]
        lg1 = lg_ref[direction, 2 * p + 1]
        row = lax.broadcasted_iota(jnp.int32, (C, C), 0)
        col = lax.broadcasted_iota(jnp.int32, (C, C), 1)
        dist = (col - row) if rev else (row - col)
        keep = (dist > 0) if rev else (dist >= 0)
        distf = jnp.maximum(dist, 0).astype(F32)
        d0_ref[...] = jnp.where(keep, jnp.exp(lg0 * distf), 0.0)
        d1_ref[...] = jnp.where(keep, jnp.exp(lg1 * distf), 0.0)
        lgl = jnp.where(first, lg0, lg1)
        idx = lax.broadcasted_iota(jnp.int32, (C, LANES), 0).astype(F32)
        z_ref[...] = jnp.exp(lgl * (idx if rev else (C - 1.0 - idx)))
        xi_ref[...] = jnp.exp(lgl * ((C - idx) if rev else (idx + 1.0)))
        dc_ref[...] = jnp.exp(lgl * float(C))
        r_ref[...] = jnp.zeros_like(r_ref)

    q2 = q_ref[...]
    k2 = k_ref[...]
    v2 = v_ref[...]
    zero = jnp.zeros_like(q2)
    q0 = jnp.where(first, q2, zero)
    q1 = jnp.where(first, zero, q2)
    a0 = (_nt_dot(q0, k2) * d0_ref[...]).astype(BF16)
    a1 = (_nt_dot(q1, k2) * d1_ref[...]).astype(BF16)
    state = r_ref[...]
    cross = _dot(q2, state.astype(BF16)) * xi_ref[...]
    o = jnp.where(first, _dot(a0, v2), _dot(a1, v2)) + cross

    kz = (k2.astype(F32) * z_ref[...]).T.astype(BF16)
    upd = _dot(kz, v2)
    rr = lax.broadcasted_iota(jnp.int32, (LANES, LANES), 0) < HEAD_DIM
    cc = lax.broadcasted_iota(jnp.int32, (LANES, LANES), 1) < HEAD_DIM
    r_ref[...] = jnp.where(rr == cc, state * dc_ref[...] + upd, 0.0)

    inv_n = 1.0 / HEAD_DIM
    s_first = jnp.sum(jnp.where(first, o, 0.0), axis=-1, keepdims=True)
    s_second = jnp.sum(jnp.where(first, 0.0, o), axis=-1, keepdims=True)
    d = o - jnp.where(first, s_first, s_second) * inv_n
    dd = d * d
    v_first = jnp.sum(jnp.where(first, dd, 0.0), axis=-1, keepdims=True)
    v_second = jnp.sum(jnp.where(first, 0.0, dd), axis=-1, keepdims=True)
    var = jnp.where(first, v_first, v_second) * inv_n
    gate = g_ref[...].astype(F32)
    y = gate * jax.nn.sigmoid(gate) * (d * lax.rsqrt(var + GN_EPS))
    if prev_ref is not None:
        y = y + prev_ref[...].astype(F32)
    o_ref[...] = y.astype(o_ref.dtype)


def _retention_pass(h, log_gamma, prev, rev):
    t = h.shape[0]
    C = RET_CHUNK
    nc = t // C
    groups = RET_W // LANES

    def row(c):
        return (nc - 1 - c) if rev else c

    def col_spec(base):
        return pl.BlockSpec((C, LANES), lambda p, c, base=base: (row(c), base + p))

    gate_base = (4 if rev else 3) * groups
    in_specs = [
        pl.BlockSpec(memory_space=pltpu.SMEM),
        col_spec(0), col_spec(groups), col_spec(2 * groups), col_spec(gate_base),
    ]
    args = [log_gamma, h, h, h, h]
    if not rev:
        in_specs.append(col_spec(0))
        args.append(prev)
    return pl.pallas_call(
        functools.partial(_ret_kernel, rev=rev),
        out_shape=jax.ShapeDtypeStruct((t, RET_W), BF16),
        grid=(groups, nc),
        in_specs=in_specs,
        out_specs=col_spec(0),
        scratch_shapes=[
            pltpu.VMEM((LANES, LANES), F32),
            pltpu.VMEM((C, C), F32),
            pltpu.VMEM((C, C), F32),
            pltpu.VMEM((C, LANES), F32),
            pltpu.VMEM((C, LANES), F32),
            pltpu.VMEM((1, LANES), F32),
        ],
        compiler_params=_params(("parallel", "arbitrary")),
        name="retention_bwd" if rev else "retention_fwd",
    )(*args)


def _retention(h, decay_logits):
    log_gamma = jax.nn.log_sigmoid(decay_logits.astype(F32))
    y_bwd = _retention_pass(h, log_gamma, None, rev=True)
    return _retention_pass(h, log_gamma, y_bwd, rev=False)


def _dil_kernel(q_ref, k_ref, v_ref, o_ref, lse_ref, *, length):
    i = pl.program_id(2)
    start = jnp.clip(i * DIL_TQ - DIL_HALF, 0, length - DIL_TK)
    start = pl.multiple_of(start, DIL_HALF)
    kw = k_ref[pl.ds(start, DIL_TK), :]
    vw = v_ref[pl.ds(start, DIL_TK), :]
    qpos = i * DIL_TQ + lax.broadcasted_iota(jnp.int32, (DIL_TQ, DIL_TK), 0)
    kpos = start + lax.broadcasted_iota(jnp.int32, (DIL_TQ, DIL_TK), 1)
    valid = jnp.abs(kpos - qpos) <= DIL_HALF
    first = _first_head_lanes((1, LANES))
    q2 = q_ref[...]
    zero = jnp.zeros_like(q2)

    def one_head(qh):
        s = jnp.where(valid, _nt_dot(qh, kw), NEG_INF)
        m = jnp.max(s, axis=-1, keepdims=True)
        e = jnp.exp(s - m)
        den = jnp.sum(e, axis=-1, keepdims=True)
        return _dot(e.astype(BF16), vw) / den, m + jnp.log(den)

    o0, l0 = one_head(jnp.where(first, q2, zero))
    o1, l1 = one_head(jnp.where(first, zero, q2))
    o_ref[...] = jnp.where(first, o0, o1).astype(o_ref.dtype)
    lse_ref[...] = jnp.where(first, l0, l1)


def _dilated_branch(h, dil):
    t = h.shape[0]
    length = t // dil
    groups = DIL_W // LANES
    cols = EVEN_IN // LANES
    q_base = 5 * RET_W // LANES
    hv = h.reshape(length, dil * EVEN_IN)
    o, lse = pl.pallas_call(
        functools.partial(_dil_kernel, length=length),
        out_shape=(jax.ShapeDtypeStruct((length, dil * DIL_W), BF16),
                   jax.ShapeDtypeStruct((length, dil * DIL_W), F32)),
        grid=(dil, groups, length // DIL_TQ),
        in_specs=[
            pl.BlockSpec((DIL_TQ, LANES), lambda r, p, i: (i, r * cols + q_base + p)),
            pl.BlockSpec((length, LANES), lambda r, p, i: (0, r * cols + q_base + groups + p)),
            pl.BlockSpec((length, LANES), lambda r, p, i: (0, r * cols + q_base + 2 * groups + p)),
        ],
        out_specs=(
            pl.BlockSpec((DIL_TQ, LANES), lambda r, p, i: (i, r * groups + p)),
            pl.BlockSpec((DIL_TQ, LANES), lambda r, p, i: (i, r * groups + p)),
        ),
        compiler_params=_params(("parallel", "parallel", "parallel")),
        name=f"dilated_attn_d{dil}",
    )(hv, hv, hv)
    return o.reshape(t, DIL_W), lse.reshape(t, DIL_W)


def _out_even_kernel(x_ref, yr_ref, o1_ref, o2_ref, o3_ref, l1_ref, l2_ref, l3_ref,
                     w_ref, g_ref, b_ref, out_ref):
    l1, l2, l3 = l1_ref[...], l2_ref[...], l3_ref[...]
    m = jnp.maximum(jnp.maximum(l1, l2), l3)
    e1, e2, e3 = jnp.exp(l1 - m), jnp.exp(l2 - m), jnp.exp(l3 - m)
    den = e1 + e2 + e3
    y_dil = (e1 / den) * o1_ref[...].astype(F32) + (e2 / den) * o2_ref[...].astype(F32) \
        + (e3 / den) * o3_ref[...].astype(F32)
    mix = _dot(yr_ref[...], w_ref[:RET_W, :]) + _dot(y_dil.astype(BF16), w_ref[RET_W:, :])
    z = DEEPNORM_ALPHA * x_ref[...] + mix
    out_ref[...] = _layer_norm(z, g_ref[...], b_ref[...])


def _out_even(x, y_ret, outs, lses, w_bf, g, b):
    t = x.shape[0]
    tm = PROJ_TM
    half = pl.BlockSpec((tm, RET_W), lambda i: (i, 0))
    vec = pl.BlockSpec((1, D_MODEL), lambda i: (0, 0))
    return pl.pallas_call(
        _out_even_kernel,
        out_shape=jax.ShapeDtypeStruct((t, D_MODEL), F32),
        grid=(t // tm,),
        in_specs=[pl.BlockSpec((tm, D_MODEL), lambda i: (i, 0))] + [half] * 7
        + [pl.BlockSpec((D_MODEL, D_MODEL), lambda i: (0, 0)), vec, vec],
        out_specs=pl.BlockSpec((tm, D_MODEL), lambda i: (i, 0)),
        compiler_params=_params(("parallel",)),
        name="out_proj_even",
    )(x, y_ret, *outs, *lses, w_bf, g.reshape(1, -1), b.reshape(1, -1))


def _out_odd_kernel(x_ref, y_ref, w_ref, g_ref, b_ref, out_ref):
    z = DEEPNORM_ALPHA * x_ref[...] + _dot(y_ref[...], w_ref[...])
    out_ref[...] = _layer_norm(z, g_ref[...], b_ref[...])


def _out_odd(x, y, w_bf, g, b):
    t = x.shape[0]
    tm = PROJ_TM
    full = pl.BlockSpec((tm, D_MODEL), lambda i: (i, 0))
    vec = pl.BlockSpec((1, D_MODEL), lambda i: (0, 0))
    return pl.pallas_call(
        _out_odd_kernel,
        out_shape=jax.ShapeDtypeStruct((t, D_MODEL), F32),
        grid=(t // tm,),
        in_specs=[full, full, pl.BlockSpec((D_MODEL, D_MODEL), lambda i: (0, 0)), vec, vec],
        out_specs=full,
        compiler_params=_params(("parallel",)),
        name="out_proj_odd",
    )(x, y, w_bf, g.reshape(1, -1), b.reshape(1, -1))


def _even_mixer(x, w_in, w_out, decay_logits, g, b, cos_t, sin_t):
    h = _proj_even(x, w_in.astype(BF16), cos_t, sin_t)
    y_ret = _retention(h, decay_logits)
    outs, lses = [], []
    for _, dil in DIL_CONFIGS:
        o, lse = _dilated_branch(h, dil)
        outs.append(o)
        lses.append(lse)
    return _out_even(x, y_ret, outs, lses, w_out.astype(BF16), g, b)


def _na_bias_table(rpb, rows):
    n_blk = rows // NA_ROWS
    blocks = np.array([0, 1, 2, n_blk - 2, n_blk - 1])
    kstart = np.clip(NA_ROWS * blocks - NA_KH // 2, 0, rows - NA_KROWS)
    r = NA_ROWS * blocks[:, None] + np.arange(NA_ROWS)[None, :]
    r0 = np.clip(r - NA_KH // 2, 0, rows - NA_KH)
    kr = kstart[:, None] + np.arange(NA_KROWS)[None, :]
    row_ok = (kr[:, None, :] >= r0[:, :, None]) & (kr[:, None, :] < r0[:, :, None] + NA_KH)
    dr = np.clip(kr[:, None, :] - r[:, :, None] + (NA_KH - 1), 0, 2 * NA_KH - 2)
    c = np.arange(GRID_W)
    c0 = np.clip(c - NA_KW // 2, 0, GRID_W - NA_KW)
    kc = np.arange(GRID_W)
    col_ok = (kc[None, :] >= c0[:, None]) & (kc[None, :] < c0[:, None] + NA_KW)
    dc = np.clip(kc[None, :] - c[:, None] + (NA_KW - 1), 0, 2 * NA_KW - 2)
    shape = (5, NA_ROWS, GRID_W, NA_KROWS, GRID_W)
    dr_full = np.broadcast_to(dr[:, :, None, :, None], shape)
    dc_full = np.broadcast_to(dc[None, None, :, None, :], shape)
    ok = np.broadcast_to(row_ok[:, :, None, :, None] & col_ok[None, None, :, None, :], shape)
    bias = rpb[:, dr_full, dc_full]
    bias = jnp.where(ok[None], bias, NEG_INF)
    bias = bias.reshape(N_NA_HEADS, 5, NA_ROWS * GRID_W, NA_KROWS * GRID_W)
    return bias.transpose(1, 0, 2, 3).astype(F32)


def _na_kernel(q_ref, k_ref, v_ref, bias_ref, o_ref, *, rows):
    rb = pl.program_id(1)
    kstart = jnp.clip(NA_ROWS * rb - NA_KH // 2, 0, rows - NA_KROWS) * GRID_W
    kstart = pl.multiple_of(kstart, GRID_W)
    nk = NA_KROWS * GRID_W
    kw = k_ref[pl.ds(kstart, nk), :]
    vw = v_ref[pl.ds(kstart, nk), :]
    first = _first_head_lanes((1, LANES))
    q2 = q_ref[...]
    zero = jnp.zeros_like(q2)

    def one_head(qh, bias):
        s = _nt_dot(qh, kw) + bias
        m = jnp.max(s, axis=-1, keepdims=True)
        e = jnp.exp(s - m)
        den = jnp.sum(e, axis=-1, keepdims=True)
        return _dot(e.astype(BF16), vw) / den

    o0 = one_head(jnp.where(first, q2, zero), bias_ref[0, 0])
    o1 = one_head(jnp.where(first, zero, q2), bias_ref[0, 1])
    o_ref[...] = jnp.where(first, o0, o1).astype(o_ref.dtype)


def _neighbourhood_attention(h, rpb):
    t = h.shape[0]
    rows = t // GRID_W
    n_blk = rows // NA_ROWS
    groups = D_MODEL // LANES
    tq = NA_ROWS * GRID_W
    nk = NA_KROWS * GRID_W
    bias = _na_bias_table(rpb, rows)

    def pattern(rb):
        return jnp.clip(rb, 0, 2) + jnp.clip(rb - (n_blk - 3), 0, 2)

    return pl.pallas_call(
        functools.partial(_na_kernel, rows=rows),
        out_shape=jax.ShapeDtypeStruct((t, D_MODEL), BF16),
        grid=(groups, n_blk),
        in_specs=[
            pl.BlockSpec((tq, LANES), lambda p, rb: (rb, p)),
            pl.BlockSpec((t, LANES), lambda p, rb: (0, groups + p)),
            pl.BlockSpec((t, LANES), lambda p, rb: (0, 2 * groups + p)),
            pl.BlockSpec((1, HEADS_PER_GROUP, tq, nk), lambda p, rb: (pattern(rb), p, 0, 0)),
        ],
        out_specs=pl.BlockSpec((tq, LANES), lambda p, rb: (rb, p)),
        compiler_params=_params(("parallel", "parallel")),
        name="neighbourhood_attn",
    )(h, h, h, bias)


def _odd_mixer(x, w_in, w_out, rpb, g, b):
    h = _proj_odd(x, w_in.astype(BF16))
    y = _neighbourhood_attention(h, rpb)
    return _out_odd(x, y, w_out.astype(BF16), g, b)


def _router_kernel(x_ref, wt_ref, b_ref, idx_ref, gate_ref):
    logits = lax.dot_general(wt_ref[...], x_ref[...], (((1,), (1,)), ((), ())),
                             precision=lax.Precision.HIGHEST, preferred_element_type=F32)
    logits = logits + b_ref[...]
    expert = lax.broadcasted_iota(jnp.int32, logits.shape, 0)
    vals, idxs = [], []
    for _ in range(TOP_K):
        m = jnp.max(logits, axis=0, keepdims=True)
        sel = jnp.min(jnp.where(logits == m, expert, N_EXPERTS), axis=0, keepdims=True)
        vals.append(m)
        idxs.append(sel)
        logits = jnp.where(expert == sel, -jnp.inf, logits)
    exps = [jnp.exp(v - vals[0]) for v in vals]
    den = exps[0] + exps[1] + exps[2] + exps[3]
    for k in range(TOP_K):
        idx_ref[k:k + 1, :] = idxs[k]
        gate_ref[k:k + 1, :] = exps[k] / den


def _router(x, router_w, router_b):
    t = x.shape[0]
    tm = ROUTER_TM
    return pl.pallas_call(
        _router_kernel,
        out_shape=(jax.ShapeDtypeStruct((TOP_K, t), jnp.int32),
                   jax.ShapeDtypeStruct((TOP_K, t), F32)),
        grid=(t // tm,),
        in_specs=[
            pl.BlockSpec((tm, D_MODEL), lambda i: (i, 0)),
            pl.BlockSpec((N_EXPERTS, D_MODEL), lambda i: (0, 0)),
            pl.BlockSpec((N_EXPERTS, 1), lambda i: (0, 0)),
        ],
        out_specs=(pl.BlockSpec((TOP_K, tm), lambda i: (0, i)),
                   pl.BlockSpec((TOP_K, tm), lambda i: (0, i))),
        compiler_params=_params(("parallel",)),
        name="moe_router",
    )(x, router_w.T, router_b.reshape(-1, 1))


def _slot_layout(top_i):
    t = top_i.shape[1]
    bm = MOE_BM
    onehot = (top_i[:, :, None] == jnp.arange(N_EXPERTS, dtype=jnp.int32)).astype(jnp.int32)
    per_tok = onehot.sum(0)
    before = jnp.cumsum(per_tok, axis=0) - per_tok
    rank = (onehot * before[None]).sum(-1)
    counts = per_tok.sum(0)
    pcounts = (counts + bm - 1) // bm * bm
    pend = jnp.cumsum(pcounts)
    pstart = pend - pcounts
    pos = pstart[top_i] + rank
    n_blk = (TOP_K * t) // bm + N_EXPERTS
    blk_e = jnp.minimum(jnp.searchsorted(pend, jnp.arange(n_blk, dtype=jnp.int32) * bm, side='right'),
                        N_EXPERTS - 1).astype(jnp.int32)
    n_used = (pend[-1] // bm).astype(jnp.int32).reshape(1)
    return pos.astype(jnp.int32), blk_e, n_used, n_blk


def _row_copy(src, src_row, dst, dst_row, sem):
    return pltpu.make_async_copy(src.at[pl.ds(src_row, 1)], dst.at[pl.ds(dst_row, 1)], sem)


def _dispatch_kernel(pos_ref, x_hbm, slots_in_hbm, slots_hbm, sem, *, n_tok, tm):
    del slots_in_hbm
    base = pl.program_id(0) * tm

    def issue(j, carry):
        tok = base + j
        for k in range(TOP_K):
            _row_copy(x_hbm, tok, slots_hbm, pos_ref[k * n_tok + tok], sem).start()
        return carry

    lax.fori_loop(0, tm, issue, 0)
    pltpu.make_async_copy(x_hbm.at[pl.ds(0, TOP_K * tm)], slots_hbm.at[pl.ds(0, TOP_K * tm)], sem).wait()


def _dispatch(x, pos_flat, n_slots):
    t = x.shape[0]
    tm = DISPATCH_TM
    return pl.pallas_call(
        functools.partial(_dispatch_kernel, n_tok=t, tm=tm),
        out_shape=jax.ShapeDtypeStruct((n_slots, D_MODEL), F32),
        grid_spec=pltpu.PrefetchScalarGridSpec(
            num_scalar_prefetch=1,
            grid=(t // tm,),
            in_specs=[pl.BlockSpec(memory_space=pl.ANY), pl.BlockSpec(memory_space=pl.ANY)],
            out_specs=pl.BlockSpec(memory_space=pl.ANY),
            scratch_shapes=[pltpu.SemaphoreType.DMA],
        ),
        input_output_aliases={2: 0},
        compiler_params=_params(("arbitrary",), has_side_effects=True),
        name="moe_dispatch",
    )(pos_flat, x, jnp.zeros((n_slots, D_MODEL), F32))


def _expert_kernel(blk_e_ref, n_used_ref, x_ref, wg_ref, wl_ref, bg_ref, bl_ref, wd_ref, bd_ref, o_ref):
    del blk_e_ref
    b = pl.program_id(0)

    @pl.when(b < n_used_ref[0])
    def _compute():
        xb = x_ref[...].astype(BF16)
        x_glu = jnp.minimum(_dot(xb, wg_ref[...]) + bg_ref[...], SWIGLU_LIMIT)
        x_lin = jnp.clip(_dot(xb, wl_ref[...]) + bl_ref[...], -SWIGLU_LIMIT, SWIGLU_LIMIT)
        act = x_glu * jax.nn.sigmoid(SWIGLU_ALPHA * x_glu) * (x_lin + 1.0)
        o_ref[...] = _dot(act.astype(BF16), wd_ref[...]) + bd_ref[...]

    @pl.when(b >= n_used_ref[0])
    def _unused():
        o_ref[...] = jnp.zeros_like(o_ref)


def _experts(slots, blk_e, n_used, n_blk, w_glu, w_lin, b_glu, b_lin, w_down, b_down):
    bm = MOE_BM

    def wspec(shape):
        return pl.BlockSpec((None,) + shape, lambda b, e, n: (e[b],) + (0,) * len(shape))

    return pl.pallas_call(
        _expert_kernel,
        out_shape=jax.ShapeDtypeStruct(slots.shape, F32),
        grid_spec=pltpu.PrefetchScalarGridSpec(
            num_scalar_prefetch=2,
            grid=(n_blk,),
            in_specs=[
                pl.BlockSpec((bm, D_MODEL), lambda b, e, n: (jnp.minimum(b, n[0] - 1), 0)),
                wspec((D_MODEL, D_FF)), wspec((D_MODEL, D_FF)),
                wspec((1, D_FF)), wspec((1, D_FF)),
                wspec((D_FF, D_MODEL)), wspec((1, D_MODEL)),
            ],
            out_specs=pl.BlockSpec((bm, D_MODEL), lambda b, e, n: (b, 0)),
        ),
        compiler_params=_params(("arbitrary",)),
        name="moe_experts",
    )(blk_e, n_used, slots, w_glu, w_lin, b_glu, b_lin, w_down, b_down)


def _combine_kernel(pos_ref, y_hbm, x_ref, gate_ref, g_ref, b_ref, o_ref, buf_ref, sem, *, n_tok, tm):
    base = pl.program_id(0) * tm

    def issue(j, carry):
        tok = base + j
        for k in range(TOP_K):
            _row_copy(y_hbm, pos_ref[k * n_tok + tok], buf_ref.at[k], j, sem).start()
        return carry

    lax.fori_loop(0, tm, issue, 0)
    for k in range(TOP_K):
        pltpu.make_async_copy(y_hbm.at[pl.ds(0, tm)], buf_ref.at[k], sem).wait()
    gates = gate_ref[...]
    ffn = gates[:, 0:1] * buf_ref[0]
    for k in range(1, TOP_K):
        ffn = ffn + gates[:, k:k + 1] * buf_ref[k]
    z = DEEPNORM_ALPHA * x_ref[...] + ffn
    o_ref[...] = _layer_norm(z, g_ref[...], b_ref[...])


def _combine(x, y_slots, pos_flat, gates_tk, g, b):
    t = x.shape[0]
    tm = COMBINE_TM
    full = pl.BlockSpec((tm, D_MODEL), lambda i, p: (i, 0))
    vec = pl.BlockSpec((1, D_MODEL), lambda i, p: (0, 0))
    return pl.pallas_call(
        functools.partial(_combine_kernel, n_tok=t, tm=tm),
        out_shape=jax.ShapeDtypeStruct((t, D_MODEL), F32),
        grid_spec=pltpu.PrefetchScalarGridSpec(
            num_scalar_prefetch=1,
            grid=(t // tm,),
            in_specs=[pl.BlockSpec(memory_space=pl.ANY), full,
                      pl.BlockSpec((tm, TOP_K), lambda i, p: (i, 0)), vec, vec],
            out_specs=full,
            scratch_shapes=[pltpu.VMEM((TOP_K, tm, D_MODEL), F32), pltpu.SemaphoreType.DMA],
        ),
        compiler_params=_params(("arbitrary",)),
        name="moe_combine",
    )(pos_flat, y_slots, x, gates_tk, g.reshape(1, -1), b.reshape(1, -1))


def _moe_layer(x, router_w, router_b, w_up, b_up, w_down, b_down, g, b):
    top_i, gates = _router(x, router_w, router_b)
    pos, blk_e, n_used, n_blk = _slot_layout(top_i)
    pos_flat = pos.reshape(-1)
    slots = _dispatch(x, pos_flat, n_blk * MOE_BM)
    w_glu = w_up[:, :, 0::2].astype(BF16)
    w_lin = w_up[:, :, 1::2].astype(BF16)
    b_glu = b_up[:, None, 0::2]
    b_lin = b_up[:, None, 1::2]
    y_slots = _experts(slots, blk_e, n_used, n_blk, w_glu, w_lin, b_glu, b_lin,
                       w_down.astype(BF16), b_down[:, None, :])
    return _combine(x, y_slots, pos_flat, gates.T, g, b)


def kernel(x, ab_w_in, ab_w_out, ret_decay, c_w_in, c_w_out, c_rpb, ln_g, ln_b,
           router_w, router_b, exp_w_up, exp_b_up, exp_w_down, exp_b_down):
    b_, t_, d_ = x.shape
    assert b_ == 1 and d_ == D_MODEL
    xs = x.reshape(t_, d_)
    cos_t, sin_t = _rope_tables(t_)
    for layer in range(DEPTH):
        j = layer // 2
        if layer % 2 == 0:
            xs = _even_mixer(xs, ab_w_in[j], ab_w_out[j], ret_decay[j], ln_g[layer, 0], ln_b[layer, 0],
                             cos_t, sin_t)
        else:
            xs = _odd_mixer(xs, c_w_in[j], c_w_out[j], c_rpb[j], ln_g[layer, 0], ln_b[layer, 0])
        xs = _moe_layer(xs, router_w[layer], router_b[layer], exp_w_up[layer], exp_b_up[layer],
                        exp_w_down[layer], exp_b_down[layer], ln_g[layer, 1], ln_b[layer, 1])
    return xs.reshape(b_, t_, d_)
```

```python
import functools

import jax
import jax.numpy as jnp
import numpy as np
from jax import lax
from jax.experimental import pallas as pl
from jax.experimental.pallas import tpu as pltpu

F32 = jnp.float32
BF16 = jnp.bfloat16

D_MODEL = 1024
HEAD_DIM = 64
LANES = 128
HEADS_PER_GROUP = LANES // HEAD_DIM
N_RET_HEADS = 8
N_DIL_HEADS = 8
N_NA_HEADS = 16
RET_W = N_RET_HEADS * HEAD_DIM
DIL_W = N_DIL_HEADS * HEAD_DIM
EVEN_IN = 5 * RET_W + 3 * DIL_W
ODD_IN = 3 * D_MODEL
ROPE_THETA = 10000.0
DIL_CONFIGS = ((128, 1), (512, 4), (2048, 16))
DIL_HALF = 64
GRID_W = 64
NA_KH = 8
NA_KW = 16
N_EXPERTS = 32
TOP_K = 4
D_FF = D_MODEL
SWIGLU_LIMIT = 7.0
SWIGLU_ALPHA = 1.702
LN_EPS = 1e-5
GN_EPS = 1e-6
NEG_INF = -1e30
DEPTH = 4
DEEPNORM_ALPHA = (2.0 * DEPTH) ** 0.25
QK_SCALE = HEAD_DIM ** -0.5

VMEM_LIMIT_BYTES = 56 * 1024 * 1024

PROJ_TM = 512
RET_CHUNK = 256
DIL_TQ = 128
DIL_TK = DIL_TQ + 2 * DIL_HALF
NA_ROWS = 2
NA_KROWS = NA_ROWS + NA_KH - 1
ROUTER_TM = 512
MOE_BM = 256
DISPATCH_TM = 512
COMBINE_TM = 256


def _params(sem, **kw):
    return pltpu.CompilerParams(dimension_semantics=sem, vmem_limit_bytes=VMEM_LIMIT_BYTES, **kw)


def _nt_dot(a, b):
    return lax.dot_general(a, b, (((1,), (1,)), ((), ())), preferred_element_type=F32)


def _dot(a, b):
    return jnp.dot(a, b, preferred_element_type=F32)


def _layer_norm(z, g, b):
    mu = jnp.mean(z, axis=-1, keepdims=True)
    d = z - mu
    var = jnp.mean(d * d, axis=-1, keepdims=True)
    return d * lax.rsqrt(var + LN_EPS) * g + b


def _first_head_lanes(shape):
    return lax.broadcasted_iota(jnp.int32, shape, len(shape) - 1) < HEAD_DIM


def _rope_tables(t):
    pos = jnp.arange(t, dtype=F32)
    inv = ROPE_THETA ** (-jnp.arange(0, HEAD_DIM, 2, dtype=F32) / HEAD_DIM)
    ang = pos[:, None] * inv[None, :]
    cos, sin = jnp.cos(ang), jnp.sin(ang)
    return (jnp.concatenate([cos, cos, cos, cos], axis=1),
            jnp.concatenate([-sin, sin, -sin, sin], axis=1))


def _rope(h, c, s):
    half = HEAD_DIM // 2
    lane = lax.broadcasted_iota(jnp.int32, h.shape, 1)
    first_half = (lane // half) % 2 == 0
    partner = jnp.where(first_half, pltpu.roll(h, LANES - half, 1), pltpu.roll(h, half, 1))
    return h * c + partner * s


def _proj_even_kernel(x_ref, w_ref, c_ref, s_ref, o_ref):
    xb = x_ref[...].astype(BF16)
    c = c_ref[...]
    s = s_ref[...]
    rope_groups = {0: 1.0, 1: QK_SCALE, 5: QK_SCALE, 6: 1.0}
    for g in range(EVEN_IN // RET_W):
        h = _dot(xb, w_ref[:, g * RET_W:(g + 1) * RET_W])
        for j in range(RET_W // LANES):
            hj = h[:, j * LANES:(j + 1) * LANES]
            if g in rope_groups:
                hj = _rope(hj, c, s) * rope_groups[g]
            col = g * RET_W + j * LANES
            o_ref[:, col:col + LANES] = hj.astype(BF16)


def _proj_even(x, w_bf, cos_t, sin_t):
    t = x.shape[0]
    tm = PROJ_TM
    return pl.pallas_call(
        _proj_even_kernel,
        out_shape=jax.ShapeDtypeStruct((t, EVEN_IN), BF16),
        grid=(t // tm,),
        in_specs=[
            pl.BlockSpec((tm, D_MODEL), lambda i: (i, 0)),
            pl.BlockSpec((D_MODEL, EVEN_IN), lambda i: (0, 0)),
            pl.BlockSpec((tm, LANES), lambda i: (i, 0)),
            pl.BlockSpec((tm, LANES), lambda i: (i, 0)),
        ],
        out_specs=pl.BlockSpec((tm, EVEN_IN), lambda i: (i, 0)),
        compiler_params=_params(("parallel",)),
        name="proj_even",
    )(x, w_bf, cos_t, sin_t)


def _proj_odd_kernel(x_ref, w_ref, o_ref):
    xb = x_ref[...].astype(BF16)
    for g in range(3):
        h = _dot(xb, w_ref[:, g * D_MODEL:(g + 1) * D_MODEL])
        if g == 0:
            h = h * QK_SCALE
        o_ref[:, g * D_MODEL:(g + 1) * D_MODEL] = h.astype(BF16)


def _proj_odd(x, w_bf):
    t = x.shape[0]
    tm = PROJ_TM
    return pl.pallas_call(
        _proj_odd_kernel,
        out_shape=jax.ShapeDtypeStruct((t, ODD_IN), BF16),
        grid=(t // tm,),
        in_specs=[
            pl.BlockSpec((tm, D_MODEL), lambda i: (i, 0)),
            pl.BlockSpec((D_MODEL, ODD_IN), lambda i: (0, 0)),
        ],
        out_specs=pl.BlockSpec((tm, ODD_IN), lambda i: (i, 0)),
        compiler_params=_params(("parallel",)),
        name="proj_odd",
    )(x, w_bf)


def _ret_kernel(lg_ref, q_ref, k_ref, v_ref, g_ref, *rest, rev):
    if rev:
        o_ref, r_ref, d0_ref, d1_ref, z_ref, xi_ref, dc_ref = rest
        prev_ref = None
    else:
        prev_ref, o_ref, r_ref, d0_ref, d1_ref, z_ref, xi_ref, dc_ref = rest
    C = RET_CHUNK
    p = pl.program_id(0)
    c = pl.program_id(1)
    first = _first_head_lanes((1, LANES))
    direction = 1 if rev else 0

    @pl.when(c == 0)
    def _init():
        lg0 = lg_ref[direction, 2 * p]
        lg1 = lg_ref[direction, 2 * p + 1]
        row = lax.broadcasted_iota(jnp.int32, (C, C), 0)
        col = lax.broadcasted_iota(jnp.int32, (C, C), 1)
        dist = (col - row) if rev else (row - col)
        keep = (dist > 0) if rev else (dist >= 0)
        distf = jnp.maximum(dist, 0).astype(F32)
        d0_ref[...] = jnp.where(keep, jnp.exp(lg0 * distf), 0.0)
        d1_ref[...] = jnp.where(keep, jnp.exp(lg1 * distf), 0.0)
        lgl = jnp.where(first, lg0, lg1)
        idx = lax.broadcasted_iota(jnp.int32, (C, LANES), 0).astype(F32)
        z_ref[...] = jnp.exp(lgl * (idx if rev else (C - 1.0 - idx)))
        xi_ref[...] = jnp.exp(lgl * ((C - idx) if rev else (idx + 1.0)))
        dc_ref[...] = jnp.exp(lgl * float(C))
        r_ref[...] = jnp.zeros_like(r_ref)

    q2 = q_ref[...]
    k2 = k_ref[...]
    v2 = v_ref[...]
    zero = jnp.zeros_like(q2)
    q0 = jnp.where(first, q2, zero)
    q1 = jnp.where(first, zero, q2)
    a0 = (_nt_dot(q0, k2) * d0_ref[...]).astype(BF16)
    a1 = (_nt_dot(q1, k2) * d1_ref[...]).astype(BF16)
    state = r_ref[...]
    cross = _dot(q2, state.astype(BF16)) * xi_ref[...]
    o = jnp.where(first, _dot(a0, v2), _dot(a1, v2)) + cross

    kz = (k2.astype(F32) * z_ref[...]).T.astype(BF16)
    upd = _dot(kz, v2)
    rr = lax.broadcasted_iota(jnp.int32, (LANES, LANES), 0) < HEAD_DIM
    cc = lax.broadcasted_iota(jnp.int32, (LANES, LANES), 1) < HEAD_DIM
    r_ref[...] = jnp.where(rr == cc, state * dc_ref[...] + upd, 0.0)

    inv_n = 1.0 / HEAD_DIM
    s_first = jnp.sum(jnp.where(first, o, 0.0), axis=-1, keepdims=True)
    s_second = jnp.sum(jnp.where(first, 0.0, o), axis=-1, keepdims=True)
    d = o - jnp.where(first, s_first, s_second) * inv_n
    dd = d * d
    v_first = jnp.sum(jnp.where(first, dd, 0.0), axis=-1, keepdims=True)
    v_second = jnp.sum(jnp.where(first, 0.0, dd), axis=-1, keepdims=True)
    var = jnp.where(first, v_first, v_second) * inv_n
    gate = g_ref[...].astype(F32)
    y = gate * jax.nn.sigmoid(gate) * (d * lax.rsqrt(var + GN_EPS))
    if prev_ref is not None:
        y = y + prev_ref[...].astype(F32)
    o_ref[...] = y.astype(o_ref.dtype)


def _retention_pass(h, log_gamma, prev, rev):
    t = h.shape[0]
    C = RET_CHUNK
    nc = t // C
    groups = RET_W // LANES

    def row(c):
        return (nc - 1 - c) if rev else c

    def col_spec(base):
        return pl.BlockSpec((C, LANES), lambda p, c, base=base: (row(c), base + p))

    gate_base = (4 if rev else 3) * groups
    in_specs = [
        pl.BlockSpec(memory_space=pltpu.SMEM),
        col_spec(0), col_spec(groups), col_spec(2 * groups), col_spec(gate_base),
    ]
    args = [log_gamma, h, h, h, h]
    if not rev:
        in_specs.append(col_spec(0))
        args.append(prev)
    return pl.pallas_call(
        functools.partial(_ret_kernel, rev=rev),
        out_shape=jax.ShapeDtypeStruct((t, RET_W), BF16),
        grid=(groups, nc),
        in_specs=in_specs,
        out_specs=col_spec(0),
        scratch_shapes=[
            pltpu.VMEM((LANES, LANES), F32),
            pltpu.VMEM((C, C), F32),
            pltpu.VMEM((C, C), F32),
            pltpu.VMEM((C, LANES), F32),
            pltpu.VMEM((C, LANES), F32),
            pltpu.VMEM((1, LANES), F32),
        ],
        compiler_params=_params(("parallel", "arbitrary")),
        name="retention_bwd" if rev else "retention_fwd",
    )(*args)


def _retention(h, decay_logits):
    log_gamma = jax.nn.log_sigmoid(decay_logits.astype(F32))
    y_bwd = _retention_pass(h, log_gamma, None, rev=True)
    return _retention_pass(h, log_gamma, y_bwd, rev=False)


def _dil_kernel(q_ref, k_ref, v_ref, o_ref, lse_ref, *, length):
    i = pl.program_id(2)
    start = jnp.clip(i * DIL_TQ - DIL_HALF, 0, length - DIL_TK)
    start = pl.multiple_of(start, DIL_HALF)
    kw = k_ref[pl.ds(start, DIL_TK), :]
    vw = v_ref[pl.ds(start, DIL_TK), :]
    qpos = i * DIL_TQ + lax.broadcasted_iota(jnp.int32, (DIL_TQ, DIL_TK), 0)
    kpos = start + lax.broadcasted_iota(jnp.int32, (DIL_TQ, DIL_TK), 1)
    valid = jnp.abs(kpos - qpos) <= DIL_HALF
    first = _first_head_lanes((1, LANES))
    q2 = q_ref[...]
    zero = jnp.zeros_like(q2)

    def one_head(qh):
        s = jnp.where(valid, _nt_dot(qh, kw), NEG_INF)
        m = jnp.max(s, axis=-1, keepdims=True)
        e = jnp.exp(s - m)
        den = jnp.sum(e, axis=-1, keepdims=True)
        return _dot(e.astype(BF16), vw) / den, m + jnp.log(den)

    o0, l0 = one_head(jnp.where(first, q2, zero))
    o1, l1 = one_head(jnp.where(first, zero, q2))
    o_ref[...] = jnp.where(first, o0, o1).astype(o_ref.dtype)
    lse_ref[...] = jnp.where(first, l0, l1)


def _dilated_branch(h, dil):
    t = h.shape[0]
    length = t // dil
    groups = DIL_W // LANES
    cols = EVEN_IN // LANES
    q_base = 5 * RET_W // LANES
    hv = h.reshape(length, dil * EVEN_IN)
    o, lse = pl.pallas_call(
        functools.partial(_dil_kernel, length=length),
        out_shape=(jax.ShapeDtypeStruct((length, dil * DIL_W), BF16),
                   jax.ShapeDtypeStruct((length, dil * DIL_W), F32)),
        grid=(dil, groups, length // DIL_TQ),
        in_specs=[
            pl.BlockSpec((DIL_TQ, LANES), lambda r, p, i: (i, r * cols + q_base + p)),
            pl.BlockSpec((length, LANES), lambda r, p, i: (0, r * cols + q_base + groups + p)),
            pl.BlockSpec((length, LANES), lambda r, p, i: (0, r * cols + q_base + 2 * groups + p)),
        ],
        out_specs=(
            pl.BlockSpec((DIL_TQ, LANES), lambda r, p, i: (i, r * groups + p)),
            pl.BlockSpec((DIL_TQ, LANES), lambda r, p, i: (i, r * groups + p)),
        ),
        compiler_params=_params(("parallel", "parallel", "parallel")),
        name=f"dilated_attn_d{dil}",
    )(hv, hv, hv)
    return o.reshape(t, DIL_W), lse.reshape(t, DIL_W)


def _out_even_kernel(x_ref, yr_ref, o1_ref, o2_ref, o3_ref, l1_ref, l2_ref, l3_ref,
                     w_ref, g_ref, b_ref, out_ref):
    l1, l2, l3 = l1_ref[...], l2_ref[...], l3_ref[...]
    m = jnp.maximum(jnp.maximum(l1, l2), l3)
    e1, e2, e3 = jnp.exp(l1 - m), jnp.exp(l2 - m), jnp.exp(l3 - m)
    den = e1 + e2 + e3
    y_dil = (e1 / den) * o1_ref[...].astype(F32) + (e2 / den) * o2_ref[...].astype(F32) \
        + (e3 / den) * o3_ref[...].astype(F32)
    mix = _dot(yr_ref[...], w_ref[:RET_W, :]) + _dot(y_dil.astype(BF16), w_ref[RET_W:, :])
    z = DEEPNORM_ALPHA * x_ref[...] + mix
    out_ref[...] = _layer_norm(z, g_ref[...], b_ref[...])


def _out_even(x, y_ret, outs, lses, w_bf, g, b):
    t = x.shape[0]
    tm = PROJ_TM
    half = pl.BlockSpec((tm, RET_W), lambda i: (i, 0))
    vec = pl.BlockSpec((1, D_MODEL), lambda i: (0, 0))
    return pl.pallas_call(
        _out_even_kernel,
        out_shape=jax.ShapeDtypeStruct((t, D_MODEL), F32),
        grid=(t // tm,),
        in_specs=[pl.BlockSpec((tm, D_MODEL), lambda i: (i, 0))] + [half] * 7
        + [pl.BlockSpec((D_MODEL, D_MODEL), lambda i: (0, 0)), vec, vec],
        out_specs=pl.BlockSpec((tm, D_MODEL), lambda i: (i, 0)),
        compiler_params=_params(("parallel",)),
        name="out_proj_even",
    )(x, y_ret, *outs, *lses, w_bf, g.reshape(1, -1), b.reshape(1, -1))


def _out_odd_kernel(x_ref, y_ref, w_ref, g_ref, b_ref, out_ref):
    z = DEEPNORM_ALPHA * x_ref[...] + _dot(y_ref[...], w_ref[...])
    out_ref[...] = _layer_norm(z, g_ref[...], b_ref[...])


def _out_odd(x, y, w_bf, g, b):
    t = x.shape[0]
    tm = PROJ_TM
    full = pl.BlockSpec((tm, D_MODEL), lambda i: (i, 0))
    vec = pl.BlockSpec((1, D_MODEL), lambda i: (0, 0))
    return pl.pallas_call(
        _out_odd_kernel,
        out_shape=jax.ShapeDtypeStruct((t, D_MODEL), F32),
        grid=(t // tm,),
        in_specs=[full, full, pl.BlockSpec((D_MODEL, D_MODEL), lambda i: (0, 0)), vec, vec],
        out_specs=full,
        compiler_params=_params(("parallel",)),
        name="out_proj_odd",
    )(x, y, w_bf, g.reshape(1, -1), b.reshape(1, -1))


def _even_mixer(x, w_in, w_out, decay_logits, g, b, cos_t, sin_t):
    h = _proj_even(x, w_in.astype(BF16), cos_t, sin_t)
    y_ret = _retention(h, decay_logits)
    outs, lses = [], []
    for _, dil in DIL_CONFIGS:
        o, lse = _dilated_branch(h, dil)
        outs.append(o)
        lses.append(lse)
    return _out_even(x, y_ret, outs, lses, w_out.astype(BF16), g, b)


def _na_bias_table(rpb, rows):
    n_blk = rows // NA_ROWS
    blocks = np.array([0, 1, 2, n_blk - 2, n_blk - 1])
    kstart = np.clip(NA_ROWS * blocks - NA_KH // 2, 0, rows - NA_KROWS)
    r = NA_ROWS * blocks[:, None] + np.arange(NA_ROWS)[None, :]
    r0 = np.clip(r - NA_KH // 2, 0, rows - NA_KH)
    kr = kstart[:, None] + np.arange(NA_KROWS)[None, :]
    row_ok = (kr[:, None, :] >= r0[:, :, None]) & (kr[:, None, :] < r0[:, :, None] + NA_KH)
    dr = kr[:, None, :] - r[:, :, None] + (NA_KH - 1)
    c = np.arange(GRID_W)
    c0 = np.clip(c - NA_KW // 2, 0, GRID_W - NA_KW)
    kc = np.arange(GRID_W)
    col_ok = (kc[None, :] >= c0[:, None]) & (kc[None, :] < c0[:, None] + NA_KW)
    pad = GRID_W - NA_KW
    rp = jnp.pad(rpb.astype(F32), ((0, 0), (0, 0), (pad, pad)))
    by_col = jnp.stack([rp[:, :, GRID_W - 1 - ci:2 * GRID_W - 1 - ci] for ci in range(GRID_W)], axis=2)
    by_col = jnp.where(col_ok[None, None], by_col, NEG_INF)
    outside = jnp.full((N_NA_HEADS, GRID_W, GRID_W), NEG_INF, F32)
    patterns = []
    for pi in range(5):
        q_rows = []
        for qi in range(NA_ROWS):
            blocks_ = [by_col[:, int(dr[pi, qi, ki])] if row_ok[pi, qi, ki] else outside
                       for ki in range(NA_KROWS)]
            q_rows.append(jnp.concatenate(blocks_, axis=-1))
        patterns.append(jnp.concatenate(q_rows, axis=-2))
    return jnp.stack(patterns, axis=0)


def _na_kernel(q_ref, k_ref, v_ref, bias_ref, o_ref, *, rows):
    rb = pl.program_id(1)
    kstart = jnp.clip(NA_ROWS * rb - NA_KH // 2, 0, rows - NA_KROWS) * GRID_W
    kstart = pl.multiple_of(kstart, GRID_W)
    nk = NA_KROWS * GRID_W
    kw = k_ref[pl.ds(kstart, nk), :]
    vw = v_ref[pl.ds(kstart, nk), :]
    first = _first_head_lanes((1, LANES))
    q2 = q_ref[...]
    zero = jnp.zeros_like(q2)

    def one_head(qh, bias):
        s = _nt_dot(qh, kw) + bias
        m = jnp.max(s, axis=-1, keepdims=True)
        e = jnp.exp(s - m)
        den = jnp.sum(e, axis=-1, keepdims=True)
        return _dot(e.astype(BF16), vw) / den

    o0 = one_head(jnp.where(first, q2, zero), bias_ref[0, 0])
    o1 = one_head(jnp.where(first, zero, q2), bias_ref[0, 1])
    o_ref[...] = jnp.where(first, o0, o1).astype(o_ref.dtype)


def _neighbourhood_attention(h, rpb):
    t = h.shape[0]
    rows = t // GRID_W
    n_blk = rows // NA_ROWS
    groups = D_MODEL // LANES
    tq = NA_ROWS * GRID_W
    nk = NA_KROWS * GRID_W
    bias = _na_bias_table(rpb, rows)

    def pattern(rb):
        return jnp.clip(rb, 0, 2) + jnp.clip(rb - (n_blk - 3), 0, 2)

    return pl.pallas_call(
        functools.partial(_na_kernel, rows=rows),
        out_shape=jax.ShapeDtypeStruct((t, D_MODEL), BF16),
        grid=(groups, n_blk),
        in_specs=[
            pl.BlockSpec((tq, LANES), lambda p, rb: (rb, p)),
            pl.BlockSpec((t, LANES), lambda p, rb: (0, groups + p)),
            pl.BlockSpec((t, LANES), lambda p, rb: (0, 2 * groups + p)),
            pl.BlockSpec((1, HEADS_PER_GROUP, tq, nk), lambda p, rb: (pattern(rb), p, 0, 0)),
        ],
        out_specs=pl.BlockSpec((tq, LANES), lambda p, rb: (rb, p)),
        compiler_params=_params(("parallel", "parallel")),
        name="neighbourhood_attn",
    )(h, h, h, bias)


def _odd_mixer(x, w_in, w_out, rpb, g, b):
    h = _proj_odd(x, w_in.astype(BF16))
    y = _neighbourhood_attention(h, rpb)
    return _out_odd(x, y, w_out.astype(BF16), g, b)


def _router_kernel(x_ref, wt_ref, b_ref, idx_ref, gate_ref, rank_ref, count_ref):
    tm = x_ref.shape[0]

    @pl.when(pl.program_id(0) == 0)
    def _init():
        count_ref[...] = jnp.zeros_like(count_ref)

    logits = lax.dot_general(wt_ref[...], x_ref[...], (((1,), (1,)), ((), ())),
                             precision=lax.Precision.HIGHEST, preferred_element_type=F32)
    logits = logits + b_ref[...]
    expert = lax.broadcasted_iota(jnp.int32, logits.shape, 0)
    vals, idxs = [], []
    for _ in range(TOP_K):
        m = jnp.max(logits, axis=0, keepdims=True)
        sel = jnp.min(jnp.where(logits == m, expert, N_EXPERTS), axis=0, keepdims=True)
        vals.append(m)
        idxs.append(sel)
        logits = jnp.where(expert == sel, -jnp.inf, logits)
    exps = [jnp.exp(v - vals[0]) for v in vals]
    den = exps[0] + exps[1] + exps[2] + exps[3]
    for k in range(TOP_K):
        idx_ref[k:k + 1, :] = idxs[k]
        gate_ref[k:k + 1, :] = exps[k] / den

    onehots = [(expert == idxs[k]).astype(F32) for k in range(TOP_K)]
    per_tok = onehots[0] + onehots[1] + onehots[2] + onehots[3]
    earlier = lax.broadcasted_iota(jnp.int32, (tm, tm), 0) < lax.broadcasted_iota(jnp.int32, (tm, tm), 1)
    before = count_ref[...] + _dot(per_tok.astype(BF16), earlier.astype(BF16))
    for k in range(TOP_K):
        rank_ref[k:k + 1, :] = jnp.sum(onehots[k] * before, axis=0, keepdims=True).astype(jnp.int32)
    count_ref[...] += jnp.sum(per_tok, axis=1, keepdims=True)


def _router(x, router_w, router_b):
    t = x.shape[0]
    tm = ROUTER_TM
    row = pl.BlockSpec((TOP_K, tm), lambda i: (0, i))
    return pl.pallas_call(
        _router_kernel,
        out_shape=(jax.ShapeDtypeStruct((TOP_K, t), jnp.int32),
                   jax.ShapeDtypeStruct((TOP_K, t), F32),
                   jax.ShapeDtypeStruct((TOP_K, t), jnp.int32),
                   jax.ShapeDtypeStruct((N_EXPERTS, 1), F32)),
        grid=(t // tm,),
        in_specs=[
            pl.BlockSpec((tm, D_MODEL), lambda i: (i, 0)),
            pl.BlockSpec((N_EXPERTS, D_MODEL), lambda i: (0, 0)),
            pl.BlockSpec((N_EXPERTS, 1), lambda i: (0, 0)),
        ],
        out_specs=(row, row, row, pl.BlockSpec((N_EXPERTS, 1), lambda i: (0, 0))),
        compiler_params=_params(("arbitrary",)),
        name="moe_router",
    )(x, router_w.T, router_b.reshape(-1, 1))


def _slot_layout(top_i, rank, counts):
    t = top_i.shape[1]
    bm = MOE_BM
    experts = jnp.arange(N_EXPERTS, dtype=jnp.int32)
    counts = counts.reshape(-1).astype(jnp.int32)
    pcounts = (counts + bm - 1) // bm * bm
    pend = jnp.cumsum(pcounts)
    pstart = pend - pcounts
    pos = rank + jnp.sum(jnp.where(top_i[:, :, None] == experts, pstart, 0), axis=-1)
    n_blk = (TOP_K * t) // bm + N_EXPERTS
    blk_start = jnp.arange(n_blk, dtype=jnp.int32) * bm
    blk_e = jnp.minimum(jnp.sum((pend[None, :] <= blk_start[:, None]).astype(jnp.int32), axis=1),
                        N_EXPERTS - 1)
    n_used = (pend[-1] // bm).astype(jnp.int32).reshape(1)
    return pos.astype(jnp.int32), blk_e, n_used, n_blk


def _row_copy(src, src_row, dst, dst_row, sem):
    return pltpu.make_async_copy(src.at[pl.ds(src_row, 1)], dst.at[pl.ds(dst_row, 1)], sem)


def _dispatch_kernel(pos_ref, x_ref, slots_in_hbm, slots_hbm, sem, *, n_tok, tm):
    del slots_in_hbm
    base = pl.program_id(0) * tm

    def issue(j, carry):
        for k in range(TOP_K):
            _row_copy(x_ref, j, slots_hbm, pos_ref[k * n_tok + base + j], sem).start()
        return carry

    lax.fori_loop(0, tm, issue, 0)
    for k in range(TOP_K):
        pltpu.make_async_copy(x_ref, slots_hbm.at[pl.ds(0, tm)], sem).wait()


def _dispatch(x, pos_flat, n_slots):
    t = x.shape[0]
    tm = DISPATCH_TM
    return pl.pallas_call(
        functools.partial(_dispatch_kernel, n_tok=t, tm=tm),
        out_shape=jax.ShapeDtypeStruct((n_slots, D_MODEL), F32),
        grid_spec=pltpu.PrefetchScalarGridSpec(
            num_scalar_prefetch=1,
            grid=(t // tm,),
            in_specs=[pl.BlockSpec((tm, D_MODEL), lambda i, p: (i, 0)),
                      pl.BlockSpec(memory_space=pl.ANY)],
            out_specs=pl.BlockSpec(memory_space=pl.ANY),
            scratch_shapes=[pltpu.SemaphoreType.DMA],
        ),
        input_output_aliases={2: 0},
        compiler_params=_params(("arbitrary",), has_side_effects=True),
        name="moe_dispatch",
    )(pos_flat, x, jnp.zeros((n_slots, D_MODEL), F32))


def _expert_kernel(blk_e_ref, n_used_ref, x_ref, wu_ref, bu_ref, wd_ref, bd_ref, o_ref):
    del blk_e_ref
    b = pl.program_id(0)

    @pl.when(b < n_used_ref[0])
    def _compute():
        xb = x_ref[...].astype(BF16)
        h = _dot(xb, wu_ref[...]) + bu_ref[...]
        nxt = pltpu.roll(h, 2 * D_FF - 1, 1)
        x_glu = jnp.minimum(h, SWIGLU_LIMIT)
        x_lin = jnp.clip(nxt, -SWIGLU_LIMIT, SWIGLU_LIMIT)
        act = x_glu * jax.nn.sigmoid(SWIGLU_ALPHA * x_glu) * (x_lin + 1.0)
        even = lax.broadcasted_iota(jnp.int32, (1, D_FF), 1) % 2 == 0
        packed = jnp.where(even, act[:, :D_FF], pltpu.roll(act[:, D_FF:], 1, 1))
        o_ref[...] = _dot(packed.astype(BF16), wd_ref[...]) + bd_ref[...]

    @pl.when(b >= n_used_ref[0])
    def _unused():
        o_ref[...] = jnp.zeros_like(o_ref)


def _experts(slots, blk_e, n_used, n_blk, w_up, b_up, w_down, b_down):
    bm = MOE_BM

    def wspec(shape):
        return pl.BlockSpec((None,) + shape, lambda b, e, n: (e[b],) + (0,) * len(shape))

    return pl.pallas_call(
        _expert_kernel,
        out_shape=jax.ShapeDtypeStruct(slots.shape, F32),
        grid_spec=pltpu.PrefetchScalarGridSpec(
            num_scalar_prefetch=2,
            grid=(n_blk,),
            in_specs=[
                pl.BlockSpec((bm, D_MODEL), lambda b, e, n: (jnp.minimum(b, n[0] - 1), 0)),
                wspec((D_MODEL, 2 * D_FF)), wspec((1, 2 * D_FF)),
                wspec((D_FF, D_MODEL)), wspec((1, D_MODEL)),
            ],
            out_specs=pl.BlockSpec((bm, D_MODEL), lambda b, e, n: (b, 0)),
        ),
        compiler_params=_params(("arbitrary",)),
        name="moe_experts",
    )(blk_e, n_used, slots, w_up, b_up, w_down, b_down)


def _combine_kernel(pos_ref, y_hbm, x_ref, gate_ref, g_ref, b_ref, o_ref, buf_ref, sem, *, n_tok, tm):
    base = pl.program_id(0) * tm

    def issue(j, carry):
        tok = base + j
        for k in range(TOP_K):
            _row_copy(y_hbm, pos_ref[k * n_tok + tok], buf_ref.at[k], j, sem).start()
        return carry

    lax.fori_loop(0, tm, issue, 0)
    for k in range(TOP_K):
        pltpu.make_async_copy(y_hbm.at[pl.ds(0, tm)], buf_ref.at[k], sem).wait()
    gates = gate_ref[...]
    ffn = gates[:, 0:1] * buf_ref[0]
    for k in range(1, TOP_K):
        ffn = ffn + gates[:, k:k + 1] * buf_ref[k]
    z = DEEPNORM_ALPHA * x_ref[...] + ffn
    o_ref[...] = _layer_norm(z, g_ref[...], b_ref[...])


def _combine(x, y_slots, pos_flat, gates_tk, g, b):
    t = x.shape[0]
    tm = COMBINE_TM
    full = pl.BlockSpec((tm, D_MODEL), lambda i, p: (i, 0))
    vec = pl.BlockSpec((1, D_MODEL), lambda i, p: (0, 0))
    return pl.pallas_call(
        functools.partial(_combine_kernel, n_tok=t, tm=tm),
        out_shape=jax.ShapeDtypeStruct((t, D_MODEL), F32),
        grid_spec=pltpu.PrefetchScalarGridSpec(
            num_scalar_prefetch=1,
            grid=(t // tm,),
            in_specs=[pl.BlockSpec(memory_space=pl.ANY), full,
                      pl.BlockSpec((tm, TOP_K), lambda i, p: (i, 0)), vec, vec],
            out_specs=full,
            scratch_shapes=[pltpu.VMEM((TOP_K, tm, D_MODEL), F32), pltpu.SemaphoreType.DMA],
        ),
        compiler_params=_params(("arbitrary",)),
        name="moe_combine",
    )(pos_flat, y_slots, x, gates_tk, g.reshape(1, -1), b.reshape(1, -1))


def _moe_layer(x, router_w, router_b, w_up, b_up, w_down, b_down, g, b):
    top_i, gates, rank, counts = _router(x, router_w, router_b)
    pos, blk_e, n_used, n_blk = _slot_layout(top_i, rank, counts)
    pos_flat = pos.reshape(-1)
    slots = _dispatch(x, pos_flat, n_blk * MOE_BM)
    w_down_perm = w_down.reshape(N_EXPERTS, 2, D_FF // 2, D_MODEL).transpose(0, 2, 1, 3)
    w_down_perm = w_down_perm.reshape(N_EXPERTS, D_FF, D_MODEL).astype(BF16)
    y_slots = _experts(slots, blk_e, n_used, n_blk, w_up.astype(BF16), b_up[:, None, :],
                       w_down_perm, b_down[:, None, :])
    return _combine(x, y_slots, pos_flat, gates.T, g, b)


def kernel(x, ab_w_in, ab_w_out, ret_decay, c_w_in, c_w_out, c_rpb, ln_g, ln_b,
           router_w, router_b, exp_w_up, exp_b_up, exp_w_down, exp_b_down):
    b_, t_, d_ = x.shape
    assert b_ == 1 and d_ == D_MODEL
    xs = x.reshape(t_, d_)
    cos_t, sin_t = _rope_tables(t_)
    for layer in range(DEPTH):
        j = layer // 2
        if layer % 2 == 0:
            xs = _even_mixer(xs, ab_w_in[j], ab_w_out[j], ret_decay[j], ln_g[layer, 0], ln_b[layer, 0],
                             cos_t, sin_t)
        else:
            xs = _odd_mixer(xs, c_w_in[j], c_w_out[j], c_rpb[j], ln_g[layer, 0], ln_b[layer, 0])
        xs = _moe_layer(xs, router_w[layer], router_b[layer], exp_w_up[layer], exp_b_up[layer],
                        exp_w_down[layer], exp_b_down[layer], ln_g[layer, 1], ln_b[layer, 1])
    return xs.reshape(b_, t_, d_)
```

```python
import functools

import jax
import jax.numpy as jnp
import numpy as np
from jax import lax
from jax.experimental import pallas as pl
from jax.experimental.pallas import tpu as pltpu

F32 = jnp.float32
BF16 = jnp.bfloat16

D_MODEL = 1024
HEAD_DIM = 64
LANES = 128
HEADS_PER_GROUP = LANES // HEAD_DIM
N_RET_HEADS = 8
N_DIL_HEADS = 8
N_NA_HEADS = 16
RET_W = N_RET_HEADS * HEAD_DIM
DIL_W = N_DIL_HEADS * HEAD_DIM
EVEN_IN = 5 * RET_W + 3 * DIL_W
ODD_IN = 3 * D_MODEL
ROPE_THETA = 10000.0
DIL_CONFIGS = ((128, 1), (512, 4), (2048, 16))
DIL_HALF = 64
GRID_W = 64
NA_KH = 8
NA_KW = 16
N_EXPERTS = 32
TOP_K = 4
D_FF = D_MODEL
SWIGLU_LIMIT = 7.0
SWIGLU_ALPHA = 1.702
LN_EPS = 1e-5
GN_EPS = 1e-6
NEG_INF = -1e30
DEPTH = 4
DEEPNORM_ALPHA = (2.0 * DEPTH) ** 0.25
QK_SCALE = HEAD_DIM ** -0.5

VMEM_LIMIT_BYTES = 56 * 1024 * 1024

PROJ_TM = 512
RET_CHUNK = 256
DIL_TQ = 128
DIL_TK = DIL_TQ + 2 * DIL_HALF
DIL_SUBTILES = 4
NA_ROWS = 2
NA_KROWS = NA_ROWS + NA_KH - 1
NA_SUBBLOCKS = 4
ROUTER_TM = 512
MOE_BM = 256
DISPATCH_TM = 512
COMBINE_TM = 256


def _params(sem, **kw):
    return pltpu.CompilerParams(dimension_semantics=sem, vmem_limit_bytes=VMEM_LIMIT_BYTES, **kw)


def _nt_dot(a, b):
    return lax.dot_general(a, b, (((1,), (1,)), ((), ())), preferred_element_type=F32)


def _dot(a, b):
    return jnp.dot(a, b, preferred_element_type=F32)


def _layer_norm(z, g, b):
    mu = jnp.mean(z, axis=-1, keepdims=True)
    d = z - mu
    var = jnp.mean(d * d, axis=-1, keepdims=True)
    return d * lax.rsqrt(var + LN_EPS) * g + b


def _first_head_lanes(shape):
    return lax.broadcasted_iota(jnp.int32, shape, len(shape) - 1) < HEAD_DIM


def _rope_tables(t):
    pos = jnp.arange(t, dtype=F32)
    inv = ROPE_THETA ** (-jnp.arange(0, HEAD_DIM, 2, dtype=F32) / HEAD_DIM)
    ang = pos[:, None] * inv[None, :]
    cos, sin = jnp.cos(ang), jnp.sin(ang)
    return (jnp.concatenate([cos, cos, cos, cos], axis=1),
            jnp.concatenate([-sin, sin, -sin, sin], axis=1))


def _rope(h, c, s):
    half = HEAD_DIM // 2
    lane = lax.broadcasted_iota(jnp.int32, h.shape, 1)
    first_half = (lane // half) % 2 == 0
    partner = jnp.where(first_half, pltpu.roll(h, LANES - half, 1), pltpu.roll(h, half, 1))
    return h * c + partner * s


def _proj_even_kernel(x_ref, w_ref, c_ref, s_ref, o_ref, *class_refs):
    tm = x_ref.shape[0]
    xb = x_ref[...].astype(BF16)
    c = c_ref[...]
    s = s_ref[...]
    rope_groups = {0: 1.0, 1: QK_SCALE, 5: QK_SCALE, 6: 1.0}
    for g in range(EVEN_IN // RET_W):
        h = _dot(xb, w_ref[:, g * RET_W:(g + 1) * RET_W])
        for j in range(RET_W // LANES):
            hj = h[:, j * LANES:(j + 1) * LANES]
            if g in rope_groups:
                hj = _rope(hj, c, s) * rope_groups[g]
            col = g * RET_W + j * LANES
            o_ref[:, col:col + LANES] = hj.astype(BF16)
    h_dil = o_ref[:, 5 * RET_W:]
    row = lax.broadcasted_iota(jnp.int32, (tm, tm), 0)
    col = lax.broadcasted_iota(jnp.int32, (tm, tm), 1)
    for cls_ref, (_, dil) in zip(class_refs, DIL_CONFIGS[1:]):
        per_class = tm // dil
        select = (col == dil * (row % per_class) + row // per_class).astype(BF16)
        regrouped = _dot(select, h_dil).astype(BF16)
        for r in range(dil):
            cls_ref[r] = regrouped[r * per_class:(r + 1) * per_class]


def _proj_even(x, w_bf, cos_t, sin_t):
    t = x.shape[0]
    tm = PROJ_TM
    dil_w = 3 * DIL_W
    class_shapes = [jax.ShapeDtypeStruct((dil, t // dil, dil_w), BF16) for _, dil in DIL_CONFIGS[1:]]
    class_specs = [pl.BlockSpec((dil, tm // dil, dil_w), lambda i: (0, i, 0)) for _, dil in DIL_CONFIGS[1:]]
    return pl.pallas_call(
        _proj_even_kernel,
        out_shape=[jax.ShapeDtypeStruct((t, EVEN_IN), BF16)] + class_shapes,
        grid=(t // tm,),
        in_specs=[
            pl.BlockSpec((tm, D_MODEL), lambda i: (i, 0)),
            pl.BlockSpec((D_MODEL, EVEN_IN), lambda i: (0, 0)),
            pl.BlockSpec((tm, LANES), lambda i: (i, 0)),
            pl.BlockSpec((tm, LANES), lambda i: (i, 0)),
        ],
        out_specs=[pl.BlockSpec((tm, EVEN_IN), lambda i: (i, 0))] + class_specs,
        compiler_params=_params(("parallel",)),
        name="proj_even",
    )(x, w_bf, cos_t, sin_t)


def _proj_odd_kernel(x_ref, w_ref, o_ref):
    xb = x_ref[...].astype(BF16)
    for g in range(3):
        h = _dot(xb, w_ref[:, g * D_MODEL:(g + 1) * D_MODEL])
        if g == 0:
            h = h * QK_SCALE
        o_ref[:, g * D_MODEL:(g + 1) * D_MODEL] = h.astype(BF16)


def _proj_odd(x, w_bf):
    t = x.shape[0]
    tm = PROJ_TM
    return pl.pallas_call(
        _proj_odd_kernel,
        out_shape=jax.ShapeDtypeStruct((t, ODD_IN), BF16),
        grid=(t // tm,),
        in_specs=[
            pl.BlockSpec((tm, D_MODEL), lambda i: (i, 0)),
            pl.BlockSpec((D_MODEL, ODD_IN), lambda i: (0, 0)),
        ],
        out_specs=pl.BlockSpec((tm, ODD_IN), lambda i: (i, 0)),
        compiler_params=_params(("parallel",)),
        name="proj_odd",
    )(x, w_bf)


def _ret_kernel(lg_ref, q_ref, k_ref, v_ref, g_ref, *rest, rev):
    if rev:
        o_ref, r_ref, d0_ref, d1_ref, z_ref, xi_ref, dc_ref = rest
        prev_ref = None
    else:
        prev_ref, o_ref, r_ref, d0_ref, d1_ref, z_ref, xi_ref, dc_ref = rest
    C = RET_CHUNK
    groups = RET_W // LANES
    c = pl.program_id(0)
    first = _first_head_lanes((1, LANES))
    direction = 1 if rev else 0

    @pl.when(c == 0)
    def _init():
        row = lax.broadcasted_iota(jnp.int32, (C, C), 0)
        col = lax.broadcasted_iota(jnp.int32, (C, C), 1)
        dist = (col - row) if rev else (row - col)
        keep = (dist > 0) if rev else (dist >= 0)
        distf = jnp.maximum(dist, 0).astype(F32)
        idx = lax.broadcasted_iota(jnp.int32, (C, LANES), 0).astype(F32)
        for p in range(groups):
            lg0 = lg_ref[direction, 2 * p]
            lg1 = lg_ref[direction, 2 * p + 1]
            d0_ref[p] = jnp.where(keep, jnp.exp(lg0 * distf), 0.0)
            d1_ref[p] = jnp.where(keep, jnp.exp(lg1 * distf), 0.0)
            lgl = jnp.where(first, lg0, lg1)
            z_ref[p] = jnp.exp(lgl * (idx if rev else (C - 1.0 - idx)))
            xi_ref[p] = jnp.exp(lgl * ((C - idx) if rev else (idx + 1.0)))
            dc_ref[p] = jnp.exp(lgl * float(C))
        r_ref[...] = jnp.zeros_like(r_ref)

    rr = lax.broadcasted_iota(jnp.int32, (LANES, LANES), 0) < HEAD_DIM
    cc = lax.broadcasted_iota(jnp.int32, (LANES, LANES), 1) < HEAD_DIM
    inv_n = 1.0 / HEAD_DIM
    for p in range(groups):
        lanes = slice(p * LANES, (p + 1) * LANES)
        q2 = q_ref[:, lanes]
        k2 = k_ref[:, lanes]
        v2 = v_ref[:, lanes]
        zero = jnp.zeros_like(q2)
        q0 = jnp.where(first, q2, zero)
        q1 = jnp.where(first, zero, q2)
        a0 = (_nt_dot(q0, k2) * d0_ref[p]).astype(BF16)
        a1 = (_nt_dot(q1, k2) * d1_ref[p]).astype(BF16)
        state = r_ref[p]
        cross = _dot(q2, state.astype(BF16)) * xi_ref[p]
        o = jnp.where(first, _dot(a0, v2), _dot(a1, v2)) + cross

        kz = (k2.astype(F32) * z_ref[p]).T.astype(BF16)
        r_ref[p] = jnp.where(rr == cc, state * dc_ref[p] + _dot(kz, v2), 0.0)

        s_first = jnp.sum(jnp.where(first, o, 0.0), axis=-1, keepdims=True)
        s_second = jnp.sum(jnp.where(first, 0.0, o), axis=-1, keepdims=True)
        d = o - jnp.where(first, s_first, s_second) * inv_n
        dd = d * d
        v_first = jnp.sum(jnp.where(first, dd, 0.0), axis=-1, keepdims=True)
        v_second = jnp.sum(jnp.where(first, 0.0, dd), axis=-1, keepdims=True)
        var = jnp.where(first, v_first, v_second) * inv_n
        gate = g_ref[:, lanes].astype(F32)
        y = gate * jax.nn.sigmoid(gate) * (d * lax.rsqrt(var + GN_EPS))
        if prev_ref is not None:
            y = y + prev_ref[:, lanes].astype(F32)
        o_ref[:, lanes] = y.astype(o_ref.dtype)


def _retention_pass(h, log_gamma, prev, rev):
    t = h.shape[0]
    C = RET_CHUNK
    nc = t // C
    groups = RET_W // LANES

    def col_spec(group):
        return pl.BlockSpec((C, RET_W), lambda c, group=group: ((nc - 1 - c) if rev else c, group))

    in_specs = [
        pl.BlockSpec(memory_space=pltpu.SMEM),
        col_spec(0), col_spec(1), col_spec(2), col_spec(4 if rev else 3),
    ]
    args = [log_gamma, h, h, h, h]
    if not rev:
        in_specs.append(col_spec(0))
        args.append(prev)
    return pl.pallas_call(
        functools.partial(_ret_kernel, rev=rev),
        out_shape=jax.ShapeDtypeStruct((t, RET_W), BF16),
        grid=(nc,),
        in_specs=in_specs,
        out_specs=col_spec(0),
        scratch_shapes=[
            pltpu.VMEM((groups, LANES, LANES), F32),
            pltpu.VMEM((groups, C, C), F32),
            pltpu.VMEM((groups, C, C), F32),
            pltpu.VMEM((groups, C, LANES), F32),
            pltpu.VMEM((groups, C, LANES), F32),
            pltpu.VMEM((groups, 1, LANES), F32),
        ],
        compiler_params=_params(("arbitrary",)),
        name="retention_bwd" if rev else "retention_fwd",
    )(*args)


def _retention(h, decay_logits):
    log_gamma = jax.nn.log_sigmoid(decay_logits.astype(F32))
    y_bwd = _retention_pass(h, log_gamma, None, rev=True)
    return _retention_pass(h, log_gamma, y_bwd, rev=False)


def _dil_kernel(q_ref, k_ref, v_ref, o_ref, lse_ref, *, length):
    first = _first_head_lanes((1, LANES))
    row = lax.broadcasted_iota(jnp.int32, (DIL_TQ, DIL_TK), 0)
    col = lax.broadcasted_iota(jnp.int32, (DIL_TQ, DIL_TK), 1)
    for sub in range(DIL_SUBTILES):
        i = pl.program_id(2) * DIL_SUBTILES + sub
        rows = slice(sub * DIL_TQ, (sub + 1) * DIL_TQ)
        start = jnp.clip(i * DIL_TQ - DIL_HALF, 0, length - DIL_TK)
        start = pl.multiple_of(start, DIL_HALF)
        kw = k_ref[pl.ds(start, DIL_TK), :]
        vw = v_ref[pl.ds(start, DIL_TK), :]
        valid = jnp.abs((start + col) - (i * DIL_TQ + row)) <= DIL_HALF
        q2 = q_ref[rows, :]
        zero = jnp.zeros_like(q2)

        def one_head(qh):
            s = jnp.where(valid, _nt_dot(qh, kw), NEG_INF)
            m = jnp.max(s, axis=-1, keepdims=True)
            e = jnp.exp(s - m)
            den = jnp.sum(e, axis=-1, keepdims=True)
            return _dot(e.astype(BF16), vw) / den, m + jnp.log(den)

        o0, l0 = one_head(jnp.where(first, q2, zero))
        o1, l1 = one_head(jnp.where(first, zero, q2))
        o_ref[rows, :] = jnp.where(first, o0, o1).astype(o_ref.dtype)
        lse_ref[rows, :] = jnp.where(first, l0, l1)


def _dilated_branch(h_cls, col_base, t):
    dil, length, _ = h_cls.shape
    groups = DIL_W // LANES
    tq = DIL_TQ * DIL_SUBTILES
    o, lse = pl.pallas_call(
        functools.partial(_dil_kernel, length=length),
        out_shape=(jax.ShapeDtypeStruct((length, dil * DIL_W), BF16),
                   jax.ShapeDtypeStruct((length, dil * DIL_W), F32)),
        grid=(dil, groups, length // tq),
        in_specs=[
            pl.BlockSpec((None, tq, LANES), lambda r, p, i: (r, i, col_base + p)),
            pl.BlockSpec((None, length, LANES), lambda r, p, i: (r, 0, col_base + groups + p)),
            pl.BlockSpec((None, length, LANES), lambda r, p, i: (r, 0, col_base + 2 * groups + p)),
        ],
        out_specs=(
            pl.BlockSpec((tq, LANES), lambda r, p, i: (i, r * groups + p)),
            pl.BlockSpec((tq, LANES), lambda r, p, i: (i, r * groups + p)),
        ),
        compiler_params=_params(("parallel", "parallel", "parallel")),
        name=f"dilated_attn_d{dil}",
    )(h_cls, h_cls, h_cls)
    return o.reshape(t, DIL_W), lse.reshape(t, DIL_W)


def _out_even_kernel(x_ref, yr_ref, o1_ref, o2_ref, o3_ref, l1_ref, l2_ref, l3_ref,
                     w_ref, g_ref, b_ref, out_ref):
    l1, l2, l3 = l1_ref[...], l2_ref[...], l3_ref[...]
    m = jnp.maximum(jnp.maximum(l1, l2), l3)
    e1, e2, e3 = jnp.exp(l1 - m), jnp.exp(l2 - m), jnp.exp(l3 - m)
    den = e1 + e2 + e3
    y_dil = (e1 / den) * o1_ref[...].astype(F32) + (e2 / den) * o2_ref[...].astype(F32) \
        + (e3 / den) * o3_ref[...].astype(F32)
    mix = _dot(yr_ref[...], w_ref[:RET_W, :]) + _dot(y_dil.astype(BF16), w_ref[RET_W:, :])
    z = DEEPNORM_ALPHA * x_ref[...] + mix
    out_ref[...] = _layer_norm(z, g_ref[...], b_ref[...])


def _out_even(x, y_ret, outs, lses, w_bf, g, b):
    t = x.shape[0]
    tm = PROJ_TM
    half = pl.BlockSpec((tm, RET_W), lambda i: (i, 0))
    vec = pl.BlockSpec((1, D_MODEL), lambda i: (0, 0))
    return pl.pallas_call(
        _out_even_kernel,
        out_shape=jax.ShapeDtypeStruct((t, D_MODEL), F32),
        grid=(t // tm,),
        in_specs=[pl.BlockSpec((tm, D_MODEL), lambda i: (i, 0))] + [half] * 7
        + [pl.BlockSpec((D_MODEL, D_MODEL), lambda i: (0, 0)), vec, vec],
        out_specs=pl.BlockSpec((tm, D_MODEL), lambda i: (i, 0)),
        compiler_params=_params(("parallel",)),
        name="out_proj_even",
    )(x, y_ret, *outs, *lses, w_bf, g.reshape(1, -1), b.reshape(1, -1))


def _out_odd_kernel(x_ref, y_ref, w_ref, g_ref, b_ref, out_ref):
    z = DEEPNORM_ALPHA * x_ref[...] + _dot(y_ref[...], w_ref[...])
    out_ref[...] = _layer_norm(z, g_ref[...], b_ref[...])


def _out_odd(x, y, w_bf, g, b):
    t = x.shape[0]
    tm = PROJ_TM
    full = pl.BlockSpec((tm, D_MODEL), lambda i: (i, 0))
    vec = pl.BlockSpec((1, D_MODEL), lambda i: (0, 0))
    return pl.pallas_call(
        _out_odd_kernel,
        out_shape=jax.ShapeDtypeStruct((t, D_MODEL), F32),
        grid=(t // tm,),
        in_specs=[full, full, pl.BlockSpec((D_MODEL, D_MODEL), lambda i: (0, 0)), vec, vec],
        out_specs=full,
        compiler_params=_params(("parallel",)),
        name="out_proj_odd",
    )(x, y, w_bf, g.reshape(1, -1), b.reshape(1, -1))


def _even_mixer(x, w_in, w_out, decay_logits, g, b, cos_t, sin_t):
    t = x.shape[0]
    h, *h_classes = _proj_even(x, w_in.astype(BF16), cos_t, sin_t)
    y_ret = _retention(h, decay_logits)
    outs, lses = [], []
    for h_cls, col_base in [(h.reshape(1, t, EVEN_IN), 5 * RET_W // LANES)] + [(hc, 0) for hc in h_classes]:
        o, lse = _dilated_branch(h_cls, col_base, t)
        outs.append(o)
        lses.append(lse)
    return _out_even(x, y_ret, outs, lses, w_out.astype(BF16), g, b)


def _na_bias_table(rpb, rows):
    n_blk = rows // NA_ROWS
    blocks = np.array([0, 1, 2, n_blk - 2, n_blk - 1])
    kstart = np.clip(NA_ROWS * blocks - NA_KH // 2, 0, rows - NA_KROWS)
    r = NA_ROWS * blocks[:, None] + np.arange(NA_ROWS)[None, :]
    r0 = np.clip(r - NA_KH // 2, 0, rows - NA_KH)
    kr = kstart[:, None] + np.arange(NA_KROWS)[None, :]
    row_ok = (kr[:, None, :] >= r0[:, :, None]) & (kr[:, None, :] < r0[:, :, None] + NA_KH)
    dr = kr[:, None, :] - r[:, :, None] + (NA_KH - 1)
    c = np.arange(GRID_W)
    c0 = np.clip(c - NA_KW // 2, 0, GRID_W - NA_KW)
    kc = np.arange(GRID_W)
    col_ok = (kc[None, :] >= c0[:, None]) & (kc[None, :] < c0[:, None] + NA_KW)
    pad = GRID_W - NA_KW
    rp = jnp.pad(rpb.astype(F32), ((0, 0), (0, 0), (pad, pad)))
    by_col = jnp.stack([rp[:, :, GRID_W - 1 - ci:2 * GRID_W - 1 - ci] for ci in range(GRID_W)], axis=2)
    by_col = jnp.where(col_ok[None, None], by_col, NEG_INF)
    outside = jnp.full((N_NA_HEADS, GRID_W, GRID_W), NEG_INF, F32)
    patterns = []
    for pi in range(5):
        q_rows = []
        for qi in range(NA_ROWS):
            blocks_ = [by_col[:, int(dr[pi, qi, ki])] if row_ok[pi, qi, ki] else outside
                       for ki in range(NA_KROWS)]
            q_rows.append(jnp.concatenate(blocks_, axis=-1))
        patterns.append(jnp.concatenate(q_rows, axis=-2))
    return jnp.stack(patterns, axis=0)


def _na_kernel(q_ref, k_ref, v_ref, bias_ref, o_ref, *, rows):
    n_blk = rows // NA_ROWS
    tq = NA_ROWS * GRID_W
    nk = NA_KROWS * GRID_W
    first = _first_head_lanes((1, LANES))
    for sub in range(NA_SUBBLOCKS):
        rb = pl.program_id(1) * NA_SUBBLOCKS + sub
        qrows = slice(sub * tq, (sub + 1) * tq)
        kstart = jnp.clip(NA_ROWS * rb - NA_KH // 2, 0, rows - NA_KROWS) * GRID_W
        kstart = pl.multiple_of(kstart, GRID_W)
        kw = k_ref[pl.ds(kstart, nk), :]
        vw = v_ref[pl.ds(kstart, nk), :]
        pattern = jnp.clip(rb, 0, 2) + jnp.clip(rb - (n_blk - 3), 0, 2)
        q2 = q_ref[qrows, :]
        zero = jnp.zeros_like(q2)

        def one_head(qh, bias):
            s = _nt_dot(qh, kw) + bias
            m = jnp.max(s, axis=-1, keepdims=True)
            e = jnp.exp(s - m)
            den = jnp.sum(e, axis=-1, keepdims=True)
            return _dot(e.astype(BF16), vw) / den

        o0 = one_head(jnp.where(first, q2, zero), bias_ref[pattern, 0])
        o1 = one_head(jnp.where(first, zero, q2), bias_ref[pattern, 1])
        o_ref[qrows, :] = jnp.where(first, o0, o1).astype(o_ref.dtype)


def _neighbourhood_attention(h, rpb):
    t = h.shape[0]
    rows = t // GRID_W
    n_blk = rows // NA_ROWS
    groups = D_MODEL // LANES
    tq = NA_SUBBLOCKS * NA_ROWS * GRID_W
    bias = _na_bias_table(rpb, rows)
    return pl.pallas_call(
        functools.partial(_na_kernel, rows=rows),
        out_shape=jax.ShapeDtypeStruct((t, D_MODEL), BF16),
        grid=(groups, n_blk // NA_SUBBLOCKS),
        in_specs=[
            pl.BlockSpec((tq, LANES), lambda p, rb: (rb, p)),
            pl.BlockSpec((t, LANES), lambda p, rb: (0, groups + p)),
            pl.BlockSpec((t, LANES), lambda p, rb: (0, 2 * groups + p)),
            pl.BlockSpec((bias.shape[0], HEADS_PER_GROUP) + bias.shape[2:], lambda p, rb: (0, p, 0, 0)),
        ],
        out_specs=pl.BlockSpec((tq, LANES), lambda p, rb: (rb, p)),
        compiler_params=_params(("parallel", "parallel")),
        name="neighbourhood_attn",
    )(h, h, h, bias)


def _odd_mixer(x, w_in, w_out, rpb, g, b):
    h = _proj_odd(x, w_in.astype(BF16))
    y = _neighbourhood_attention(h, rpb)
    return _out_odd(x, y, w_out.astype(BF16), g, b)


def _router_kernel(x_ref, wt_ref, b_ref, idx_ref, gate_ref, rank_ref, count_ref):
    tm = x_ref.shape[0]

    @pl.when(pl.program_id(0) == 0)
    def _init():
        count_ref[...] = jnp.zeros_like(count_ref)

    logits = lax.dot_general(wt_ref[...], x_ref[...], (((1,), (1,)), ((), ())),
                             precision=lax.Precision.HIGHEST, preferred_element_type=F32)
    logits = logits + b_ref[...]
    expert = lax.broadcasted_iota(jnp.int32, logits.shape, 0)
    vals, idxs = [], []
    for _ in range(TOP_K):
        m = jnp.max(logits, axis=0, keepdims=True)
        sel = jnp.min(jnp.where(logits == m, expert, N_EXPERTS), axis=0, keepdims=True)
        vals.append(m)
        idxs.append(sel)
        logits = jnp.where(expert == sel, -jnp.inf, logits)
    exps = [jnp.exp(v - vals[0]) for v in vals]
    den = exps[0] + exps[1] + exps[2] + exps[3]
    for k in range(TOP_K):
        idx_ref[k:k + 1, :] = idxs[k]
        gate_ref[k:k + 1, :] = exps[k] / den

    onehots = [(expert == idxs[k]).astype(F32) for k in range(TOP_K)]
    per_tok = onehots[0] + onehots[1] + onehots[2] + onehots[3]
    earlier = lax.broadcasted_iota(jnp.int32, (tm, tm), 0) < lax.broadcasted_iota(jnp.int32, (tm, tm), 1)
    before = count_ref[...] + _dot(per_tok.astype(BF16), earlier.astype(BF16))
    for k in range(TOP_K):
        rank_ref[k:k + 1, :] = jnp.sum(onehots[k] * before, axis=0, keepdims=True).astype(jnp.int32)
    count_ref[...] += jnp.sum(per_tok, axis=1, keepdims=True)


def _router(x, router_w, router_b):
    t = x.shape[0]
    tm = ROUTER_TM
    row = pl.BlockSpec((TOP_K, tm), lambda i: (0, i))
    return pl.pallas_call(
        _router_kernel,
        out_shape=(jax.ShapeDtypeStruct((TOP_K, t), jnp.int32),
                   jax.ShapeDtypeStruct((TOP_K, t), F32),
                   jax.ShapeDtypeStruct((TOP_K, t), jnp.int32),
                   jax.ShapeDtypeStruct((N_EXPERTS, 1), F32)),
        grid=(t // tm,),
        in_specs=[
            pl.BlockSpec((tm, D_MODEL), lambda i: (i, 0)),
            pl.BlockSpec((N_EXPERTS, D_MODEL), lambda i: (0, 0)),
            pl.BlockSpec((N_EXPERTS, 1), lambda i: (0, 0)),
        ],
        out_specs=(row, row, row, pl.BlockSpec((N_EXPERTS, 1), lambda i: (0, 0))),
        compiler_params=_params(("arbitrary",)),
        name="moe_router",
    )(x, router_w.T, router_b.reshape(-1, 1))


def _slot_layout(top_i, rank, counts):
    t = top_i.shape[1]
    bm = MOE_BM
    experts = jnp.arange(N_EXPERTS, dtype=jnp.int32)
    counts = counts.reshape(-1).astype(jnp.int32)
    pcounts = (counts + bm - 1) // bm * bm
    pend = jnp.cumsum(pcounts)
    pstart = pend - pcounts
    pos = rank + jnp.sum(jnp.where(top_i[:, :, None] == experts, pstart, 0), axis=-1)
    n_blk = (TOP_K * t) // bm + N_EXPERTS
    blk_start = jnp.arange(n_blk, dtype=jnp.int32) * bm
    blk_e = jnp.minimum(jnp.sum((pend[None, :] <= blk_start[:, None]).astype(jnp.int32), axis=1),
                        N_EXPERTS - 1)
    n_used = (pend[-1] // bm).astype(jnp.int32).reshape(1)
    return pos.astype(jnp.int32), blk_e, n_used, n_blk


def _row_copy(src, src_row, dst, dst_row, sem):
    return pltpu.make_async_copy(src.at[pl.ds(src_row, 1)], dst.at[pl.ds(dst_row, 1)], sem)


def _dispatch_kernel(pos_ref, x_ref, slots_in_hbm, slots_hbm, sem, *, n_tok, tm):
    del slots_in_hbm
    base = pl.program_id(0) * tm

    def issue(j, carry):
        for k in range(TOP_K):
            _row_copy(x_ref, j, slots_hbm, pos_ref[k * n_tok + base + j], sem).start()
        return carry

    lax.fori_loop(0, tm, issue, 0)
    for k in range(TOP_K):
        pltpu.make_async_copy(x_ref, slots_hbm.at[pl.ds(0, tm)], sem).wait()


def _dispatch(x, pos_flat, n_slots):
    t = x.shape[0]
    tm = DISPATCH_TM
    return pl.pallas_call(
        functools.partial(_dispatch_kernel, n_tok=t, tm=tm),
        out_shape=jax.ShapeDtypeStruct((n_slots, D_MODEL), F32),
        grid_spec=pltpu.PrefetchScalarGridSpec(
            num_scalar_prefetch=1,
            grid=(t // tm,),
            in_specs=[pl.BlockSpec((tm, D_MODEL), lambda i, p: (i, 0)),
                      pl.BlockSpec(memory_space=pl.ANY)],
            out_specs=pl.BlockSpec(memory_space=pl.ANY),
            scratch_shapes=[pltpu.SemaphoreType.DMA],
        ),
        input_output_aliases={2: 0},
        compiler_params=_params(("arbitrary",), has_side_effects=True),
        name="moe_dispatch",
    )(pos_flat, x, jnp.zeros((n_slots, D_MODEL), F32))


def _expert_kernel(blk_e_ref, n_used_ref, x_ref, wu_ref, bu_ref, wd_ref, bd_ref, o_ref, wu_bf_ref, wd_bf_ref):
    b = pl.program_id(0)
    used = b < n_used_ref[0]
    half = LANES // 2
    n_groups = D_FF // LANES
    new_expert = jnp.logical_or(b == 0, blk_e_ref[b] != blk_e_ref[jnp.maximum(b - 1, 0)])

    @pl.when(jnp.logical_and(used, new_expert))
    def _stage_weights():
        wu_bf_ref[...] = wu_ref[...].astype(BF16)
        row = lax.broadcasted_iota(jnp.int32, (LANES, LANES), 0)
        col = lax.broadcasted_iota(jnp.int32, (LANES, LANES), 1)
        select = (col == row // 2 + half * (row % 2)).astype(BF16)
        for g in range(n_groups):
            rows = slice(g * LANES, (g + 1) * LANES)
            wd_bf_ref[rows, :] = _dot(select, wd_ref[rows, :].astype(BF16)).astype(BF16)

    @pl.when(used)
    def _compute():
        xb = x_ref[...].astype(BF16)
        h = _dot(xb, wu_bf_ref[...]) + bu_ref[...]
        even = lax.broadcasted_iota(jnp.int32, (1, LANES), 1) % 2 == 0
        packed = []
        for g in range(n_groups):
            lo = h[:, 2 * g * LANES:(2 * g + 1) * LANES]
            hi = h[:, (2 * g + 1) * LANES:(2 * g + 2) * LANES]
            x_glu = jnp.minimum(jnp.where(even, lo, pltpu.roll(hi, 1, 1)), SWIGLU_LIMIT)
            x_lin = jnp.clip(jnp.where(even, pltpu.roll(lo, LANES - 1, 1), hi), -SWIGLU_LIMIT, SWIGLU_LIMIT)
            packed.append((x_glu * jax.nn.sigmoid(SWIGLU_ALPHA * x_glu) * (x_lin + 1.0)).astype(BF16))
        act = jnp.concatenate(packed, axis=1)
        o_ref[...] = _dot(act, wd_bf_ref[...]) + bd_ref[...]

    @pl.when(jnp.logical_not(used))
    def _unused():
        o_ref[...] = jnp.zeros_like(o_ref)


def _experts(slots, blk_e, n_used, n_blk, w_up, b_up, w_down, b_down):
    bm = MOE_BM

    def wspec(shape):
        return pl.BlockSpec((None,) + shape, lambda b, e, n: (e[b],) + (0,) * len(shape))

    return pl.pallas_call(
        _expert_kernel,
        out_shape=jax.ShapeDtypeStruct(slots.shape, F32),
        grid_spec=pltpu.PrefetchScalarGridSpec(
            num_scalar_prefetch=2,
            grid=(n_blk,),
            in_specs=[
                pl.BlockSpec((bm, D_MODEL), lambda b, e, n: (jnp.maximum(jnp.minimum(b, n[0] - 1), 0), 0)),
                wspec((D_MODEL, 2 * D_FF)), wspec((1, 2 * D_FF)),
                wspec((D_FF, D_MODEL)), wspec((1, D_MODEL)),
            ],
            out_specs=pl.BlockSpec((bm, D_MODEL), lambda b, e, n: (b, 0)),
            scratch_shapes=[pltpu.VMEM((D_MODEL, 2 * D_FF), BF16), pltpu.VMEM((D_FF, D_MODEL), BF16)],
        ),
        compiler_params=_params(("arbitrary",)),
        name="moe_experts",
    )(blk_e, n_used, slots, w_up, b_up, w_down, b_down)


def _combine_kernel(pos_ref, y_hbm, x_ref, gate_ref, g_ref, b_ref, o_ref, buf_ref, sem, *, n_tok, tm):
    base = pl.program_id(0) * tm

    def issue(j, carry):
        tok = base + j
        for k in range(TOP_K):
            _row_copy(y_hbm, pos_ref[k * n_tok + tok], buf_ref.at[k], j, sem).start()
        return carry

    lax.fori_loop(0, tm, issue, 0)
    for k in range(TOP_K):
        pltpu.make_async_copy(y_hbm.at[pl.ds(0, tm)], buf_ref.at[k], sem).wait()
    gates = gate_ref[...]
    ffn = gates[:, 0:1] * buf_ref[0]
    for k in range(1, TOP_K):
        ffn = ffn + gates[:, k:k + 1] * buf_ref[k]
    z = DEEPNORM_ALPHA * x_ref[...] + ffn
    o_ref[...] = _layer_norm(z, g_ref[...], b_ref[...])


def _combine(x, y_slots, pos_flat, gates_tk, g, b):
    t = x.shape[0]
    tm = COMBINE_TM
    full = pl.BlockSpec((tm, D_MODEL), lambda i, p: (i, 0))
    vec = pl.BlockSpec((1, D_MODEL), lambda i, p: (0, 0))
    return pl.pallas_call(
        functools.partial(_combine_kernel, n_tok=t, tm=tm),
        out_shape=jax.ShapeDtypeStruct((t, D_MODEL), F32),
        grid_spec=pltpu.PrefetchScalarGridSpec(
            num_scalar_prefetch=1,
            grid=(t // tm,),
            in_specs=[pl.BlockSpec(memory_space=pl.ANY), full,
                      pl.BlockSpec((tm, TOP_K), lambda i, p: (i, 0)), vec, vec],
            out_specs=full,
            scratch_shapes=[pltpu.VMEM((TOP_K, tm, D_MODEL), F32), pltpu.SemaphoreType.DMA],
        ),
        compiler_params=_params(("arbitrary",)),
        name="moe_combine",
    )(pos_flat, y_slots, x, gates_tk, g.reshape(1, -1), b.reshape(1, -1))


def _moe_layer(x, router_w, router_b, w_up, b_up, w_down, b_down, g, b):
    top_i, gates, rank, counts = _router(x, router_w, router_b)
    pos, blk_e, n_used, n_blk = _slot_layout(top_i, rank, counts)
    pos_flat = pos.reshape(-1)
    slots = _dispatch(x, pos_flat, n_blk * MOE_BM)
    y_slots = _experts(slots, blk_e, n_used, n_blk, w_up, b_up[:, None, :], w_down, b_down[:, None, :])
    return _combine(x, y_slots, pos_flat, gates.T, g, b)


def kernel(x, ab_w_in, ab_w_out, ret_decay, c_w_in, c_w_out, c_rpb, ln_g, ln_b,
           router_w, router_b, exp_w_up, exp_b_up, exp_w_down, exp_b_down):
    b_, t_, d_ = x.shape
    assert b_ == 1 and d_ == D_MODEL
    xs = x.reshape(t_, d_)
    cos_t, sin_t = _rope_tables(t_)
    for layer in range(DEPTH):
        j = layer // 2
        if layer % 2 == 0:
            xs = _even_mixer(xs, ab_w_in[j], ab_w_out[j], ret_decay[j], ln_g[layer, 0], ln_b[layer, 0],
                             cos_t, sin_t)
        else:
            xs = _odd_mixer(xs, c_w_in[j], c_w_out[j], c_rpb[j], ln_g[layer, 0], ln_b[layer, 0])
        xs = _moe_layer(xs, router_w[layer], router_b[layer], exp_w_up[layer], exp_b_up[layer],
                        exp_w_down[layer], exp_b_down[layer], ln_g[layer, 1], ln_b[layer, 1])
    return xs.reshape(b_, t_, d_)
```

```python
import functools

import jax
import jax.numpy as jnp
import numpy as np
from jax import lax
from jax.experimental import pallas as pl
from jax.experimental.pallas import tpu as pltpu

F32 = jnp.float32
BF16 = jnp.bfloat16

D_MODEL = 1024
HEAD_DIM = 64
LANES = 128
HEADS_PER_GROUP = LANES // HEAD_DIM
N_RET_HEADS = 8
N_DIL_HEADS = 8
N_NA_HEADS = 16
RET_W = N_RET_HEADS * HEAD_DIM
DIL_W = N_DIL_HEADS * HEAD_DIM
EVEN_IN = 5 * RET_W + 3 * DIL_W
ODD_IN = 3 * D_MODEL
ROPE_THETA = 10000.0
DIL_CONFIGS = ((128, 1), (512, 4), (2048, 16))
DIL_HALF = 64
GRID_W = 64
NA_KH = 8
NA_KW = 16
N_EXPERTS = 32
TOP_K = 4
D_FF = D_MODEL
SWIGLU_LIMIT = 7.0
SWIGLU_ALPHA = 1.702
LN_EPS = 1e-5
GN_EPS = 1e-6
NEG_INF = -1e30
DEPTH = 4
DEEPNORM_ALPHA = (2.0 * DEPTH) ** 0.25
QK_SCALE = HEAD_DIM ** -0.5

VMEM_LIMIT_BYTES = 56 * 1024 * 1024

PROJ_TM = 512
RET_CHUNK = 256
DIL_TQ = 128
DIL_TK = DIL_TQ + 2 * DIL_HALF
DIL_SUBTILES = 4
NA_ROWS = 2
NA_KROWS = NA_ROWS + NA_KH - 1
NA_SUBBLOCKS = 4
ROUTER_TM = 512
MOE_BM = 512
DISPATCH_TM = 512
COMBINE_TM = 256


def _params(sem, **kw):
    return pltpu.CompilerParams(dimension_semantics=sem, vmem_limit_bytes=VMEM_LIMIT_BYTES, **kw)


def _nt_dot(a, b):
    return lax.dot_general(a, b, (((1,), (1,)), ((), ())), preferred_element_type=F32)


def _dot(a, b):
    return jnp.dot(a, b, preferred_element_type=F32)


def _layer_norm(z, g, b):
    mu = jnp.mean(z, axis=-1, keepdims=True)
    d = z - mu
    var = jnp.mean(d * d, axis=-1, keepdims=True)
    return d * lax.rsqrt(var + LN_EPS) * g + b


def _first_head_lanes(shape):
    return lax.broadcasted_iota(jnp.int32, shape, len(shape) - 1) < HEAD_DIM


def _rope_tables(t):
    pos = jnp.arange(t, dtype=F32)
    inv = ROPE_THETA ** (-jnp.arange(0, HEAD_DIM, 2, dtype=F32) / HEAD_DIM)
    ang = pos[:, None] * inv[None, :]
    cos, sin = jnp.cos(ang), jnp.sin(ang)
    return (jnp.concatenate([cos, cos, cos, cos], axis=1),
            jnp.concatenate([-sin, sin, -sin, sin], axis=1))


def _rope(h, c, s):
    half = HEAD_DIM // 2
    lane = lax.broadcasted_iota(jnp.int32, h.shape, 1)
    first_half = (lane // half) % 2 == 0
    partner = jnp.where(first_half, pltpu.roll(h, LANES - half, 1), pltpu.roll(h, half, 1))
    return h * c + partner * s


def _proj_even_kernel(x_ref, w_ref, c_ref, s_ref, o_ref, *class_refs):
    tm = x_ref.shape[0]
    xb = x_ref[...].astype(BF16)
    c = c_ref[...]
    s = s_ref[...]
    rope_groups = {0: 1.0, 1: QK_SCALE, 5: QK_SCALE, 6: 1.0}
    for g in range(EVEN_IN // RET_W):
        h = _dot(xb, w_ref[:, g * RET_W:(g + 1) * RET_W])
        for j in range(RET_W // LANES):
            hj = h[:, j * LANES:(j + 1) * LANES]
            if g in rope_groups:
                hj = _rope(hj, c, s) * rope_groups[g]
            col = g * RET_W + j * LANES
            o_ref[:, col:col + LANES] = hj.astype(BF16)
    h_dil = o_ref[:, 5 * RET_W:]
    row = lax.broadcasted_iota(jnp.int32, (tm, tm), 0)
    col = lax.broadcasted_iota(jnp.int32, (tm, tm), 1)
    for cls_ref, (_, dil) in zip(class_refs, DIL_CONFIGS[1:]):
        per_class = tm // dil
        select = (col == dil * (row % per_class) + row // per_class).astype(BF16)
        regrouped = _dot(select, h_dil).astype(BF16)
        for r in range(dil):
            cls_ref[r] = regrouped[r * per_class:(r + 1) * per_class]


def _proj_even(x, w_bf, cos_t, sin_t):
    t = x.shape[0]
    tm = PROJ_TM
    dil_w = 3 * DIL_W
    class_shapes = [jax.ShapeDtypeStruct((dil, t // dil, dil_w), BF16) for _, dil in DIL_CONFIGS[1:]]
    class_specs = [pl.BlockSpec((dil, tm // dil, dil_w), lambda i: (0, i, 0)) for _, dil in DIL_CONFIGS[1:]]
    return pl.pallas_call(
        _proj_even_kernel,
        out_shape=[jax.ShapeDtypeStruct((t, EVEN_IN), BF16)] + class_shapes,
        grid=(t // tm,),
        in_specs=[
            pl.BlockSpec((tm, D_MODEL), lambda i: (i, 0)),
            pl.BlockSpec((D_MODEL, EVEN_IN), lambda i: (0, 0)),
            pl.BlockSpec((tm, LANES), lambda i: (i, 0)),
            pl.BlockSpec((tm, LANES), lambda i: (i, 0)),
        ],
        out_specs=[pl.BlockSpec((tm, EVEN_IN), lambda i: (i, 0))] + class_specs,
        compiler_params=_params(("parallel",)),
        name="proj_even",
    )(x, w_bf, cos_t, sin_t)


def _proj_odd_kernel(x_ref, w_ref, o_ref):
    xb = x_ref[...].astype(BF16)
    for g in range(3):
        h = _dot(xb, w_ref[:, g * D_MODEL:(g + 1) * D_MODEL])
        if g == 0:
            h = h * QK_SCALE
        o_ref[:, g * D_MODEL:(g + 1) * D_MODEL] = h.astype(BF16)


def _proj_odd(x, w_bf):
    t = x.shape[0]
    tm = PROJ_TM
    return pl.pallas_call(
        _proj_odd_kernel,
        out_shape=jax.ShapeDtypeStruct((t, ODD_IN), BF16),
        grid=(t // tm,),
        in_specs=[
            pl.BlockSpec((tm, D_MODEL), lambda i: (i, 0)),
            pl.BlockSpec((D_MODEL, ODD_IN), lambda i: (0, 0)),
        ],
        out_specs=pl.BlockSpec((tm, ODD_IN), lambda i: (i, 0)),
        compiler_params=_params(("parallel",)),
        name="proj_odd",
    )(x, w_bf)


def _ret_kernel(lg_ref, q_ref, k_ref, v_ref, g_ref, *rest, rev):
    if rev:
        o_ref, r_ref, d0_ref, d1_ref, z_ref, xi_ref, dc_ref = rest
        prev_ref = None
    else:
        prev_ref, o_ref, r_ref, d0_ref, d1_ref, z_ref, xi_ref, dc_ref = rest
    C = RET_CHUNK
    groups = RET_W // LANES
    c = pl.program_id(0)
    first = _first_head_lanes((1, LANES))
    direction = 1 if rev else 0

    @pl.when(c == 0)
    def _init():
        row = lax.broadcasted_iota(jnp.int32, (C, C), 0)
        col = lax.broadcasted_iota(jnp.int32, (C, C), 1)
        dist = (col - row) if rev else (row - col)
        keep = (dist > 0) if rev else (dist >= 0)
        distf = jnp.maximum(dist, 0).astype(F32)
        idx = lax.broadcasted_iota(jnp.int32, (C, LANES), 0).astype(F32)
        for p in range(groups):
            lg0 = lg_ref[direction, 2 * p]
            lg1 = lg_ref[direction, 2 * p + 1]
            d0_ref[p] = jnp.where(keep, jnp.exp(lg0 * distf), 0.0)
            d1_ref[p] = jnp.where(keep, jnp.exp(lg1 * distf), 0.0)
            lgl = jnp.where(first, lg0, lg1)
            z_ref[p] = jnp.exp(lgl * (idx if rev else (C - 1.0 - idx)))
            xi_ref[p] = jnp.exp(lgl * ((C - idx) if rev else (idx + 1.0)))
            dc_ref[p] = jnp.exp(lgl * float(C))
        r_ref[...] = jnp.zeros_like(r_ref)

    rr = lax.broadcasted_iota(jnp.int32, (LANES, LANES), 0) < HEAD_DIM
    cc = lax.broadcasted_iota(jnp.int32, (LANES, LANES), 1) < HEAD_DIM
    inv_n = 1.0 / HEAD_DIM
    for p in range(groups):
        lanes = slice(p * LANES, (p + 1) * LANES)
        q2 = q_ref[:, lanes]
        k2 = k_ref[:, lanes]
        v2 = v_ref[:, lanes]
        zero = jnp.zeros_like(q2)
        q0 = jnp.where(first, q2, zero)
        q1 = jnp.where(first, zero, q2)
        a0 = (_nt_dot(q0, k2) * d0_ref[p]).astype(BF16)
        a1 = (_nt_dot(q1, k2) * d1_ref[p]).astype(BF16)
        state = r_ref[p]
        cross = _dot(q2, state.astype(BF16)) * xi_ref[p]
        o = jnp.where(first, _dot(a0, v2), _dot(a1, v2)) + cross

        kz = (k2.astype(F32) * z_ref[p]).T.astype(BF16)
        r_ref[p] = jnp.where(rr == cc, state * dc_ref[p] + _dot(kz, v2), 0.0)

        s_first = jnp.sum(jnp.where(first, o, 0.0), axis=-1, keepdims=True)
        s_second = jnp.sum(jnp.where(first, 0.0, o), axis=-1, keepdims=True)
        d = o - jnp.where(first, s_first, s_second) * inv_n
        dd = d * d
        v_first = jnp.sum(jnp.where(first, dd, 0.0), axis=-1, keepdims=True)
        v_second = jnp.sum(jnp.where(first, 0.0, dd), axis=-1, keepdims=True)
        var = jnp.where(first, v_first, v_second) * inv_n
        gate = g_ref[:, lanes].astype(F32)
        y = gate * jax.nn.sigmoid(gate) * (d * lax.rsqrt(var + GN_EPS))
        if prev_ref is not None:
            y = y + prev_ref[:, lanes].astype(F32)
        o_ref[:, lanes] = y.astype(o_ref.dtype)


def _retention_pass(h, log_gamma, prev, rev):
    t = h.shape[0]
    C = RET_CHUNK
    nc = t // C
    groups = RET_W // LANES

    def col_spec(group):
        return pl.BlockSpec((C, RET_W), lambda c, group=group: ((nc - 1 - c) if rev else c, group))

    in_specs = [
        pl.BlockSpec(memory_space=pltpu.SMEM),
        col_spec(0), col_spec(1), col_spec(2), col_spec(4 if rev else 3),
    ]
    args = [log_gamma, h, h, h, h]
    if not rev:
        in_specs.append(col_spec(0))
        args.append(prev)
    return pl.pallas_call(
        functools.partial(_ret_kernel, rev=rev),
        out_shape=jax.ShapeDtypeStruct((t, RET_W), BF16),
        grid=(nc,),
        in_specs=in_specs,
        out_specs=col_spec(0),
        scratch_shapes=[
            pltpu.VMEM((groups, LANES, LANES), F32),
            pltpu.VMEM((groups, C, C), F32),
            pltpu.VMEM((groups, C, C), F32),
            pltpu.VMEM((groups, C, LANES), F32),
            pltpu.VMEM((groups, C, LANES), F32),
            pltpu.VMEM((groups, 1, LANES), F32),
        ],
        compiler_params=_params(("arbitrary",)),
        name="retention_bwd" if rev else "retention_fwd",
    )(*args)


def _retention(h, decay_logits):
    log_gamma = jax.nn.log_sigmoid(decay_logits.astype(F32))
    y_bwd = _retention_pass(h, log_gamma, None, rev=True)
    return _retention_pass(h, log_gamma, y_bwd, rev=False)


def _dil_kernel(q_ref, k_ref, v_ref, o_ref, lse_ref, *, length):
    first = _first_head_lanes((1, LANES))
    row = lax.broadcasted_iota(jnp.int32, (DIL_TQ, DIL_TK), 0)
    col = lax.broadcasted_iota(jnp.int32, (DIL_TQ, DIL_TK), 1)
    for sub in range(DIL_SUBTILES):
        i = pl.program_id(2) * DIL_SUBTILES + sub
        rows = slice(sub * DIL_TQ, (sub + 1) * DIL_TQ)
        start = jnp.clip(i * DIL_TQ - DIL_HALF, 0, length - DIL_TK)
        start = pl.multiple_of(start, DIL_HALF)
        kw = k_ref[pl.ds(start, DIL_TK), :]
        vw = v_ref[pl.ds(start, DIL_TK), :]
        valid = jnp.abs((start + col) - (i * DIL_TQ + row)) <= DIL_HALF
        q2 = q_ref[rows, :]
        zero = jnp.zeros_like(q2)

        def one_head(qh):
            s = jnp.where(valid, _nt_dot(qh, kw), NEG_INF)
            m = jnp.max(s, axis=-1, keepdims=True)
            e = jnp.exp(s - m)
            den = jnp.sum(e, axis=-1, keepdims=True)
            return _dot(e.astype(BF16), vw) / den, m + jnp.log(den)

        o0, l0 = one_head(jnp.where(first, q2, zero))
        o1, l1 = one_head(jnp.where(first, zero, q2))
        o_ref[rows, :] = jnp.where(first, o0, o1).astype(o_ref.dtype)
        lse_ref[rows, :] = jnp.where(first, l0, l1)


def _dilated_branch(h_cls, col_base, t):
    dil, length, _ = h_cls.shape
    groups = DIL_W // LANES
    tq = DIL_TQ * DIL_SUBTILES
    o, lse = pl.pallas_call(
        functools.partial(_dil_kernel, length=length),
        out_shape=(jax.ShapeDtypeStruct((length, dil * DIL_W), BF16),
                   jax.ShapeDtypeStruct((length, dil * DIL_W), F32)),
        grid=(dil, groups, length // tq),
        in_specs=[
            pl.BlockSpec((None, tq, LANES), lambda r, p, i: (r, i, col_base + p)),
            pl.BlockSpec((None, length, LANES), lambda r, p, i: (r, 0, col_base + groups + p)),
            pl.BlockSpec((None, length, LANES), lambda r, p, i: (r, 0, col_base + 2 * groups + p)),
        ],
        out_specs=(
            pl.BlockSpec((tq, LANES), lambda r, p, i: (i, r * groups + p)),
            pl.BlockSpec((tq, LANES), lambda r, p, i: (i, r * groups + p)),
        ),
        compiler_params=_params(("parallel", "parallel", "parallel")),
        name=f"dilated_attn_d{dil}",
    )(h_cls, h_cls, h_cls)
    return o.reshape(t, DIL_W), lse.reshape(t, DIL_W)


def _out_even_kernel(x_ref, yr_ref, o1_ref, o2_ref, o3_ref, l1_ref, l2_ref, l3_ref,
                     w_ref, g_ref, b_ref, out_ref):
    l1, l2, l3 = l1_ref[...], l2_ref[...], l3_ref[...]
    m = jnp.maximum(jnp.maximum(l1, l2), l3)
    e1, e2, e3 = jnp.exp(l1 - m), jnp.exp(l2 - m), jnp.exp(l3 - m)
    den = e1 + e2 + e3
    y_dil = (e1 / den) * o1_ref[...].astype(F32) + (e2 / den) * o2_ref[...].astype(F32) \
        + (e3 / den) * o3_ref[...].astype(F32)
    mix = _dot(yr_ref[...], w_ref[:RET_W, :]) + _dot(y_dil.astype(BF16), w_ref[RET_W:, :])
    z = DEEPNORM_ALPHA * x_ref[...] + mix
    out_ref[...] = _layer_norm(z, g_ref[...], b_ref[...])


def _out_even(x, y_ret, outs, lses, w_bf, g, b):
    t = x.shape[0]
    tm = PROJ_TM
    half = pl.BlockSpec((tm, RET_W), lambda i: (i, 0))
    vec = pl.BlockSpec((1, D_MODEL), lambda i: (0, 0))
    return pl.pallas_call(
        _out_even_kernel,
        out_shape=jax.ShapeDtypeStruct((t, D_MODEL), F32),
        grid=(t // tm,),
        in_specs=[pl.BlockSpec((tm, D_MODEL), lambda i: (i, 0))] + [half] * 7
        + [pl.BlockSpec((D_MODEL, D_MODEL), lambda i: (0, 0)), vec, vec],
        out_specs=pl.BlockSpec((tm, D_MODEL), lambda i: (i, 0)),
        compiler_params=_params(("parallel",)),
        name="out_proj_even",
    )(x, y_ret, *outs, *lses, w_bf, g.reshape(1, -1), b.reshape(1, -1))


def _out_odd_kernel(x_ref, y_ref, w_ref, g_ref, b_ref, out_ref):
    z = DEEPNORM_ALPHA * x_ref[...] + _dot(y_ref[...], w_ref[...])
    out_ref[...] = _layer_norm(z, g_ref[...], b_ref[...])


def _out_odd(x, y, w_bf, g, b):
    t = x.shape[0]
    tm = PROJ_TM
    full = pl.BlockSpec((tm, D_MODEL), lambda i: (i, 0))
    vec = pl.BlockSpec((1, D_MODEL), lambda i: (0, 0))
    return pl.pallas_call(
        _out_odd_kernel,
        out_shape=jax.ShapeDtypeStruct((t, D_MODEL), F32),
        grid=(t // tm,),
        in_specs=[full, full, pl.BlockSpec((D_MODEL, D_MODEL), lambda i: (0, 0)), vec, vec],
        out_specs=full,
        compiler_params=_params(("parallel",)),
        name="out_proj_odd",
    )(x, y, w_bf, g.reshape(1, -1), b.reshape(1, -1))


def _even_mixer(x, w_in, w_out, decay_logits, g, b, cos_t, sin_t):
    t = x.shape[0]
    h, *h_classes = _proj_even(x, w_in.astype(BF16), cos_t, sin_t)
    y_ret = _retention(h, decay_logits)
    outs, lses = [], []
    for h_cls, col_base in [(h.reshape(1, t, EVEN_IN), 5 * RET_W // LANES)] + [(hc, 0) for hc in h_classes]:
        o, lse = _dilated_branch(h_cls, col_base, t)
        outs.append(o)
        lses.append(lse)
    return _out_even(x, y_ret, outs, lses, w_out.astype(BF16), g, b)


def _na_bias_table(rpb, rows):
    n_blk = rows // NA_ROWS
    blocks = np.array([0, 1, 2, n_blk - 2, n_blk - 1])
    kstart = np.clip(NA_ROWS * blocks - NA_KH // 2, 0, rows - NA_KROWS)
    r = NA_ROWS * blocks[:, None] + np.arange(NA_ROWS)[None, :]
    r0 = np.clip(r - NA_KH // 2, 0, rows - NA_KH)
    kr = kstart[:, None] + np.arange(NA_KROWS)[None, :]
    row_ok = (kr[:, None, :] >= r0[:, :, None]) & (kr[:, None, :] < r0[:, :, None] + NA_KH)
    dr = kr[:, None, :] - r[:, :, None] + (NA_KH - 1)
    c = np.arange(GRID_W)
    c0 = np.clip(c - NA_KW // 2, 0, GRID_W - NA_KW)
    kc = np.arange(GRID_W)
    col_ok = (kc[None, :] >= c0[:, None]) & (kc[None, :] < c0[:, None] + NA_KW)
    pad = GRID_W - NA_KW
    rp = jnp.pad(rpb.astype(F32), ((0, 0), (0, 0), (pad, pad)))
    by_col = jnp.stack([rp[:, :, GRID_W - 1 - ci:2 * GRID_W - 1 - ci] for ci in range(GRID_W)], axis=2)
    by_col = jnp.where(col_ok[None, None], by_col, NEG_INF)
    outside = jnp.full((N_NA_HEADS, GRID_W, GRID_W), NEG_INF, F32)
    patterns = []
    for pi in range(5):
        q_rows = []
        for qi in range(NA_ROWS):
            blocks_ = [by_col[:, int(dr[pi, qi, ki])] if row_ok[pi, qi, ki] else outside
                       for ki in range(NA_KROWS)]
            q_rows.append(jnp.concatenate(blocks_, axis=-1))
        patterns.append(jnp.concatenate(q_rows, axis=-2))
    return jnp.stack(patterns, axis=0)


def _na_kernel(q_ref, k_ref, v_ref, bias_ref, o_ref, *, rows):
    n_blk = rows // NA_ROWS
    tq = NA_ROWS * GRID_W
    nk = NA_KROWS * GRID_W
    first = _first_head_lanes((1, LANES))
    for sub in range(NA_SUBBLOCKS):
        rb = pl.program_id(1) * NA_SUBBLOCKS + sub
        qrows = slice(sub * tq, (sub + 1) * tq)
        kstart = jnp.clip(NA_ROWS * rb - NA_KH // 2, 0, rows - NA_KROWS) * GRID_W
        kstart = pl.multiple_of(kstart, GRID_W)
        kw = k_ref[pl.ds(kstart, nk), :]
        vw = v_ref[pl.ds(kstart, nk), :]
        pattern = jnp.clip(rb, 0, 2) + jnp.clip(rb - (n_blk - 3), 0, 2)
        q2 = q_ref[qrows, :]
        zero = jnp.zeros_like(q2)

        def one_head(qh, bias):
            s = _nt_dot(qh, kw) + bias
            m = jnp.max(s, axis=-1, keepdims=True)
            e = jnp.exp(s - m)
            den = jnp.sum(e, axis=-1, keepdims=True)
            return _dot(e.astype(BF16), vw) / den

        o0 = one_head(jnp.where(first, q2, zero), bias_ref[pattern, 0])
        o1 = one_head(jnp.where(first, zero, q2), bias_ref[pattern, 1])
        o_ref[qrows, :] = jnp.where(first, o0, o1).astype(o_ref.dtype)


def _neighbourhood_attention(h, rpb):
    t = h.shape[0]
    rows = t // GRID_W
    n_blk = rows // NA_ROWS
    groups = D_MODEL // LANES
    tq = NA_SUBBLOCKS * NA_ROWS * GRID_W
    bias = _na_bias_table(rpb, rows)
    return pl.pallas_call(
        functools.partial(_na_kernel, rows=rows),
        out_shape=jax.ShapeDtypeStruct((t, D_MODEL), BF16),
        grid=(groups, n_blk // NA_SUBBLOCKS),
        in_specs=[
            pl.BlockSpec((tq, LANES), lambda p, rb: (rb, p)),
            pl.BlockSpec((t, LANES), lambda p, rb: (0, groups + p)),
            pl.BlockSpec((t, LANES), lambda p, rb: (0, 2 * groups + p)),
            pl.BlockSpec((bias.shape[0], HEADS_PER_GROUP) + bias.shape[2:], lambda p, rb: (0, p, 0, 0)),
        ],
        out_specs=pl.BlockSpec((tq, LANES), lambda p, rb: (rb, p)),
        compiler_params=_params(("parallel", "parallel")),
        name="neighbourhood_attn",
    )(h, h, h, bias)


def _odd_mixer(x, w_in, w_out, rpb, g, b):
    h = _proj_odd(x, w_in.astype(BF16))
    y = _neighbourhood_attention(h, rpb)
    return _out_odd(x, y, w_out.astype(BF16), g, b)


def _router_kernel(x_ref, wt_ref, b_ref, idx_ref, gate_ref, rank_ref, count_ref):
    tm = x_ref.shape[0]

    @pl.when(pl.program_id(0) == 0)
    def _init():
        count_ref[...] = jnp.zeros_like(count_ref)

    logits = lax.dot_general(wt_ref[...], x_ref[...], (((1,), (1,)), ((), ())),
                             precision=lax.Precision.HIGHEST, preferred_element_type=F32)
    logits = logits + b_ref[...]
    expert = lax.broadcasted_iota(jnp.int32, logits.shape, 0)
    vals, idxs = [], []
    for _ in range(TOP_K):
        m = jnp.max(logits, axis=0, keepdims=True)
        sel = jnp.min(jnp.where(logits == m, expert, N_EXPERTS), axis=0, keepdims=True)
        vals.append(m)
        idxs.append(sel)
        logits = jnp.where(expert == sel, -jnp.inf, logits)
    exps = [jnp.exp(v - vals[0]) for v in vals]
    den = exps[0] + exps[1] + exps[2] + exps[3]
    for k in range(TOP_K):
        idx_ref[k:k + 1, :] = idxs[k]
        gate_ref[k:k + 1, :] = exps[k] / den

    onehots = [(expert == idxs[k]).astype(F32) for k in range(TOP_K)]
    per_tok = onehots[0] + onehots[1] + onehots[2] + onehots[3]
    earlier = lax.broadcasted_iota(jnp.int32, (tm, tm), 0) < lax.broadcasted_iota(jnp.int32, (tm, tm), 1)
    before = count_ref[...] + _dot(per_tok.astype(BF16), earlier.astype(BF16))
    for k in range(TOP_K):
        rank_ref[k:k + 1, :] = jnp.sum(onehots[k] * before, axis=0, keepdims=True).astype(jnp.int32)
    count_ref[...] += jnp.sum(per_tok, axis=1, keepdims=True)


def _router(x, router_w, router_b):
    t = x.shape[0]
    tm = ROUTER_TM
    row = pl.BlockSpec((TOP_K, tm), lambda i: (0, i))
    return pl.pallas_call(
        _router_kernel,
        out_shape=(jax.ShapeDtypeStruct((TOP_K, t), jnp.int32),
                   jax.ShapeDtypeStruct((TOP_K, t), F32),
                   jax.ShapeDtypeStruct((TOP_K, t), jnp.int32),
                   jax.ShapeDtypeStruct((N_EXPERTS, 1), F32)),
        grid=(t // tm,),
        in_specs=[
            pl.BlockSpec((tm, D_MODEL), lambda i: (i, 0)),
            pl.BlockSpec((N_EXPERTS, D_MODEL), lambda i: (0, 0)),
            pl.BlockSpec((N_EXPERTS, 1), lambda i: (0, 0)),
        ],
        out_specs=(row, row, row, pl.BlockSpec((N_EXPERTS, 1), lambda i: (0, 0))),
        compiler_params=_params(("arbitrary",)),
        name="moe_router",
    )(x, router_w.T, router_b.reshape(-1, 1))


def _slot_layout(top_i, rank, counts):
    t = top_i.shape[1]
    bm = MOE_BM
    experts = jnp.arange(N_EXPERTS, dtype=jnp.int32)
    counts = counts.reshape(-1).astype(jnp.int32)
    pcounts = (counts + bm - 1) // bm * bm
    pend = jnp.cumsum(pcounts)
    pstart = pend - pcounts
    pos = rank + jnp.sum(jnp.where(top_i[:, :, None] == experts, pstart, 0), axis=-1)
    n_blk = (TOP_K * t) // bm + N_EXPERTS
    blk_start = jnp.arange(n_blk, dtype=jnp.int32) * bm
    blk_e = jnp.minimum(jnp.sum((pend[None, :] <= blk_start[:, None]).astype(jnp.int32), axis=1),
                        N_EXPERTS - 1)
    n_used = (pend[-1] // bm).astype(jnp.int32).reshape(1)
    return pos.astype(jnp.int32), blk_e, n_used, n_blk, pstart + counts, pend


def _row_copy(src, src_row, dst, dst_row, sem):
    return pltpu.make_async_copy(src.at[pl.ds(src_row, 1)], dst.at[pl.ds(dst_row, 1)], sem)


def _dispatch_kernel(pos_ref, pad_lo_ref, pad_hi_ref, n_used_ref, x_ref, slots_hbm, zeros_ref, sem, fill_sem,
                     *, n_tok, tm, n_blk):
    bm = MOE_BM
    base = pl.program_id(0) * tm

    @pl.when(pl.program_id(0) == 0)
    def _fill_padding():
        zeros_ref[...] = jnp.zeros_like(zeros_ref)

        def pad_row(r):
            return _row_copy(zeros_ref, 0, slots_hbm, r, fill_sem)

        def unused_block(blk):
            return pltpu.make_async_copy(zeros_ref, slots_hbm.at[pl.ds(blk * bm, bm)], fill_sem)

        def for_each_fill(act):
            def per_expert(e, carry):
                lax.fori_loop(pad_lo_ref[e], pad_hi_ref[e], lambda r, c: (act(pad_row(r)), c)[1], 0)
                return carry
            lax.fori_loop(0, N_EXPERTS, per_expert, 0)
            lax.fori_loop(n_used_ref[0], n_blk, lambda blk, c: (act(unused_block(blk)), c)[1], 0)

        for_each_fill(lambda copy: copy.start())
        for_each_fill(lambda copy: copy.wait())

    def issue(j, carry):
        for k in range(TOP_K):
            _row_copy(x_ref, j, slots_hbm, pos_ref[k * n_tok + base + j], sem).start(priority=k % 2)
        return carry

    lax.fori_loop(0, tm, issue, 0)
    for k in range(TOP_K):
        pltpu.make_async_copy(x_ref, slots_hbm.at[pl.ds(0, tm)], sem).wait()


def _dispatch(x, pos_flat, pad_lo, pad_hi, n_used, n_blk):
    t = x.shape[0]
    tm = DISPATCH_TM
    return pl.pallas_call(
        functools.partial(_dispatch_kernel, n_tok=t, tm=tm, n_blk=n_blk),
        out_shape=jax.ShapeDtypeStruct((n_blk * MOE_BM, D_MODEL), F32),
        grid_spec=pltpu.PrefetchScalarGridSpec(
            num_scalar_prefetch=4,
            grid=(t // tm,),
            in_specs=[pl.BlockSpec((tm, D_MODEL), lambda i, *_: (i, 0))],
            out_specs=pl.BlockSpec(memory_space=pl.ANY),
            scratch_shapes=[pltpu.VMEM((MOE_BM, D_MODEL), F32),
                            pltpu.SemaphoreType.DMA, pltpu.SemaphoreType.DMA],
        ),
        compiler_params=_params(("arbitrary",), has_side_effects=True),
        name="moe_dispatch",
    )(pos_flat, pad_lo, pad_hi, n_used, x)


def _expert_kernel(blk_e_ref, n_used_ref, x_ref, wu_ref, bu_ref, wd_ref, bd_ref, o_ref, wu_bf_ref, wd_bf_ref):
    b = pl.program_id(0)
    used = b < n_used_ref[0]
    half = LANES // 2
    n_groups = D_FF // LANES
    new_expert = jnp.logical_or(b == 0, blk_e_ref[b] != blk_e_ref[jnp.maximum(b - 1, 0)])

    @pl.when(jnp.logical_and(used, new_expert))
    def _stage_weights():
        wu_bf_ref[...] = wu_ref[...].astype(BF16)
        row = lax.broadcasted_iota(jnp.int32, (LANES, LANES), 0)
        col = lax.broadcasted_iota(jnp.int32, (LANES, LANES), 1)
        select = (col == row // 2 + half * (row % 2)).astype(BF16)
        for g in range(n_groups):
            rows = slice(g * LANES, (g + 1) * LANES)
            wd_bf_ref[rows, :] = _dot(select, wd_ref[rows, :].astype(BF16)).astype(BF16)

    @pl.when(used)
    def _compute():
        xb = x_ref[...].astype(BF16)
        h = _dot(xb, wu_bf_ref[...]) + bu_ref[...]
        even = lax.broadcasted_iota(jnp.int32, (1, LANES), 1) % 2 == 0
        packed = []
        for g in range(n_groups):
            lo = h[:, 2 * g * LANES:(2 * g + 1) * LANES]
            hi = h[:, (2 * g + 1) * LANES:(2 * g + 2) * LANES]
            x_glu = jnp.minimum(jnp.where(even, lo, pltpu.roll(hi, 1, 1)), SWIGLU_LIMIT)
            x_lin = jnp.clip(jnp.where(even, pltpu.roll(lo, LANES - 1, 1), hi), -SWIGLU_LIMIT, SWIGLU_LIMIT)
            packed.append((x_glu * jax.nn.sigmoid(SWIGLU_ALPHA * x_glu) * (x_lin + 1.0)).astype(BF16))
        act = jnp.concatenate(packed, axis=1)
        o_ref[...] = _dot(act, wd_bf_ref[...]) + bd_ref[...]

    @pl.when(jnp.logical_not(used))
    def _unused():
        o_ref[...] = jnp.zeros_like(o_ref)


def _experts(slots, blk_e, n_used, n_blk, layer, w_up, b_up, w_down, b_down):
    bm = MOE_BM

    def wspec(shape):
        return pl.BlockSpec((None, None) + shape, lambda b, e, n: (layer, e[b], 0, 0))

    return pl.pallas_call(
        _expert_kernel,
        out_shape=jax.ShapeDtypeStruct(slots.shape, F32),
        grid_spec=pltpu.PrefetchScalarGridSpec(
            num_scalar_prefetch=2,
            grid=(n_blk,),
            in_specs=[
                pl.BlockSpec((bm, D_MODEL), lambda b, e, n: (jnp.maximum(jnp.minimum(b, n[0] - 1), 0), 0)),
                wspec((D_MODEL, 2 * D_FF)), wspec((1, 2 * D_FF)),
                wspec((D_FF, D_MODEL)), wspec((1, D_MODEL)),
            ],
            out_specs=pl.BlockSpec((bm, D_MODEL), lambda b, e, n: (b, 0)),
            scratch_shapes=[pltpu.VMEM((D_MODEL, 2 * D_FF), BF16), pltpu.VMEM((D_FF, D_MODEL), BF16)],
        ),
        compiler_params=_params(("arbitrary",)),
        name="moe_experts",
    )(blk_e, n_used, slots, w_up, b_up, w_down, b_down)


def _combine_kernel(pos_ref, y_hbm, x_ref, gate_ref, g_ref, b_ref, o_ref, buf_ref, sem, *, n_tok, tm):
    base = pl.program_id(0) * tm

    def issue(j, carry):
        tok = base + j
        for k in range(TOP_K):
            _row_copy(y_hbm, pos_ref[k * n_tok + tok], buf_ref.at[k], j, sem).start(priority=k % 2)
        return carry

    lax.fori_loop(0, tm, issue, 0)
    for k in range(TOP_K):
        pltpu.make_async_copy(y_hbm.at[pl.ds(0, tm)], buf_ref.at[k], sem).wait()
    gates = gate_ref[...]
    ffn = gates[:, 0:1] * buf_ref[0]
    for k in range(1, TOP_K):
        ffn = ffn + gates[:, k:k + 1] * buf_ref[k]
    z = DEEPNORM_ALPHA * x_ref[...] + ffn
    o_ref[...] = _layer_norm(z, g_ref[...], b_ref[...])


def _combine(x, y_slots, pos_flat, gates_tk, g, b):
    t = x.shape[0]
    tm = COMBINE_TM
    full = pl.BlockSpec((tm, D_MODEL), lambda i, p: (i, 0))
    vec = pl.BlockSpec((1, D_MODEL), lambda i, p: (0, 0))
    return pl.pallas_call(
        functools.partial(_combine_kernel, n_tok=t, tm=tm),
        out_shape=jax.ShapeDtypeStruct((t, D_MODEL), F32),
        grid_spec=pltpu.PrefetchScalarGridSpec(
            num_scalar_prefetch=1,
            grid=(t // tm,),
            in_specs=[pl.BlockSpec(memory_space=pl.ANY), full,
                      pl.BlockSpec((tm, TOP_K), lambda i, p: (i, 0)), vec, vec],
            out_specs=full,
            scratch_shapes=[pltpu.VMEM((TOP_K, tm, D_MODEL), F32), pltpu.SemaphoreType.DMA],
        ),
        compiler_params=_params(("arbitrary",)),
        name="moe_combine",
    )(pos_flat, y_slots, x, gates_tk, g.reshape(1, -1), b.reshape(1, -1))


def _moe_layer(x, layer, router_w, router_b, w_up, b_up, w_down, b_down, g, b):
    top_i, gates, rank, counts = _router(x, router_w, router_b)
    pos, blk_e, n_used, n_blk, pad_lo, pad_hi = _slot_layout(top_i, rank, counts)
    pos_flat = pos.reshape(-1)
    slots = _dispatch(x, pos_flat, pad_lo, pad_hi, n_used, n_blk)
    y_slots = _experts(slots, blk_e, n_used, n_blk, layer, w_up, b_up[:, :, None, :], w_down,
                       b_down[:, :, None, :])
    return _combine(x, y_slots, pos_flat, gates.T, g, b)


def kernel(x, ab_w_in, ab_w_out, ret_decay, c_w_in, c_w_out, c_rpb, ln_g, ln_b,
           router_w, router_b, exp_w_up, exp_b_up, exp_w_down, exp_b_down):
    b_, t_, d_ = x.shape
    assert b_ == 1 and d_ == D_MODEL
    xs = x.reshape(t_, d_)
    cos_t, sin_t = _rope_tables(t_)
    for layer in range(DEPTH):
        j = layer // 2
        if layer % 2 == 0:
            xs = _even_mixer(xs, ab_w_in[j], ab_w_out[j], ret_decay[j], ln_g[layer, 0], ln_b[layer, 0],
                             cos_t, sin_t)
        else:
            xs = _odd_mixer(xs, c_w_in[j], c_w_out[j], c_rpb[j], ln_g[layer, 0], ln_b[layer, 0])
        xs = _moe_layer(xs, layer, router_w[layer], router_b[layer], exp_w_up, exp_b_up,
                        exp_w_down, exp_b_down, ln_g[layer, 1], ln_b[layer, 1])
    return xs.reshape(b_, t_, d_)
```

```python
import functools

import jax
import jax.numpy as jnp
import numpy as np
from jax import lax
from jax.experimental import pallas as pl
from jax.experimental.pallas import tpu as pltpu

F32 = jnp.float32
BF16 = jnp.bfloat16

D_MODEL = 1024
HEAD_DIM = 64
LANES = 128
HEADS_PER_GROUP = LANES // HEAD_DIM
N_RET_HEADS = 8
N_DIL_HEADS = 8
N_NA_HEADS = 16
RET_W = N_RET_HEADS * HEAD_DIM
DIL_W = N_DIL_HEADS * HEAD_DIM
EVEN_IN = 5 * RET_W + 3 * DIL_W
ODD_IN = 3 * D_MODEL
ROPE_THETA = 10000.0
DIL_CONFIGS = ((128, 1), (512, 4), (2048, 16))
DIL_HALF = 64
GRID_W = 64
NA_KH = 8
NA_KW = 16
N_EXPERTS = 32
TOP_K = 4
D_FF = D_MODEL
SWIGLU_LIMIT = 7.0
SWIGLU_ALPHA = 1.702
LN_EPS = 1e-5
GN_EPS = 1e-6
NEG_INF = -1e30
DEPTH = 4
DEEPNORM_ALPHA = (2.0 * DEPTH) ** 0.25
QK_SCALE = HEAD_DIM ** -0.5

VMEM_LIMIT_BYTES = 56 * 1024 * 1024

PROJ_TM = 512
RET_CHUNK = 256
DIL_TQ = 128
DIL_TK = DIL_TQ + 2 * DIL_HALF
DIL_SUBTILES = 4
NA_ROWS = 2
NA_KROWS = NA_ROWS + NA_KH - 1
NA_SUBBLOCKS = 4
ROUTER_TM = 512
MOE_BM = 512
DISPATCH_TM = 512
COMBINE_TM = 256
ROW_UNROLL = 8


def _params(sem, **kw):
    return pltpu.CompilerParams(dimension_semantics=sem, vmem_limit_bytes=VMEM_LIMIT_BYTES, **kw)


def _nt_dot(a, b):
    return lax.dot_general(a, b, (((1,), (1,)), ((), ())), preferred_element_type=F32)


def _dot(a, b):
    return jnp.dot(a, b, preferred_element_type=F32)


def _layer_norm(z, g, b):
    mu = jnp.mean(z, axis=-1, keepdims=True)
    d = z - mu
    var = jnp.mean(d * d, axis=-1, keepdims=True)
    return d * lax.rsqrt(var + LN_EPS) * g + b


def _first_head_lanes(shape):
    return lax.broadcasted_iota(jnp.int32, shape, len(shape) - 1) < HEAD_DIM


def _rope_tables(t):
    pos = jnp.arange(t, dtype=F32)
    inv = ROPE_THETA ** (-jnp.arange(0, HEAD_DIM, 2, dtype=F32) / HEAD_DIM)
    ang = pos[:, None] * inv[None, :]
    cos, sin = jnp.cos(ang), jnp.sin(ang)
    return (jnp.concatenate([cos, cos, cos, cos], axis=1),
            jnp.concatenate([-sin, sin, -sin, sin], axis=1))


def _rope(h, c, s):
    half = HEAD_DIM // 2
    lane = lax.broadcasted_iota(jnp.int32, h.shape, 1)
    first_half = (lane // half) % 2 == 0
    partner = jnp.where(first_half, pltpu.roll(h, LANES - half, 1), pltpu.roll(h, half, 1))
    return h * c + partner * s


def _proj_even_kernel(x_ref, w_ref, c_ref, s_ref, o_ref, *class_refs):
    tm = x_ref.shape[0]
    xb = x_ref[...].astype(BF16)
    c = c_ref[...]
    s = s_ref[...]
    rope_groups = {0: 1.0, 1: QK_SCALE, 5: QK_SCALE, 6: 1.0}
    for g in range(EVEN_IN // RET_W):
        h = _dot(xb, w_ref[:, g * RET_W:(g + 1) * RET_W])
        for j in range(RET_W // LANES):
            hj = h[:, j * LANES:(j + 1) * LANES]
            if g in rope_groups:
                hj = _rope(hj, c, s) * rope_groups[g]
            col = g * RET_W + j * LANES
            o_ref[:, col:col + LANES] = hj.astype(BF16)
    h_dil = o_ref[:, 5 * RET_W:]
    row = lax.broadcasted_iota(jnp.int32, (tm, tm), 0)
    col = lax.broadcasted_iota(jnp.int32, (tm, tm), 1)
    for cls_ref, (_, dil) in zip(class_refs, DIL_CONFIGS[1:]):
        per_class = tm // dil
        select = (col == dil * (row % per_class) + row // per_class).astype(BF16)
        regrouped = _dot(select, h_dil).astype(BF16)
        for r in range(dil):
            cls_ref[r] = regrouped[r * per_class:(r + 1) * per_class]


def _proj_even(x, w_bf, cos_t, sin_t):
    t = x.shape[0]
    tm = PROJ_TM
    dil_w = 3 * DIL_W
    class_shapes = [jax.ShapeDtypeStruct((dil, t // dil, dil_w), BF16) for _, dil in DIL_CONFIGS[1:]]
    class_specs = [pl.BlockSpec((dil, tm // dil, dil_w), lambda i: (0, i, 0)) for _, dil in DIL_CONFIGS[1:]]
    return pl.pallas_call(
        _proj_even_kernel,
        out_shape=[jax.ShapeDtypeStruct((t, EVEN_IN), BF16)] + class_shapes,
        grid=(t // tm,),
        in_specs=[
            pl.BlockSpec((tm, D_MODEL), lambda i: (i, 0)),
            pl.BlockSpec((D_MODEL, EVEN_IN), lambda i: (0, 0)),
            pl.BlockSpec((tm, LANES), lambda i: (i, 0)),
            pl.BlockSpec((tm, LANES), lambda i: (i, 0)),
        ],
        out_specs=[pl.BlockSpec((tm, EVEN_IN), lambda i: (i, 0))] + class_specs,
        compiler_params=_params(("parallel",)),
        name="proj_even",
    )(x, w_bf, cos_t, sin_t)


def _proj_odd_kernel(x_ref, w_ref, o_ref):
    xb = x_ref[...].astype(BF16)
    for g in range(3):
        h = _dot(xb, w_ref[:, g * D_MODEL:(g + 1) * D_MODEL])
        if g == 0:
            h = h * QK_SCALE
        o_ref[:, g * D_MODEL:(g + 1) * D_MODEL] = h.astype(BF16)


def _proj_odd(x, w_bf):
    t = x.shape[0]
    tm = PROJ_TM
    return pl.pallas_call(
        _proj_odd_kernel,
        out_shape=jax.ShapeDtypeStruct((t, ODD_IN), BF16),
        grid=(t // tm,),
        in_specs=[
            pl.BlockSpec((tm, D_MODEL), lambda i: (i, 0)),
            pl.BlockSpec((D_MODEL, ODD_IN), lambda i: (0, 0)),
        ],
        out_specs=pl.BlockSpec((tm, ODD_IN), lambda i: (i, 0)),
        compiler_params=_params(("parallel",)),
        name="proj_odd",
    )(x, w_bf)


def _ret_kernel(lg_ref, q_ref, k_ref, v_ref, g_ref, *rest, rev):
    if rev:
        o_ref, r_ref, d0_ref, d1_ref, z_ref, xi_ref, dc_ref = rest
        prev_ref = None
    else:
        prev_ref, o_ref, r_ref, d0_ref, d1_ref, z_ref, xi_ref, dc_ref = rest
    C = RET_CHUNK
    groups = RET_W // LANES
    c = pl.program_id(0)
    first = _first_head_lanes((1, LANES))
    direction = 1 if rev else 0

    @pl.when(c == 0)
    def _init():
        row = lax.broadcasted_iota(jnp.int32, (C, C), 0)
        col = lax.broadcasted_iota(jnp.int32, (C, C), 1)
        dist = (col - row) if rev else (row - col)
        keep = (dist > 0) if rev else (dist >= 0)
        distf = jnp.maximum(dist, 0).astype(F32)
        idx = lax.broadcasted_iota(jnp.int32, (C, LANES), 0).astype(F32)
        for p in range(groups):
            lg0 = lg_ref[direction, 2 * p]
            lg1 = lg_ref[direction, 2 * p + 1]
            d0_ref[p] = jnp.where(keep, jnp.exp(lg0 * distf), 0.0)
            d1_ref[p] = jnp.where(keep, jnp.exp(lg1 * distf), 0.0)
            lgl = jnp.where(first, lg0, lg1)
            z_ref[p] = jnp.exp(lgl * (idx if rev else (C - 1.0 - idx)))
            xi_ref[p] = jnp.exp(lgl * ((C - idx) if rev else (idx + 1.0)))
            dc_ref[p] = jnp.exp(lgl * float(C))
        r_ref[...] = jnp.zeros_like(r_ref)

    rr = lax.broadcasted_iota(jnp.int32, (LANES, LANES), 0) < HEAD_DIM
    cc = lax.broadcasted_iota(jnp.int32, (LANES, LANES), 1) < HEAD_DIM
    inv_n = 1.0 / HEAD_DIM
    for p in range(groups):
        lanes = slice(p * LANES, (p + 1) * LANES)
        q2 = q_ref[:, lanes]
        k2 = k_ref[:, lanes]
        v2 = v_ref[:, lanes]
        zero = jnp.zeros_like(q2)
        q0 = jnp.where(first, q2, zero)
        q1 = jnp.where(first, zero, q2)
        a0 = (_nt_dot(q0, k2) * d0_ref[p]).astype(BF16)
        a1 = (_nt_dot(q1, k2) * d1_ref[p]).astype(BF16)
        state = r_ref[p]
        cross = _dot(q2, state.astype(BF16)) * xi_ref[p]
        o = jnp.where(first, _dot(a0, v2), _dot(a1, v2)) + cross

        kz = (k2.astype(F32) * z_ref[p]).T.astype(BF16)
        r_ref[p] = jnp.where(rr == cc, state * dc_ref[p] + _dot(kz, v2), 0.0)

        s_first = jnp.sum(jnp.where(first, o, 0.0), axis=-1, keepdims=True)
        s_second = jnp.sum(jnp.where(first, 0.0, o), axis=-1, keepdims=True)
        d = o - jnp.where(first, s_first, s_second) * inv_n
        dd = d * d
        v_first = jnp.sum(jnp.where(first, dd, 0.0), axis=-1, keepdims=True)
        v_second = jnp.sum(jnp.where(first, 0.0, dd), axis=-1, keepdims=True)
        var = jnp.where(first, v_first, v_second) * inv_n
        gate = g_ref[:, lanes].astype(F32)
        y = gate * jax.nn.sigmoid(gate) * (d * lax.rsqrt(var + GN_EPS))
        if prev_ref is not None:
            y = y + prev_ref[:, lanes].astype(F32)
        o_ref[:, lanes] = y.astype(o_ref.dtype)


def _retention_pass(h, log_gamma, prev, rev):
    t = h.shape[0]
    C = RET_CHUNK
    nc = t // C
    groups = RET_W // LANES

    def col_spec(group):
        return pl.BlockSpec((C, RET_W), lambda c, group=group: ((nc - 1 - c) if rev else c, group))

    in_specs = [
        pl.BlockSpec(memory_space=pltpu.SMEM),
        col_spec(0), col_spec(1), col_spec(2), col_spec(4 if rev else 3),
    ]
    args = [log_gamma, h, h, h, h]
    if not rev:
        in_specs.append(col_spec(0))
        args.append(prev)
    return pl.pallas_call(
        functools.partial(_ret_kernel, rev=rev),
        out_shape=jax.ShapeDtypeStruct((t, RET_W), BF16),
        grid=(nc,),
        in_specs=in_specs,
        out_specs=col_spec(0),
        scratch_shapes=[
            pltpu.VMEM((groups, LANES, LANES), F32),
            pltpu.VMEM((groups, C, C), F32),
            pltpu.VMEM((groups, C, C), F32),
            pltpu.VMEM((groups, C, LANES), F32),
            pltpu.VMEM((groups, C, LANES), F32),
            pltpu.VMEM((groups, 1, LANES), F32),
        ],
        compiler_params=_params(("arbitrary",)),
        name="retention_bwd" if rev else "retention_fwd",
    )(*args)


def _retention(h, decay_logits):
    log_gamma = jax.nn.log_sigmoid(decay_logits.astype(F32))
    y_bwd = _retention_pass(h, log_gamma, None, rev=True)
    return _retention_pass(h, log_gamma, y_bwd, rev=False)


def _dil_kernel(q_ref, k_ref, v_ref, o_ref, lse_ref, *, length):
    first = _first_head_lanes((1, LANES))
    row = lax.broadcasted_iota(jnp.int32, (DIL_TQ, DIL_TK), 0)
    col = lax.broadcasted_iota(jnp.int32, (DIL_TQ, DIL_TK), 1)
    for sub in range(DIL_SUBTILES):
        i = pl.program_id(2) * DIL_SUBTILES + sub
        rows = slice(sub * DIL_TQ, (sub + 1) * DIL_TQ)
        start = jnp.clip(i * DIL_TQ - DIL_HALF, 0, length - DIL_TK)
        start = pl.multiple_of(start, DIL_HALF)
        kw = k_ref[pl.ds(start, DIL_TK), :]
        vw = v_ref[pl.ds(start, DIL_TK), :]
        valid = jnp.abs((start + col) - (i * DIL_TQ + row)) <= DIL_HALF
        q2 = q_ref[rows, :]
        zero = jnp.zeros_like(q2)
        one = jnp.ones_like(vw)

        def one_head(qh, vh):
            s = jnp.where(valid, _nt_dot(qh, kw), NEG_INF)
            m = jnp.max(s, axis=-1, keepdims=True)
            acc = _dot(jnp.exp(s - m).astype(BF16), vh)
            den = pltpu.roll(acc, HEAD_DIM, 1)
            return acc / den, m + jnp.log(den)

        o0, l0 = one_head(jnp.where(first, q2, zero), jnp.where(first, vw, one))
        o1, l1 = one_head(jnp.where(first, zero, q2), jnp.where(first, one, vw))
        o_ref[rows, :] = jnp.where(first, o0, o1).astype(o_ref.dtype)
        lse_ref[rows, :] = jnp.where(first, l0, l1)


def _dilated_branch(h_cls, col_base, t):
    dil, length, _ = h_cls.shape
    groups = DIL_W // LANES
    tq = DIL_TQ * DIL_SUBTILES
    o, lse = pl.pallas_call(
        functools.partial(_dil_kernel, length=length),
        out_shape=(jax.ShapeDtypeStruct((length, dil * DIL_W), BF16),
                   jax.ShapeDtypeStruct((length, dil * DIL_W), F32)),
        grid=(dil, groups, length // tq),
        in_specs=[
            pl.BlockSpec((None, tq, LANES), lambda r, p, i: (r, i, col_base + p)),
            pl.BlockSpec((None, length, LANES), lambda r, p, i: (r, 0, col_base + groups + p)),
            pl.BlockSpec((None, length, LANES), lambda r, p, i: (r, 0, col_base + 2 * groups + p)),
        ],
        out_specs=(
            pl.BlockSpec((tq, LANES), lambda r, p, i: (i, r * groups + p)),
            pl.BlockSpec((tq, LANES), lambda r, p, i: (i, r * groups + p)),
        ),
        compiler_params=_params(("parallel", "parallel", "parallel")),
        name=f"dilated_attn_d{dil}",
    )(h_cls, h_cls, h_cls)
    return o.reshape(t, DIL_W), lse.reshape(t, DIL_W)


def _out_even_kernel(x_ref, yr_ref, o1_ref, o2_ref, o3_ref, l1_ref, l2_ref, l3_ref,
                     w_ref, g_ref, b_ref, out_ref):
    l1, l2, l3 = l1_ref[...], l2_ref[...], l3_ref[...]
    m = jnp.maximum(jnp.maximum(l1, l2), l3)
    e1, e2, e3 = jnp.exp(l1 - m), jnp.exp(l2 - m), jnp.exp(l3 - m)
    den = e1 + e2 + e3
    y_dil = (e1 / den) * o1_ref[...].astype(F32) + (e2 / den) * o2_ref[...].astype(F32) \
        + (e3 / den) * o3_ref[...].astype(F32)
    mix = _dot(yr_ref[...], w_ref[:RET_W, :]) + _dot(y_dil.astype(BF16), w_ref[RET_W:, :])
    z = DEEPNORM_ALPHA * x_ref[...] + mix
    out_ref[...] = _layer_norm(z, g_ref[...], b_ref[...])


def _out_even(x, y_ret, outs, lses, w_bf, g, b):
    t = x.shape[0]
    tm = PROJ_TM
    half = pl.BlockSpec((tm, RET_W), lambda i: (i, 0))
    vec = pl.BlockSpec((1, D_MODEL), lambda i: (0, 0))
    return pl.pallas_call(
        _out_even_kernel,
        out_shape=jax.ShapeDtypeStruct((t, D_MODEL), F32),
        grid=(t // tm,),
        in_specs=[pl.BlockSpec((tm, D_MODEL), lambda i: (i, 0))] + [half] * 7
        + [pl.BlockSpec((D_MODEL, D_MODEL), lambda i: (0, 0)), vec, vec],
        out_specs=pl.BlockSpec((tm, D_MODEL), lambda i: (i, 0)),
        compiler_params=_params(("parallel",)),
        name="out_proj_even",
    )(x, y_ret, *outs, *lses, w_bf, g.reshape(1, -1), b.reshape(1, -1))


def _out_odd_kernel(x_ref, y_ref, w_ref, g_ref, b_ref, out_ref):
    z = DEEPNORM_ALPHA * x_ref[...] + _dot(y_ref[...], w_ref[...])
    out_ref[...] = _layer_norm(z, g_ref[...], b_ref[...])


def _out_odd(x, y, w_bf, g, b):
    t = x.shape[0]
    tm = PROJ_TM
    full = pl.BlockSpec((tm, D_MODEL), lambda i: (i, 0))
    vec = pl.BlockSpec((1, D_MODEL), lambda i: (0, 0))
    return pl.pallas_call(
        _out_odd_kernel,
        out_shape=jax.ShapeDtypeStruct((t, D_MODEL), F32),
        grid=(t // tm,),
        in_specs=[full, full, pl.BlockSpec((D_MODEL, D_MODEL), lambda i: (0, 0)), vec, vec],
        out_specs=full,
        compiler_params=_params(("parallel",)),
        name="out_proj_odd",
    )(x, y, w_bf, g.reshape(1, -1), b.reshape(1, -1))


def _even_mixer(x, w_in, w_out, decay_logits, g, b, cos_t, sin_t):
    t = x.shape[0]
    h, *h_classes = _proj_even(x, w_in.astype(BF16), cos_t, sin_t)
    y_ret = _retention(h, decay_logits)
    outs, lses = [], []
    for h_cls, col_base in [(h.reshape(1, t, EVEN_IN), 5 * RET_W // LANES)] + [(hc, 0) for hc in h_classes]:
        o, lse = _dilated_branch(h_cls, col_base, t)
        outs.append(o)
        lses.append(lse)
    return _out_even(x, y_ret, outs, lses, w_out.astype(BF16), g, b)


def _na_bias_table(rpb, rows):
    n_blk = rows // NA_ROWS
    blocks = np.array([0, 1, 2, n_blk - 2, n_blk - 1])
    kstart = np.clip(NA_ROWS * blocks - NA_KH // 2, 0, rows - NA_KROWS)
    r = NA_ROWS * blocks[:, None] + np.arange(NA_ROWS)[None, :]
    r0 = np.clip(r - NA_KH // 2, 0, rows - NA_KH)
    kr = kstart[:, None] + np.arange(NA_KROWS)[None, :]
    row_ok = (kr[:, None, :] >= r0[:, :, None]) & (kr[:, None, :] < r0[:, :, None] + NA_KH)
    dr = kr[:, None, :] - r[:, :, None] + (NA_KH - 1)
    c = np.arange(GRID_W)
    c0 = np.clip(c - NA_KW // 2, 0, GRID_W - NA_KW)
    kc = np.arange(GRID_W)
    col_ok = (kc[None, :] >= c0[:, None]) & (kc[None, :] < c0[:, None] + NA_KW)
    pad = GRID_W - NA_KW
    rp = jnp.pad(rpb.astype(F32), ((0, 0), (0, 0), (pad, pad)))
    by_col = jnp.stack([rp[:, :, GRID_W - 1 - ci:2 * GRID_W - 1 - ci] for ci in range(GRID_W)], axis=2)
    by_col = jnp.where(col_ok[None, None], by_col, NEG_INF)
    outside = jnp.full((N_NA_HEADS, GRID_W, GRID_W), NEG_INF, F32)
    patterns = []
    for pi in range(5):
        q_rows = []
        for qi in range(NA_ROWS):
            blocks_ = [by_col[:, int(dr[pi, qi, ki])] if row_ok[pi, qi, ki] else outside
                       for ki in range(NA_KROWS)]
            q_rows.append(jnp.concatenate(blocks_, axis=-1))
        patterns.append(jnp.concatenate(q_rows, axis=-2))
    return jnp.stack(patterns, axis=0)


def _na_kernel(q_ref, k_ref, v_ref, bias_ref, o_ref, *, rows):
    n_blk = rows // NA_ROWS
    tq = NA_ROWS * GRID_W
    nk = NA_KROWS * GRID_W
    first = _first_head_lanes((1, LANES))
    for sub in range(NA_SUBBLOCKS):
        rb = pl.program_id(1) * NA_SUBBLOCKS + sub
        qrows = slice(sub * tq, (sub + 1) * tq)
        kstart = jnp.clip(NA_ROWS * rb - NA_KH // 2, 0, rows - NA_KROWS) * GRID_W
        kstart = pl.multiple_of(kstart, GRID_W)
        kw = k_ref[pl.ds(kstart, nk), :]
        vw = v_ref[pl.ds(kstart, nk), :]
        pattern = jnp.clip(rb, 0, 2) + jnp.clip(rb - (n_blk - 3), 0, 2)
        q2 = q_ref[qrows, :]
        zero = jnp.zeros_like(q2)
        one = jnp.ones_like(vw)

        def one_head(qh, vh, bias):
            s = _nt_dot(qh, kw) + bias
            e = jnp.exp(s - jnp.max(s, axis=-1, keepdims=True))
            acc = _dot(e.astype(BF16), vh)
            return acc / pltpu.roll(acc, HEAD_DIM, 1)

        o0 = one_head(jnp.where(first, q2, zero), jnp.where(first, vw, one), bias_ref[pattern, 0])
        o1 = one_head(jnp.where(first, zero, q2), jnp.where(first, one, vw), bias_ref[pattern, 1])
        o_ref[qrows, :] = jnp.where(first, o0, o1).astype(o_ref.dtype)


def _neighbourhood_attention(h, rpb):
    t = h.shape[0]
    rows = t // GRID_W
    n_blk = rows // NA_ROWS
    groups = D_MODEL // LANES
    tq = NA_SUBBLOCKS * NA_ROWS * GRID_W
    bias = _na_bias_table(rpb, rows)
    return pl.pallas_call(
        functools.partial(_na_kernel, rows=rows),
        out_shape=jax.ShapeDtypeStruct((t, D_MODEL), BF16),
        grid=(groups, n_blk // NA_SUBBLOCKS),
        in_specs=[
            pl.BlockSpec((tq, LANES), lambda p, rb: (rb, p)),
            pl.BlockSpec((t, LANES), lambda p, rb: (0, groups + p)),
            pl.BlockSpec((t, LANES), lambda p, rb: (0, 2 * groups + p)),
            pl.BlockSpec((bias.shape[0], HEADS_PER_GROUP) + bias.shape[2:], lambda p, rb: (0, p, 0, 0)),
        ],
        out_specs=pl.BlockSpec((tq, LANES), lambda p, rb: (rb, p)),
        compiler_params=_params(("parallel", "parallel")),
        name="neighbourhood_attn",
    )(h, h, h, bias)


def _odd_mixer(x, w_in, w_out, rpb, g, b):
    h = _proj_odd(x, w_in.astype(BF16))
    y = _neighbourhood_attention(h, rpb)
    return _out_odd(x, y, w_out.astype(BF16), g, b)


def _router_kernel(x_ref, wt_ref, b_ref, idx_ref, gate_ref, rank_ref, count_ref):
    tm = x_ref.shape[0]

    @pl.when(pl.program_id(0) == 0)
    def _init():
        count_ref[...] = jnp.zeros_like(count_ref)

    logits = lax.dot_general(wt_ref[...], x_ref[...], (((1,), (1,)), ((), ())),
                             precision=lax.Precision.HIGHEST, preferred_element_type=F32)
    logits = logits + b_ref[...]
    expert = lax.broadcasted_iota(jnp.int32, logits.shape, 0)
    vals, idxs = [], []
    for _ in range(TOP_K):
        m = jnp.max(logits, axis=0, keepdims=True)
        sel = jnp.min(jnp.where(logits == m, expert, N_EXPERTS), axis=0, keepdims=True)
        vals.append(m)
        idxs.append(sel)
        logits = jnp.where(expert == sel, -jnp.inf, logits)
    exps = [jnp.exp(v - vals[0]) for v in vals]
    den = exps[0] + exps[1] + exps[2] + exps[3]
    for k in range(TOP_K):
        idx_ref[k:k + 1, :] = idxs[k]
        gate_ref[k:k + 1, :] = exps[k] / den

    onehots = [(expert == idxs[k]).astype(F32) for k in range(TOP_K)]
    per_tok = onehots[0] + onehots[1] + onehots[2] + onehots[3]
    earlier = lax.broadcasted_iota(jnp.int32, (tm, tm), 0) < lax.broadcasted_iota(jnp.int32, (tm, tm), 1)
    before = count_ref[...] + _dot(per_tok.astype(BF16), earlier.astype(BF16))
    for k in range(TOP_K):
        rank_ref[k:k + 1, :] = jnp.sum(onehots[k] * before, axis=0, keepdims=True).astype(jnp.int32)
    count_ref[...] += jnp.sum(per_tok, axis=1, keepdims=True)


def _router(x, router_w, router_b):
    t = x.shape[0]
    tm = ROUTER_TM
    row = pl.BlockSpec((TOP_K, tm), lambda i: (0, i))
    return pl.pallas_call(
        _router_kernel,
        out_shape=(jax.ShapeDtypeStruct((TOP_K, t), jnp.int32),
                   jax.ShapeDtypeStruct((TOP_K, t), F32),
                   jax.ShapeDtypeStruct((TOP_K, t), jnp.int32),
                   jax.ShapeDtypeStruct((N_EXPERTS, 1), F32)),
        grid=(t // tm,),
        in_specs=[
            pl.BlockSpec((tm, D_MODEL), lambda i: (i, 0)),
            pl.BlockSpec((N_EXPERTS, D_MODEL), lambda i: (0, 0)),
            pl.BlockSpec((N_EXPERTS, 1), lambda i: (0, 0)),
        ],
        out_specs=(row, row, row, pl.BlockSpec((N_EXPERTS, 1), lambda i: (0, 0))),
        compiler_params=_params(("arbitrary",)),
        name="moe_router",
    )(x, router_w.T, router_b.reshape(-1, 1))


def _slot_layout(top_i, rank, counts):
    t = top_i.shape[1]
    bm = MOE_BM
    experts = jnp.arange(N_EXPERTS, dtype=jnp.int32)
    counts = counts.reshape(-1).astype(jnp.int32)
    pcounts = (counts + bm - 1) // bm * bm
    pend = jnp.cumsum(pcounts)
    pstart = pend - pcounts
    pos = rank + jnp.sum(jnp.where(top_i[:, :, None] == experts, pstart, 0), axis=-1)
    n_blk = (TOP_K * t) // bm + N_EXPERTS
    blk_start = jnp.arange(n_blk, dtype=jnp.int32) * bm
    blk_e = jnp.minimum(jnp.sum((pend[None, :] <= blk_start[:, None]).astype(jnp.int32), axis=1),
                        N_EXPERTS - 1)
    n_used = (pend[-1] // bm).astype(jnp.int32).reshape(1)
    return pos.astype(jnp.int32), blk_e, n_used, n_blk, pstart + counts, pend


def _row_copy(src, src_row, dst, dst_row, sem):
    return pltpu.make_async_copy(src.at[pl.ds(src_row, 1)], dst.at[pl.ds(dst_row, 1)], sem)


def _dispatch_kernel(pos_ref, pad_lo_ref, pad_hi_ref, n_used_ref, x_ref, slots_hbm, zeros_ref, sem, fill_sem,
                     *, n_tok, tm, n_blk):
    bm = MOE_BM
    base = pl.program_id(0) * tm

    @pl.when(pl.program_id(0) == 0)
    def _fill_padding():
        zeros_ref[...] = jnp.zeros_like(zeros_ref)

        def fill_expert_tail(e, act):
            lo = pad_lo_ref[e]
            hi = pad_hi_ref[e]
            aligned = jnp.minimum((lo + 7) // 8 * 8, hi)
            lax.fori_loop(lo, aligned, lambda r, c: (act(_row_copy(zeros_ref, 0, slots_hbm, r, fill_sem)), c)[1], 0)
            remaining = hi - aligned
            start = aligned
            size = bm // 2
            while size >= 8:
                row0 = pl.multiple_of(start, 8)

                @pl.when((remaining & size) != 0)
                def _(row0=row0, size=size):
                    act(pltpu.make_async_copy(zeros_ref.at[pl.ds(0, size)],
                                              slots_hbm.at[pl.ds(row0, size)], fill_sem))

                start = start + (remaining & size)
                size //= 2

        def per_expert(e, carry):
            fill_expert_tail(e, lambda copy: copy.start())
            fill_expert_tail(e, lambda copy: copy.wait())
            return carry

        lax.fori_loop(0, N_EXPERTS, per_expert, 0)

        def unused_block(blk):
            return pltpu.make_async_copy(zeros_ref, slots_hbm.at[pl.ds(blk * bm, bm)], fill_sem)

        lax.fori_loop(n_used_ref[0], n_blk, lambda blk, c: (unused_block(blk).start(), c)[1], 0)
        lax.fori_loop(n_used_ref[0], n_blk, lambda blk, c: (unused_block(blk).wait(), c)[1], 0)

    def issue(j, carry):
        for k in range(TOP_K):
            _row_copy(x_ref, j, slots_hbm, pos_ref[k * n_tok + base + j], sem).start(priority=k % 2)
        return carry

    lax.fori_loop(0, tm, issue, 0)
    for k in range(TOP_K):
        pltpu.make_async_copy(x_ref, slots_hbm.at[pl.ds(0, tm)], sem).wait()


def _dispatch(x, pos_flat, pad_lo, pad_hi, n_used, n_blk):
    t = x.shape[0]
    tm = DISPATCH_TM
    return pl.pallas_call(
        functools.partial(_dispatch_kernel, n_tok=t, tm=tm, n_blk=n_blk),
        out_shape=jax.ShapeDtypeStruct((n_blk * MOE_BM, D_MODEL), F32),
        grid_spec=pltpu.PrefetchScalarGridSpec(
            num_scalar_prefetch=4,
            grid=(t // tm,),
            in_specs=[pl.BlockSpec((tm, D_MODEL), lambda i, *_: (i, 0))],
            out_specs=pl.BlockSpec(memory_space=pl.ANY),
            scratch_shapes=[pltpu.VMEM((MOE_BM, D_MODEL), F32),
                            pltpu.SemaphoreType.DMA, pltpu.SemaphoreType.DMA],
        ),
        compiler_params=_params(("arbitrary",), has_side_effects=True),
        name="moe_dispatch",
    )(pos_flat, pad_lo, pad_hi, n_used, x)


def _expert_kernel(blk_e_ref, n_used_ref, x_ref, wu_ref, bu_ref, wd_ref, bd_ref, o_ref, wu_bf_ref, wd_bf_ref):
    b = pl.program_id(0)
    used = b < n_used_ref[0]
    half = LANES // 2
    n_groups = D_FF // LANES
    new_expert = jnp.logical_or(b == 0, blk_e_ref[b] != blk_e_ref[jnp.maximum(b - 1, 0)])

    @pl.when(jnp.logical_and(used, new_expert))
    def _stage_weights():
        wu_bf_ref[...] = wu_ref[...].astype(BF16)
        row = lax.broadcasted_iota(jnp.int32, (LANES, LANES), 0)
        col = lax.broadcasted_iota(jnp.int32, (LANES, LANES), 1)
        select = (col == row // 2 + half * (row % 2)).astype(BF16)
        for g in range(n_groups):
            rows = slice(g * LANES, (g + 1) * LANES)
            wd_bf_ref[rows, :] = _dot(select, wd_ref[rows, :].astype(BF16)).astype(BF16)

    @pl.when(used)
    def _compute():
        xb = x_ref[...].astype(BF16)
        h = _dot(xb, wu_bf_ref[...]) + bu_ref[...]
        even = lax.broadcasted_iota(jnp.int32, (1, LANES), 1) % 2 == 0
        packed = []
        for g in range(n_groups):
            lo = h[:, 2 * g * LANES:(2 * g + 1) * LANES]
            hi = h[:, (2 * g + 1) * LANES:(2 * g + 2) * LANES]
            x_glu = jnp.minimum(jnp.where(even, lo, pltpu.roll(hi, 1, 1)), SWIGLU_LIMIT)
            x_lin = jnp.clip(jnp.where(even, pltpu.roll(lo, LANES - 1, 1), hi), -SWIGLU_LIMIT, SWIGLU_LIMIT)
            packed.append((x_glu * jax.nn.sigmoid(SWIGLU_ALPHA * x_glu) * (x_lin + 1.0)).astype(BF16))
        act = jnp.concatenate(packed, axis=1)
        o_ref[...] = _dot(act, wd_bf_ref[...]) + bd_ref[...]

    @pl.when(jnp.logical_not(used))
    def _unused():
        o_ref[...] = jnp.zeros_like(o_ref)


def _experts(slots, blk_e, n_used, n_blk, layer, w_up, b_up, w_down, b_down):
    bm = MOE_BM

    def wspec(shape):
        return pl.BlockSpec((None, None) + shape, lambda b, e, n: (layer, e[b], 0, 0))

    return pl.pallas_call(
        _expert_kernel,
        out_shape=jax.ShapeDtypeStruct(slots.shape, F32),
        grid_spec=pltpu.PrefetchScalarGridSpec(
            num_scalar_prefetch=2,
            grid=(n_blk,),
            in_specs=[
                pl.BlockSpec((bm, D_MODEL), lambda b, e, n: (jnp.maximum(jnp.minimum(b, n[0] - 1), 0), 0)),
                wspec((D_MODEL, 2 * D_FF)), wspec((1, 2 * D_FF)),
                wspec((D_FF, D_MODEL)), wspec((1, D_MODEL)),
            ],
            out_specs=pl.BlockSpec((bm, D_MODEL), lambda b, e, n: (b, 0)),
            scratch_shapes=[pltpu.VMEM((D_MODEL, 2 * D_FF), BF16), pltpu.VMEM((D_FF, D_MODEL), BF16)],
        ),
        compiler_params=_params(("arbitrary",)),
        name="moe_experts",
    )(blk_e, n_used, slots, w_up, b_up, w_down, b_down)


def _combine_kernel(pos_ref, y_hbm, x_ref, gate_ref, g_ref, b_ref, o_ref, buf_ref, sem, *, n_tok, tm):
    base = pl.program_id(0) * tm

    def issue(group, carry):
        j0 = pl.multiple_of(group * ROW_UNROLL, ROW_UNROLL)
        for u in range(ROW_UNROLL):
            for k in range(TOP_K):
                _row_copy(y_hbm, pos_ref[k * n_tok + base + j0 + u], buf_ref.at[k], j0 + u, sem).start(priority=k % 2)
        return carry

    lax.fori_loop(0, tm // ROW_UNROLL, issue, 0)
    for k in range(TOP_K):
        pltpu.make_async_copy(y_hbm.at[pl.ds(0, tm)], buf_ref.at[k], sem).wait()
    gates = gate_ref[...]
    ffn = gates[:, 0:1] * buf_ref[0]
    for k in range(1, TOP_K):
        ffn = ffn + gates[:, k:k + 1] * buf_ref[k]
    z = DEEPNORM_ALPHA * x_ref[...] + ffn
    o_ref[...] = _layer_norm(z, g_ref[...], b_ref[...])


def _combine(x, y_slots, pos_flat, gates_tk, g, b):
    t = x.shape[0]
    tm = COMBINE_TM
    full = pl.BlockSpec((tm, D_MODEL), lambda i, p: (i, 0))
    vec = pl.BlockSpec((1, D_MODEL), lambda i, p: (0, 0))
    return pl.pallas_call(
        functools.partial(_combine_kernel, n_tok=t, tm=tm),
        out_shape=jax.ShapeDtypeStruct((t, D_MODEL), F32),
        grid_spec=pltpu.PrefetchScalarGridSpec(
            num_scalar_prefetch=1,
            grid=(t // tm,),
            in_specs=[pl.BlockSpec(memory_space=pl.ANY), full,
                      pl.BlockSpec((tm, TOP_K), lambda i, p: (i, 0)), vec, vec],
            out_specs=full,
            scratch_shapes=[pltpu.VMEM((TOP_K, tm, D_MODEL), F32), pltpu.SemaphoreType.DMA],
        ),
        compiler_params=_params(("arbitrary",)),
        name="moe_combine",
    )(pos_flat, y_slots, x, gates_tk, g.reshape(1, -1), b.reshape(1, -1))


def _moe_layer(x, layer, router_w, router_b, w_up, b_up, w_down, b_down, g, b):
    top_i, gates, rank, counts = _router(x, router_w, router_b)
    pos, blk_e, n_used, n_blk, pad_lo, pad_hi = _slot_layout(top_i, rank, counts)
    pos_flat = pos.reshape(-1)
    slots = _dispatch(x, pos_flat, pad_lo, pad_hi, n_used, n_blk)
    y_slots = _experts(slots, blk_e, n_used, n_blk, layer, w_up, b_up[:, :, None, :], w_down,
                       b_down[:, :, None, :])
    return _combine(x, y_slots, pos_flat, gates.T, g, b)


def kernel(x, ab_w_in, ab_w_out, ret_decay, c_w_in, c_w_out, c_rpb, ln_g, ln_b,
           router_w, router_b, exp_w_up, exp_b_up, exp_w_down, exp_b_down):
    b_, t_, d_ = x.shape
    assert b_ == 1 and d_ == D_MODEL
    xs = x.reshape(t_, d_)
    cos_t, sin_t = _rope_tables(t_)
    for layer in range(DEPTH):
        j = layer // 2
        if layer % 2 == 0:
            xs = _even_mixer(xs, ab_w_in[j], ab_w_out[j], ret_decay[j], ln_g[layer, 0], ln_b[layer, 0],
                             cos_t, sin_t)
        else:
            xs = _odd_mixer(xs, c_w_in[j], c_w_out[j], c_rpb[j], ln_g[layer, 0], ln_b[layer, 0])
        xs = _moe_layer(xs, layer, router_w[layer], router_b[layer], exp_w_up, exp_b_up,
                        exp_w_down, exp_b_down, ln_g[layer, 1], ln_b[layer, 1])
    return xs.reshape(b_, t_, d_)
```

```python
import functools

import jax
import jax.numpy as jnp
import numpy as np
from jax import lax
from jax.experimental import pallas as pl
from jax.experimental.pallas import tpu as pltpu

F32 = jnp.float32
BF16 = jnp.bfloat16

D_MODEL = 1024
HEAD_DIM = 64
LANES = 128
HEADS_PER_GROUP = LANES // HEAD_DIM
N_RET_HEADS = 8
N_DIL_HEADS = 8
N_NA_HEADS = 16
RET_W = N_RET_HEADS * HEAD_DIM
DIL_W = N_DIL_HEADS * HEAD_DIM
EVEN_IN = 5 * RET_W + 3 * DIL_W
ODD_IN = 3 * D_MODEL
ROPE_THETA = 10000.0
DIL_CONFIGS = ((128, 1), (512, 4), (2048, 16))
DIL_HALF = 64
GRID_W = 64
NA_KH = 8
NA_KW = 16
N_EXPERTS = 32
TOP_K = 4
D_FF = D_MODEL
SWIGLU_LIMIT = 7.0
SWIGLU_ALPHA = 1.702
LN_EPS = 1e-5
GN_EPS = 1e-6
NEG_INF = -1e30
DEPTH = 4
DEEPNORM_ALPHA = (2.0 * DEPTH) ** 0.25
QK_SCALE = HEAD_DIM ** -0.5

VMEM_LIMIT_BYTES = 56 * 1024 * 1024

PROJ_TM = 512
RET_CHUNK = 256
DIL_TQ = 128
DIL_TK = DIL_TQ + 2 * DIL_HALF
DIL_SUBTILES = 4
NA_ROWS = 2
NA_KROWS = NA_ROWS + NA_KH - 1
NA_SUBBLOCKS = 4
ROUTER_TM = 512
MOE_BM = 512
COMBINE_TM = 256
ROW_UNROLL = 8


def _params(sem, **kw):
    return pltpu.CompilerParams(dimension_semantics=sem, vmem_limit_bytes=VMEM_LIMIT_BYTES, **kw)


def _nt_dot(a, b):
    return lax.dot_general(a, b, (((1,), (1,)), ((), ())), preferred_element_type=F32)


def _dot(a, b):
    return jnp.dot(a, b, preferred_element_type=F32)


def _layer_norm(z, g, b):
    mu = jnp.mean(z, axis=-1, keepdims=True)
    d = z - mu
    var = jnp.mean(d * d, axis=-1, keepdims=True)
    return d * lax.rsqrt(var + LN_EPS) * g + b


def _first_head_lanes(shape):
    return lax.broadcasted_iota(jnp.int32, shape, len(shape) - 1) < HEAD_DIM


def _rope_tables(t):
    pos = jnp.arange(t, dtype=F32)
    inv = ROPE_THETA ** (-jnp.arange(0, HEAD_DIM, 2, dtype=F32) / HEAD_DIM)
    ang = pos[:, None] * inv[None, :]
    cos, sin = jnp.cos(ang), jnp.sin(ang)
    return (jnp.concatenate([cos, cos, cos, cos], axis=1),
            jnp.concatenate([-sin, sin, -sin, sin], axis=1))


def _rope(h, c, s):
    half = HEAD_DIM // 2
    lane = lax.broadcasted_iota(jnp.int32, h.shape, 1)
    first_half = (lane // half) % 2 == 0
    partner = jnp.where(first_half, pltpu.roll(h, LANES - half, 1), pltpu.roll(h, half, 1))
    return h * c + partner * s


def _proj_even_kernel(x_ref, w_ref, c_ref, s_ref, o_ref, *class_refs):
    tm = x_ref.shape[0]
    xb = x_ref[...].astype(BF16)
    c = c_ref[...]
    s = s_ref[...]
    rope_groups = {0: 1.0, 1: QK_SCALE, 5: QK_SCALE, 6: 1.0}
    for g in range(EVEN_IN // RET_W):
        h = _dot(xb, w_ref[:, g * RET_W:(g + 1) * RET_W])
        for j in range(RET_W // LANES):
            hj = h[:, j * LANES:(j + 1) * LANES]
            if g in rope_groups:
                hj = _rope(hj, c, s) * rope_groups[g]
            col = g * RET_W + j * LANES
            o_ref[:, col:col + LANES] = hj.astype(BF16)
    h_dil = o_ref[:, 5 * RET_W:]
    row = lax.broadcasted_iota(jnp.int32, (tm, tm), 0)
    col = lax.broadcasted_iota(jnp.int32, (tm, tm), 1)
    for cls_ref, (_, dil) in zip(class_refs, DIL_CONFIGS[1:]):
        per_class = tm // dil
        select = (col == dil * (row % per_class) + row // per_class).astype(BF16)
        regrouped = _dot(select, h_dil).astype(BF16)
        for r in range(dil):
            cls_ref[r] = regrouped[r * per_class:(r + 1) * per_class]


def _proj_even(x, w_bf, cos_t, sin_t):
    t = x.shape[0]
    tm = PROJ_TM
    dil_w = 3 * DIL_W
    class_shapes = [jax.ShapeDtypeStruct((dil, t // dil, dil_w), BF16) for _, dil in DIL_CONFIGS[1:]]
    class_specs = [pl.BlockSpec((dil, tm // dil, dil_w), lambda i: (0, i, 0)) for _, dil in DIL_CONFIGS[1:]]
    return pl.pallas_call(
        _proj_even_kernel,
        out_shape=[jax.ShapeDtypeStruct((t, EVEN_IN), BF16)] + class_shapes,
        grid=(t // tm,),
        in_specs=[
            pl.BlockSpec((tm, D_MODEL), lambda i: (i, 0)),
            pl.BlockSpec((D_MODEL, EVEN_IN), lambda i: (0, 0)),
            pl.BlockSpec((tm, LANES), lambda i: (i, 0)),
            pl.BlockSpec((tm, LANES), lambda i: (i, 0)),
        ],
        out_specs=[pl.BlockSpec((tm, EVEN_IN), lambda i: (i, 0))] + class_specs,
        compiler_params=_params(("parallel",)),
        name="proj_even",
    )(x, w_bf, cos_t, sin_t)


def _proj_odd_kernel(x_ref, w_ref, o_ref):
    xb = x_ref[...].astype(BF16)
    for g in range(3):
        h = _dot(xb, w_ref[:, g * D_MODEL:(g + 1) * D_MODEL])
        if g == 0:
            h = h * QK_SCALE
        o_ref[:, g * D_MODEL:(g + 1) * D_MODEL] = h.astype(BF16)


def _proj_odd(x, w_bf):
    t = x.shape[0]
    tm = PROJ_TM
    return pl.pallas_call(
        _proj_odd_kernel,
        out_shape=jax.ShapeDtypeStruct((t, ODD_IN), BF16),
        grid=(t // tm,),
        in_specs=[
            pl.BlockSpec((tm, D_MODEL), lambda i: (i, 0)),
            pl.BlockSpec((D_MODEL, ODD_IN), lambda i: (0, 0)),
        ],
        out_specs=pl.BlockSpec((tm, ODD_IN), lambda i: (i, 0)),
        compiler_params=_params(("parallel",)),
        name="proj_odd",
    )(x, w_bf)


def _ret_kernel(lg_ref, q_ref, k_ref, v_ref, g_ref, *rest, rev):
    if rev:
        o_ref, r_ref, d0_ref, d1_ref, z_ref, xi_ref, dc_ref = rest
        prev_ref = None
    else:
        prev_ref, o_ref, r_ref, d0_ref, d1_ref, z_ref, xi_ref, dc_ref = rest
    C = RET_CHUNK
    groups = RET_W // LANES
    c = pl.program_id(0)
    first = _first_head_lanes((1, LANES))
    direction = 1 if rev else 0

    @pl.when(c == 0)
    def _init():
        row = lax.broadcasted_iota(jnp.int32, (C, C), 0)
        col = lax.broadcasted_iota(jnp.int32, (C, C), 1)
        dist = (col - row) if rev else (row - col)
        keep = (dist > 0) if rev else (dist >= 0)
        distf = jnp.maximum(dist, 0).astype(F32)
        idx = lax.broadcasted_iota(jnp.int32, (C, LANES), 0).astype(F32)
        for p in range(groups):
            lg0 = lg_ref[direction, 2 * p]
            lg1 = lg_ref[direction, 2 * p + 1]
            d0_ref[p] = jnp.where(keep, jnp.exp(lg0 * distf), 0.0)
            d1_ref[p] = jnp.where(keep, jnp.exp(lg1 * distf), 0.0)
            lgl = jnp.where(first, lg0, lg1)
            z_ref[p] = jnp.exp(lgl * (idx if rev else (C - 1.0 - idx)))
            xi_ref[p] = jnp.exp(lgl * ((C - idx) if rev else (idx + 1.0)))
            dc_ref[p] = jnp.exp(lgl * float(C))
        r_ref[...] = jnp.zeros_like(r_ref)

    rr = lax.broadcasted_iota(jnp.int32, (LANES, LANES), 0) < HEAD_DIM
    cc = lax.broadcasted_iota(jnp.int32, (LANES, LANES), 1) < HEAD_DIM
    inv_n = 1.0 / HEAD_DIM
    for p in range(groups):
        lanes = slice(p * LANES, (p + 1) * LANES)
        q2 = q_ref[:, lanes]
        k2 = k_ref[:, lanes]
        v2 = v_ref[:, lanes]
        zero = jnp.zeros_like(q2)
        q0 = jnp.where(first, q2, zero)
        q1 = jnp.where(first, zero, q2)
        a0 = (_nt_dot(q0, k2) * d0_ref[p]).astype(BF16)
        a1 = (_nt_dot(q1, k2) * d1_ref[p]).astype(BF16)
        state = r_ref[p]
        cross = _dot(q2, state.astype(BF16)) * xi_ref[p]
        o = jnp.where(first, _dot(a0, v2), _dot(a1, v2)) + cross

        kz = (k2.astype(F32) * z_ref[p]).T.astype(BF16)
        r_ref[p] = jnp.where(rr == cc, state * dc_ref[p] + _dot(kz, v2), 0.0)

        s_first = jnp.sum(jnp.where(first, o, 0.0), axis=-1, keepdims=True)
        s_second = jnp.sum(jnp.where(first, 0.0, o), axis=-1, keepdims=True)
        d = o - jnp.where(first, s_first, s_second) * inv_n
        dd = d * d
        v_first = jnp.sum(jnp.where(first, dd, 0.0), axis=-1, keepdims=True)
        v_second = jnp.sum(jnp.where(first, 0.0, dd), axis=-1, keepdims=True)
        var = jnp.where(first, v_first, v_second) * inv_n
        gate = g_ref[:, lanes].astype(F32)
        y = gate * jax.nn.sigmoid(gate) * (d * lax.rsqrt(var + GN_EPS))
        if prev_ref is not None:
            y = y + prev_ref[:, lanes].astype(F32)
        o_ref[:, lanes] = y.astype(o_ref.dtype)


def _retention_pass(h, log_gamma, prev, rev):
    t = h.shape[0]
    C = RET_CHUNK
    nc = t // C
    groups = RET_W // LANES

    def col_spec(group):
        return pl.BlockSpec((C, RET_W), lambda c, group=group: ((nc - 1 - c) if rev else c, group))

    in_specs = [
        pl.BlockSpec(memory_space=pltpu.SMEM),
        col_spec(0), col_spec(1), col_spec(2), col_spec(4 if rev else 3),
    ]
    args = [log_gamma, h, h, h, h]
    if not rev:
        in_specs.append(col_spec(0))
        args.append(prev)
    return pl.pallas_call(
        functools.partial(_ret_kernel, rev=rev),
        out_shape=jax.ShapeDtypeStruct((t, RET_W), BF16),
        grid=(nc,),
        in_specs=in_specs,
        out_specs=col_spec(0),
        scratch_shapes=[
            pltpu.VMEM((groups, LANES, LANES), F32),
            pltpu.VMEM((groups, C, C), F32),
            pltpu.VMEM((groups, C, C), F32),
            pltpu.VMEM((groups, C, LANES), F32),
            pltpu.VMEM((groups, C, LANES), F32),
            pltpu.VMEM((groups, 1, LANES), F32),
        ],
        compiler_params=_params(("arbitrary",)),
        name="retention_bwd" if rev else "retention_fwd",
    )(*args)


def _retention(h, decay_logits):
    log_gamma = jax.nn.log_sigmoid(decay_logits.astype(F32))
    y_bwd = _retention_pass(h, log_gamma, None, rev=True)
    return _retention_pass(h, log_gamma, y_bwd, rev=False)


def _dil_kernel(q_ref, k_ref, v_ref, o_ref, lse_ref, *, length):
    first = _first_head_lanes((1, LANES))
    row = lax.broadcasted_iota(jnp.int32, (DIL_TQ, DIL_TK), 0)
    col = lax.broadcasted_iota(jnp.int32, (DIL_TQ, DIL_TK), 1)
    for sub in range(DIL_SUBTILES):
        i = pl.program_id(2) * DIL_SUBTILES + sub
        rows = slice(sub * DIL_TQ, (sub + 1) * DIL_TQ)
        start = jnp.clip(i * DIL_TQ - DIL_HALF, 0, length - DIL_TK)
        start = pl.multiple_of(start, DIL_HALF)
        kw = k_ref[pl.ds(start, DIL_TK), :]
        vw = v_ref[pl.ds(start, DIL_TK), :]
        valid = jnp.abs((start + col) - (i * DIL_TQ + row)) <= DIL_HALF
        q2 = q_ref[rows, :]
        zero = jnp.zeros_like(q2)

        def one_head(qh):
            s = jnp.where(valid, _nt_dot(qh, kw), NEG_INF)
            m = jnp.max(s, axis=-1, keepdims=True)
            e = jnp.exp(s - m)
            den = jnp.sum(e, axis=-1, keepdims=True)
            return _dot(e.astype(BF16), vw) / den, m + jnp.log(den)

        o0, l0 = one_head(jnp.where(first, q2, zero))
        o1, l1 = one_head(jnp.where(first, zero, q2))
        o_ref[rows, :] = jnp.where(first, o0, o1).astype(o_ref.dtype)
        lse_ref[rows, :] = jnp.where(first, l0, l1)


def _dilated_branch(h_cls, col_base, t):
    dil, length, _ = h_cls.shape
    groups = DIL_W // LANES
    tq = DIL_TQ * DIL_SUBTILES
    o, lse = pl.pallas_call(
        functools.partial(_dil_kernel, length=length),
        out_shape=(jax.ShapeDtypeStruct((length, dil * DIL_W), BF16),
                   jax.ShapeDtypeStruct((length, dil * DIL_W), F32)),
        grid=(dil, groups, length // tq),
        in_specs=[
            pl.BlockSpec((None, tq, LANES), lambda r, p, i: (r, i, col_base + p)),
            pl.BlockSpec((None, length, LANES), lambda r, p, i: (r, 0, col_base + groups + p)),
            pl.BlockSpec((None, length, LANES), lambda r, p, i: (r, 0, col_base + 2 * groups + p)),
        ],
        out_specs=(
            pl.BlockSpec((tq, LANES), lambda r, p, i: (i, r * groups + p)),
            pl.BlockSpec((tq, LANES), lambda r, p, i: (i, r * groups + p)),
        ),
        compiler_params=_params(("parallel", "parallel", "parallel")),
        name=f"dilated_attn_d{dil}",
    )(h_cls, h_cls, h_cls)
    return o.reshape(t, DIL_W), lse.reshape(t, DIL_W)


def _out_even_kernel(x_ref, yr_ref, o1_ref, o2_ref, o3_ref, l1_ref, l2_ref, l3_ref,
                     w_ref, g_ref, b_ref, out_ref):
    l1, l2, l3 = l1_ref[...], l2_ref[...], l3_ref[...]
    m = jnp.maximum(jnp.maximum(l1, l2), l3)
    e1, e2, e3 = jnp.exp(l1 - m), jnp.exp(l2 - m), jnp.exp(l3 - m)
    den = e1 + e2 + e3
    y_dil = (e1 / den) * o1_ref[...].astype(F32) + (e2 / den) * o2_ref[...].astype(F32) \
        + (e3 / den) * o3_ref[...].astype(F32)
    mix = _dot(yr_ref[...], w_ref[:RET_W, :]) + _dot(y_dil.astype(BF16), w_ref[RET_W:, :])
    z = DEEPNORM_ALPHA * x_ref[...] + mix
    out_ref[...] = _layer_norm(z, g_ref[...], b_ref[...])


def _out_even(x, y_ret, outs, lses, w_bf, g, b):
    t = x.shape[0]
    tm = PROJ_TM
    half = pl.BlockSpec((tm, RET_W), lambda i: (i, 0))
    vec = pl.BlockSpec((1, D_MODEL), lambda i: (0, 0))
    return pl.pallas_call(
        _out_even_kernel,
        out_shape=jax.ShapeDtypeStruct((t, D_MODEL), F32),
        grid=(t // tm,),
        in_specs=[pl.BlockSpec((tm, D_MODEL), lambda i: (i, 0))] + [half] * 7
        + [pl.BlockSpec((D_MODEL, D_MODEL), lambda i: (0, 0)), vec, vec],
        out_specs=pl.BlockSpec((tm, D_MODEL), lambda i: (i, 0)),
        compiler_params=_params(("parallel",)),
        name="out_proj_even",
    )(x, y_ret, *outs, *lses, w_bf, g.reshape(1, -1), b.reshape(1, -1))


def _out_odd_kernel(x_ref, y_ref, w_ref, g_ref, b_ref, out_ref):
    z = DEEPNORM_ALPHA * x_ref[...] + _dot(y_ref[...], w_ref[...])
    out_ref[...] = _layer_norm(z, g_ref[...], b_ref[...])


def _out_odd(x, y, w_bf, g, b):
    t = x.shape[0]
    tm = PROJ_TM
    full = pl.BlockSpec((tm, D_MODEL), lambda i: (i, 0))
    vec = pl.BlockSpec((1, D_MODEL), lambda i: (0, 0))
    return pl.pallas_call(
        _out_odd_kernel,
        out_shape=jax.ShapeDtypeStruct((t, D_MODEL), F32),
        grid=(t // tm,),
        in_specs=[full, full, pl.BlockSpec((D_MODEL, D_MODEL), lambda i: (0, 0)), vec, vec],
        out_specs=full,
        compiler_params=_params(("parallel",)),
        name="out_proj_odd",
    )(x, y, w_bf, g.reshape(1, -1), b.reshape(1, -1))


def _even_mixer(x, w_in, w_out, decay_logits, g, b, cos_t, sin_t):
    t = x.shape[0]
    h, *h_classes = _proj_even(x, w_in.astype(BF16), cos_t, sin_t)
    y_ret = _retention(h, decay_logits)
    outs, lses = [], []
    for h_cls, col_base in [(h.reshape(1, t, EVEN_IN), 5 * RET_W // LANES)] + [(hc, 0) for hc in h_classes]:
        o, lse = _dilated_branch(h_cls, col_base, t)
        outs.append(o)
        lses.append(lse)
    return _out_even(x, y_ret, outs, lses, w_out.astype(BF16), g, b)


def _na_bias_table(rpb, rows):
    n_blk = rows // NA_ROWS
    blocks = np.array([0, 1, 2, n_blk - 2, n_blk - 1])
    kstart = np.clip(NA_ROWS * blocks - NA_KH // 2, 0, rows - NA_KROWS)
    r = NA_ROWS * blocks[:, None] + np.arange(NA_ROWS)[None, :]
    r0 = np.clip(r - NA_KH // 2, 0, rows - NA_KH)
    kr = kstart[:, None] + np.arange(NA_KROWS)[None, :]
    row_ok = (kr[:, None, :] >= r0[:, :, None]) & (kr[:, None, :] < r0[:, :, None] + NA_KH)
    dr = kr[:, None, :] - r[:, :, None] + (NA_KH - 1)
    c = np.arange(GRID_W)
    c0 = np.clip(c - NA_KW // 2, 0, GRID_W - NA_KW)
    kc = np.arange(GRID_W)
    col_ok = (kc[None, :] >= c0[:, None]) & (kc[None, :] < c0[:, None] + NA_KW)
    pad = GRID_W - NA_KW
    rp = jnp.pad(rpb.astype(F32), ((0, 0), (0, 0), (pad, pad)))
    by_col = jnp.stack([rp[:, :, GRID_W - 1 - ci:2 * GRID_W - 1 - ci] for ci in range(GRID_W)], axis=2)
    by_col = jnp.where(col_ok[None, None], by_col, NEG_INF)
    outside = jnp.full((N_NA_HEADS, GRID_W, GRID_W), NEG_INF, F32)
    patterns = []
    for pi in range(5):
        q_rows = []
        for qi in range(NA_ROWS):
            blocks_ = [by_col[:, int(dr[pi, qi, ki])] if row_ok[pi, qi, ki] else outside
                       for ki in range(NA_KROWS)]
            q_rows.append(jnp.concatenate(blocks_, axis=-1))
        patterns.append(jnp.concatenate(q_rows, axis=-2))
    return jnp.stack(patterns, axis=0)


def _na_kernel(q_ref, k_ref, v_ref, bias_ref, o_ref, *, rows):
    n_blk = rows // NA_ROWS
    tq = NA_ROWS * GRID_W
    nk = NA_KROWS * GRID_W
    first = _first_head_lanes((1, LANES))
    for sub in range(NA_SUBBLOCKS):
        rb = pl.program_id(1) * NA_SUBBLOCKS + sub
        qrows = slice(sub * tq, (sub + 1) * tq)
        kstart = jnp.clip(NA_ROWS * rb - NA_KH // 2, 0, rows - NA_KROWS) * GRID_W
        kstart = pl.multiple_of(kstart, GRID_W)
        kw = k_ref[pl.ds(kstart, nk), :]
        vw = v_ref[pl.ds(kstart, nk), :]
        pattern = jnp.clip(rb, 0, 2) + jnp.clip(rb - (n_blk - 3), 0, 2)
        q2 = q_ref[qrows, :]
        zero = jnp.zeros_like(q2)

        def one_head(qh, bias):
            s = _nt_dot(qh, kw) + bias
            e = jnp.exp(s - jnp.max(s, axis=-1, keepdims=True))
            return _dot(e.astype(BF16), vw) / jnp.sum(e, axis=-1, keepdims=True)

        o0 = one_head(jnp.where(first, q2, zero), bias_ref[pattern, 0])
        o1 = one_head(jnp.where(first, zero, q2), bias_ref[pattern, 1])
        o_ref[qrows, :] = jnp.where(first, o0, o1).astype(o_ref.dtype)


def _neighbourhood_attention(h, rpb):
    t = h.shape[0]
    rows = t // GRID_W
    n_blk = rows // NA_ROWS
    groups = D_MODEL // LANES
    tq = NA_SUBBLOCKS * NA_ROWS * GRID_W
    bias = _na_bias_table(rpb, rows)
    return pl.pallas_call(
        functools.partial(_na_kernel, rows=rows),
        out_shape=jax.ShapeDtypeStruct((t, D_MODEL), BF16),
        grid=(groups, n_blk // NA_SUBBLOCKS),
        in_specs=[
            pl.BlockSpec((tq, LANES), lambda p, rb: (rb, p)),
            pl.BlockSpec((t, LANES), lambda p, rb: (0, groups + p)),
            pl.BlockSpec((t, LANES), lambda p, rb: (0, 2 * groups + p)),
            pl.BlockSpec((bias.shape[0], HEADS_PER_GROUP) + bias.shape[2:], lambda p, rb: (0, p, 0, 0)),
        ],
        out_specs=pl.BlockSpec((tq, LANES), lambda p, rb: (rb, p)),
        compiler_params=_params(("parallel", "parallel")),
        name="neighbourhood_attn",
    )(h, h, h, bias)


def _odd_mixer(x, w_in, w_out, rpb, g, b):
    h = _proj_odd(x, w_in.astype(BF16))
    y = _neighbourhood_attention(h, rpb)
    return _out_odd(x, y, w_out.astype(BF16), g, b)


def _router_kernel(x_ref, wt_ref, b_ref, idx_ref, gate_ref, rank_ref, count_ref):
    tm = x_ref.shape[0]

    @pl.when(pl.program_id(0) == 0)
    def _init():
        count_ref[...] = jnp.zeros_like(count_ref)

    logits = lax.dot_general(wt_ref[...], x_ref[...], (((1,), (1,)), ((), ())),
                             precision=lax.Precision.HIGHEST, preferred_element_type=F32)
    logits = logits + b_ref[...]
    expert = lax.broadcasted_iota(jnp.int32, logits.shape, 0)
    vals, idxs = [], []
    for _ in range(TOP_K):
        m = jnp.max(logits, axis=0, keepdims=True)
        sel = jnp.min(jnp.where(logits == m, expert, N_EXPERTS), axis=0, keepdims=True)
        vals.append(m)
        idxs.append(sel)
        logits = jnp.where(expert == sel, -jnp.inf, logits)
    exps = [jnp.exp(v - vals[0]) for v in vals]
    den = exps[0] + exps[1] + exps[2] + exps[3]
    for k in range(TOP_K):
        idx_ref[k:k + 1, :] = idxs[k]
        gate_ref[k:k + 1, :] = exps[k] / den

    onehots = [(expert == idxs[k]).astype(F32) for k in range(TOP_K)]
    per_tok = onehots[0] + onehots[1] + onehots[2] + onehots[3]
    earlier = lax.broadcasted_iota(jnp.int32, (tm, tm), 0) < lax.broadcasted_iota(jnp.int32, (tm, tm), 1)
    before = count_ref[...] + _dot(per_tok.astype(BF16), earlier.astype(BF16))
    for k in range(TOP_K):
        rank_ref[k:k + 1, :] = jnp.sum(onehots[k] * before, axis=0, keepdims=True).astype(jnp.int32)
    count_ref[...] += jnp.sum(per_tok, axis=1, keepdims=True)


def _router(x, router_w, router_b):
    t = x.shape[0]
    tm = ROUTER_TM
    row = pl.BlockSpec((TOP_K, tm), lambda i: (0, i))
    return pl.pallas_call(
        _router_kernel,
        out_shape=(jax.ShapeDtypeStruct((TOP_K, t), jnp.int32),
                   jax.ShapeDtypeStruct((TOP_K, t), F32),
                   jax.ShapeDtypeStruct((TOP_K, t), jnp.int32),
                   jax.ShapeDtypeStruct((N_EXPERTS, 1), F32)),
        grid=(t // tm,),
        in_specs=[
            pl.BlockSpec((tm, D_MODEL), lambda i: (i, 0)),
            pl.BlockSpec((N_EXPERTS, D_MODEL), lambda i: (0, 0)),
            pl.BlockSpec((N_EXPERTS, 1), lambda i: (0, 0)),
        ],
        out_specs=(row, row, row, pl.BlockSpec((N_EXPERTS, 1), lambda i: (0, 0))),
        compiler_params=_params(("arbitrary",)),
        name="moe_router",
    )(x, router_w.T, router_b.reshape(-1, 1))


def _slot_layout(top_i, rank, counts):
    t = top_i.shape[1]
    bm = MOE_BM
    experts = jnp.arange(N_EXPERTS, dtype=jnp.int32)
    counts = counts.reshape(-1).astype(jnp.int32)
    pcounts = (counts + bm - 1) // bm * bm
    pend = jnp.cumsum(pcounts)
    pstart = pend - pcounts
    pos = rank + jnp.sum(jnp.where(top_i[:, :, None] == experts, pstart, 0), axis=-1)
    n_blk = (TOP_K * t) // bm + N_EXPERTS
    blk_start = jnp.arange(n_blk, dtype=jnp.int32) * bm
    blk_e = jnp.minimum(jnp.sum((pend[None, :] <= blk_start[:, None]).astype(jnp.int32), axis=1),
                        N_EXPERTS - 1)
    n_used = (pend[-1] // bm).astype(jnp.int32).reshape(1)
    start = jnp.cumsum(counts) - counts
    sorted_idx = rank + jnp.sum(jnp.where(top_i[:, :, None] == experts, start, 0), axis=-1)
    blk_row0 = start[blk_e] + blk_start - pstart[blk_e]
    return pos.astype(jnp.int32), sorted_idx.astype(jnp.int32), blk_e, blk_row0.astype(jnp.int32), n_used, n_blk


def _row_copy(src, src_row, dst, dst_row, sem):
    return pltpu.make_async_copy(src.at[pl.ds(src_row, 1)], dst.at[pl.ds(dst_row, 1)], sem)


def _invert_kernel(sorted_idx_ref, inv_ref):
    n = sorted_idx_ref.shape[0]

    def body(group, carry):
        for u in range(ROW_UNROLL):
            f = group * ROW_UNROLL + u
            inv_ref[sorted_idx_ref[f]] = f
        return carry

    lax.fori_loop(0, n // ROW_UNROLL, body, 0)


def _invert_permutation(sorted_idx_flat):
    return pl.pallas_call(
        _invert_kernel,
        out_shape=jax.ShapeDtypeStruct(sorted_idx_flat.shape, jnp.int32),
        in_specs=[pl.BlockSpec(memory_space=pltpu.SMEM)],
        out_specs=pl.BlockSpec(memory_space=pltpu.SMEM),
        name="moe_invert",
    )(sorted_idx_flat)


def _expert_kernel(inv_ref, blk_e_ref, blk_row0_ref, n_used_ref, x_hbm, wu_ref, bu_ref, wd_ref, bd_ref, o_ref,
                   wu_bf_ref, wd_bf_ref, xbuf0_ref, xbuf1_ref, gather_sem, *, n_tok):
    bm = MOE_BM
    b = pl.program_id(0)
    n_used = n_used_ref[0]
    used = b < n_used
    half = LANES // 2
    n_groups = D_FF // LANES
    n_assign = inv_ref.shape[0]
    new_expert = jnp.logical_or(b == 0, blk_e_ref[b] != blk_e_ref[jnp.maximum(b - 1, 0)])
    xbufs = (xbuf0_ref, xbuf1_ref)

    def gather_rows(blk, parity):
        row0 = blk_row0_ref[blk]
        for r in range(bm):
            flat = inv_ref[jnp.minimum(row0 + r, n_assign - 1)]
            _row_copy(x_hbm, flat & (n_tok - 1), xbufs[parity], r, gather_sem.at[parity]).start()

    def wait_rows(parity):
        pltpu.make_async_copy(x_hbm.at[pl.ds(0, bm)], xbufs[parity], gather_sem.at[parity]).wait()

    @pl.when(b == 0)
    def _first_rows():
        gather_rows(0, 0)

    @pl.when(jnp.logical_and(used, new_expert))
    def _stage_weights():
        wu_bf_ref[...] = wu_ref[...].astype(BF16)
        row = lax.broadcasted_iota(jnp.int32, (LANES, LANES), 0)
        col = lax.broadcasted_iota(jnp.int32, (LANES, LANES), 1)
        select = (col == row // 2 + half * (row % 2)).astype(BF16)
        for g in range(n_groups):
            rows = slice(g * LANES, (g + 1) * LANES)
            wd_bf_ref[rows, :] = _dot(select, wd_ref[rows, :].astype(BF16)).astype(BF16)

    def compute(parity):
        wait_rows(parity)
        gather_rows(jnp.minimum(b + 1, n_used - 1), 1 - parity)
        xb = xbufs[parity][...].astype(BF16)
        h = _dot(xb, wu_bf_ref[...]) + bu_ref[...]
        even = lax.broadcasted_iota(jnp.int32, (1, LANES), 1) % 2 == 0
        packed = []
        for g in range(n_groups):
            lo = h[:, 2 * g * LANES:(2 * g + 1) * LANES]
            hi = h[:, (2 * g + 1) * LANES:(2 * g + 2) * LANES]
            x_glu = jnp.minimum(jnp.where(even, lo, pltpu.roll(hi, 1, 1)), SWIGLU_LIMIT)
            x_lin = jnp.clip(jnp.where(even, pltpu.roll(lo, LANES - 1, 1), hi), -SWIGLU_LIMIT, SWIGLU_LIMIT)
            packed.append((x_glu * jax.nn.sigmoid(SWIGLU_ALPHA * x_glu) * (x_lin + 1.0)).astype(BF16))
        act = jnp.concatenate(packed, axis=1)
        o_ref[...] = _dot(act, wd_bf_ref[...]) + bd_ref[...]

        @pl.when(b == n_used - 1)
        def _drain():
            wait_rows(1 - parity)

    for parity in range(2):
        pl.when(jnp.logical_and(used, b % 2 == parity))(functools.partial(compute, parity))

    @pl.when(jnp.logical_not(used))
    def _unused():
        o_ref[...] = jnp.zeros_like(o_ref)


def _experts(x, inv, blk_e, blk_row0, n_used, n_blk, layer, w_up, b_up, w_down, b_down):
    bm = MOE_BM
    t = x.shape[0]
    assert t & (t - 1) == 0

    def wspec(shape):
        return pl.BlockSpec((None, None) + shape, lambda b, inv, e, r, n: (layer, e[b], 0, 0))

    return pl.pallas_call(
        functools.partial(_expert_kernel, n_tok=t),
        out_shape=jax.ShapeDtypeStruct((n_blk * bm, D_MODEL), F32),
        grid_spec=pltpu.PrefetchScalarGridSpec(
            num_scalar_prefetch=4,
            grid=(n_blk,),
            in_specs=[
                pl.BlockSpec(memory_space=pl.ANY),
                wspec((D_MODEL, 2 * D_FF)), wspec((1, 2 * D_FF)),
                wspec((D_FF, D_MODEL)), wspec((1, D_MODEL)),
            ],
            out_specs=pl.BlockSpec((bm, D_MODEL), lambda b, inv, e, r, n: (b, 0)),
            scratch_shapes=[pltpu.VMEM((D_MODEL, 2 * D_FF), BF16), pltpu.VMEM((D_FF, D_MODEL), BF16),
                            pltpu.VMEM((bm, D_MODEL), F32), pltpu.VMEM((bm, D_MODEL), F32),
                            pltpu.SemaphoreType.DMA((2,))],
        ),
        compiler_params=_params(("arbitrary",)),
        name="moe_experts",
    )(inv, blk_e, blk_row0, n_used, x, w_up, b_up, w_down, b_down)


def _combine_kernel(pos_ref, y_hbm, x_ref, gate_ref, g_ref, b_ref, o_ref, buf_ref, sem, *, n_tok, tm):
    i = pl.program_id(0)
    slot = i % 2

    def gather_tile(tile, buf_slot):
        base = tile * tm

        def issue(group, carry):
            j0 = pl.multiple_of(group * ROW_UNROLL, ROW_UNROLL)
            for u in range(ROW_UNROLL):
                for k in range(TOP_K):
                    _row_copy(y_hbm, pos_ref[k * n_tok + base + j0 + u], buf_ref.at[buf_slot, k], j0 + u,
                              sem.at[buf_slot]).start(priority=k % 2)
            return carry

        lax.fori_loop(0, tm // ROW_UNROLL, issue, 0)

    @pl.when(i == 0)
    def _first_tile():
        gather_tile(0, 0)

    @pl.when(i + 1 < pl.num_programs(0))
    def _next_tile():
        gather_tile(i + 1, 1 - slot)

    for k in range(TOP_K):
        pltpu.make_async_copy(y_hbm.at[pl.ds(0, tm)], buf_ref.at[slot, k], sem.at[slot]).wait()
    gates = gate_ref[...]
    ffn = gates[:, 0:1] * buf_ref[slot, 0]
    for k in range(1, TOP_K):
        ffn = ffn + gates[:, k:k + 1] * buf_ref[slot, k]
    z = DEEPNORM_ALPHA * x_ref[...] + ffn
    o_ref[...] = _layer_norm(z, g_ref[...], b_ref[...])


def _combine(x, y_slots, pos_flat, gates_tk, g, b):
    t = x.shape[0]
    tm = COMBINE_TM
    full = pl.BlockSpec((tm, D_MODEL), lambda i, p: (i, 0))
    vec = pl.BlockSpec((1, D_MODEL), lambda i, p: (0, 0))
    return pl.pallas_call(
        functools.partial(_combine_kernel, n_tok=t, tm=tm),
        out_shape=jax.ShapeDtypeStruct((t, D_MODEL), F32),
        grid_spec=pltpu.PrefetchScalarGridSpec(
            num_scalar_prefetch=1,
            grid=(t // tm,),
            in_specs=[pl.BlockSpec(memory_space=pl.ANY), full,
                      pl.BlockSpec((tm, TOP_K), lambda i, p: (i, 0)), vec, vec],
            out_specs=full,
            scratch_shapes=[pltpu.VMEM((2, TOP_K, tm, D_MODEL), F32), pltpu.SemaphoreType.DMA((2,))],
        ),
        compiler_params=_params(("arbitrary",)),
        name="moe_combine",
    )(pos_flat, y_slots, x, gates_tk, g.reshape(1, -1), b.reshape(1, -1))


def _moe_layer(x, layer, router_w, router_b, w_up, b_up, w_down, b_down, g, b):
    top_i, gates, rank, counts = _router(x, router_w, router_b)
    pos, sorted_idx, blk_e, blk_row0, n_used, n_blk = _slot_layout(top_i, rank, counts)
    pos_flat = pos.reshape(-1)
    inv = _invert_permutation(sorted_idx.reshape(-1))
    y_slots = _experts(x, inv, blk_e, blk_row0, n_used, n_blk, layer, w_up, b_up[:, :, None, :], w_down,
                       b_down[:, :, None, :])
    return _combine(x, y_slots, pos_flat, gates.T, g, b)


def kernel(x, ab_w_in, ab_w_out, ret_decay, c_w_in, c_w_out, c_rpb, ln_g, ln_b,
           router_w, router_b, exp_w_up, exp_b_up, exp_w_down, exp_b_down):
    b_, t_, d_ = x.shape
    assert b_ == 1 and d_ == D_MODEL
    xs = x.reshape(t_, d_)
    cos_t, sin_t = _rope_tables(t_)
    for layer in range(DEPTH):
        j = layer // 2
        if layer % 2 == 0:
            xs = _even_mixer(xs, ab_w_in[j], ab_w_out[j], ret_decay[j], ln_g[layer, 0], ln_b[layer, 0],
                             cos_t, sin_t)
        else:
            xs = _odd_mixer(xs, c_w_in[j], c_w_out[j], c_rpb[j], ln_g[layer, 0], ln_b[layer, 0])
        xs = _moe_layer(xs, layer, router_w[layer], router_b[layer], exp_w_up, exp_b_up,
                        exp_w_down, exp_b_down, ln_g[layer, 1], ln_b[layer, 1])
    return xs.reshape(b_, t_, d_)
```

```python
import functools

import jax
import jax.numpy as jnp
import numpy as np
from jax import lax
from jax.experimental import pallas as pl
from jax.experimental.pallas import tpu as pltpu

F32 = jnp.float32
BF16 = jnp.bfloat16

D_MODEL = 1024
HEAD_DIM = 64
LANES = 128
HEADS_PER_GROUP = LANES // HEAD_DIM
N_RET_HEADS = 8
N_DIL_HEADS = 8
N_NA_HEADS = 16
RET_W = N_RET_HEADS * HEAD_DIM
DIL_W = N_DIL_HEADS * HEAD_DIM
EVEN_IN = 5 * RET_W + 3 * DIL_W
ODD_IN = 3 * D_MODEL
ROPE_THETA = 10000.0
DIL_CONFIGS = ((128, 1), (512, 4), (2048, 16))
DIL_HALF = 64
GRID_W = 64
NA_KH = 8
NA_KW = 16
N_EXPERTS = 32
TOP_K = 4
D_FF = D_MODEL
SWIGLU_LIMIT = 7.0
SWIGLU_ALPHA = 1.702
LN_EPS = 1e-5
GN_EPS = 1e-6
NEG_INF = -1e30
DEPTH = 4
DEEPNORM_ALPHA = (2.0 * DEPTH) ** 0.25
QK_SCALE = HEAD_DIM ** -0.5

VMEM_LIMIT_BYTES = 56 * 1024 * 1024

PROJ_TM = 512
RET_CHUNK = 256
DIL_TQ = 128
DIL_TK = DIL_TQ + 2 * DIL_HALF
DIL_SUBTILES = 4
NA_ROWS = 2
NA_KROWS = NA_ROWS + NA_KH - 1
NA_SUBBLOCKS = 4
ROUTER_TM = 512
MOE_BM = 512
COMBINE_TM = 256
ROW_UNROLL = 8


def _params(sem, **kw):
    return pltpu.CompilerParams(dimension_semantics=sem, vmem_limit_bytes=VMEM_LIMIT_BYTES, **kw)


def _nt_dot(a, b):
    return lax.dot_general(a, b, (((1,), (1,)), ((), ())), preferred_element_type=F32)


def _dot(a, b):
    return jnp.dot(a, b, preferred_element_type=F32)


def _layer_norm(z, g, b):
    mu = jnp.mean(z, axis=-1, keepdims=True)
    d = z - mu
    var = jnp.mean(d * d, axis=-1, keepdims=True)
    return d * lax.rsqrt(var + LN_EPS) * g + b


def _first_head_lanes(shape):
    return lax.broadcasted_iota(jnp.int32, shape, len(shape) - 1) < HEAD_DIM


def _rope_tables(t):
    pos = jnp.arange(t, dtype=F32)
    inv = ROPE_THETA ** (-jnp.arange(0, HEAD_DIM, 2, dtype=F32) / HEAD_DIM)
    ang = pos[:, None] * inv[None, :]
    cos, sin = jnp.cos(ang), jnp.sin(ang)
    return (jnp.concatenate([cos, cos, cos, cos], axis=1),
            jnp.concatenate([-sin, sin, -sin, sin], axis=1))


def _rope(h, c, s):
    half = HEAD_DIM // 2
    lane = lax.broadcasted_iota(jnp.int32, h.shape, 1)
    first_half = (lane // half) % 2 == 0
    partner = jnp.where(first_half, pltpu.roll(h, LANES - half, 1), pltpu.roll(h, half, 1))
    return h * c + partner * s


def _proj_even_kernel(x_ref, w_ref, c_ref, s_ref, o_ref, *class_refs):
    tm = x_ref.shape[0]
    xb = x_ref[...].astype(BF16)
    c = c_ref[...]
    s = s_ref[...]
    rope_groups = {0: 1.0, 1: QK_SCALE, 5: QK_SCALE, 6: 1.0}
    for g in range(EVEN_IN // RET_W):
        h = _dot(xb, w_ref[:, g * RET_W:(g + 1) * RET_W])
        for j in range(RET_W // LANES):
            hj = h[:, j * LANES:(j + 1) * LANES]
            if g in rope_groups:
                hj = _rope(hj, c, s) * rope_groups[g]
            col = g * RET_W + j * LANES
            o_ref[:, col:col + LANES] = hj.astype(BF16)
    h_dil = o_ref[:, 5 * RET_W:]
    row = lax.broadcasted_iota(jnp.int32, (tm, tm), 0)
    col = lax.broadcasted_iota(jnp.int32, (tm, tm), 1)
    for cls_ref, (_, dil) in zip(class_refs, DIL_CONFIGS[1:]):
        per_class = tm // dil
        select = (col == dil * (row % per_class) + row // per_class).astype(BF16)
        regrouped = _dot(select, h_dil).astype(BF16)
        for r in range(dil):
            cls_ref[r] = regrouped[r * per_class:(r + 1) * per_class]


def _proj_even(x, w_bf, cos_t, sin_t):
    t = x.shape[0]
    tm = PROJ_TM
    dil_w = 3 * DIL_W
    class_shapes = [jax.ShapeDtypeStruct((dil, t // dil, dil_w), BF16) for _, dil in DIL_CONFIGS[1:]]
    class_specs = [pl.BlockSpec((dil, tm // dil, dil_w), lambda i: (0, i, 0)) for _, dil in DIL_CONFIGS[1:]]
    return pl.pallas_call(
        _proj_even_kernel,
        out_shape=[jax.ShapeDtypeStruct((t, EVEN_IN), BF16)] + class_shapes,
        grid=(t // tm,),
        in_specs=[
            pl.BlockSpec((tm, D_MODEL), lambda i: (i, 0)),
            pl.BlockSpec((D_MODEL, EVEN_IN), lambda i: (0, 0)),
            pl.BlockSpec((tm, LANES), lambda i: (i, 0)),
            pl.BlockSpec((tm, LANES), lambda i: (i, 0)),
        ],
        out_specs=[pl.BlockSpec((tm, EVEN_IN), lambda i: (i, 0))] + class_specs,
        compiler_params=_params(("parallel",)),
        name="proj_even",
    )(x, w_bf, cos_t, sin_t)


def _proj_odd_kernel(x_ref, w_ref, o_ref):
    xb = x_ref[...].astype(BF16)
    for g in range(3):
        h = _dot(xb, w_ref[:, g * D_MODEL:(g + 1) * D_MODEL])
        if g == 0:
            h = h * QK_SCALE
        o_ref[:, g * D_MODEL:(g + 1) * D_MODEL] = h.astype(BF16)


def _proj_odd(x, w_bf):
    t = x.shape[0]
    tm = PROJ_TM
    return pl.pallas_call(
        _proj_odd_kernel,
        out_shape=jax.ShapeDtypeStruct((t, ODD_IN), BF16),
        grid=(t // tm,),
        in_specs=[
            pl.BlockSpec((tm, D_MODEL), lambda i: (i, 0)),
            pl.BlockSpec((D_MODEL, ODD_IN), lambda i: (0, 0)),
        ],
        out_specs=pl.BlockSpec((tm, ODD_IN), lambda i: (i, 0)),
        compiler_params=_params(("parallel",)),
        name="proj_odd",
    )(x, w_bf)


def _ret_kernel(lg_ref, q_ref, k_ref, v_ref, g_ref, *rest, rev):
    if rev:
        o_ref, r_ref, d0_ref, d1_ref, z_ref, xi_ref, dc_ref = rest
        prev_ref = None
    else:
        prev_ref, o_ref, r_ref, d0_ref, d1_ref, z_ref, xi_ref, dc_ref = rest
    C = RET_CHUNK
    groups = RET_W // LANES
    c = pl.program_id(0)
    first = _first_head_lanes((1, LANES))
    direction = 1 if rev else 0

    @pl.when(c == 0)
    def _init():
        row = lax.broadcasted_iota(jnp.int32, (C, C), 0)
        col = lax.broadcasted_iota(jnp.int32, (C, C), 1)
        dist = (col - row) if rev else (row - col)
        keep = (dist > 0) if rev else (dist >= 0)
        distf = jnp.maximum(dist, 0).astype(F32)
        idx = lax.broadcasted_iota(jnp.int32, (C, LANES), 0).astype(F32)
        for p in range(groups):
            lg0 = lg_ref[direction, 2 * p]
            lg1 = lg_ref[direction, 2 * p + 1]
            d0_ref[p] = jnp.where(keep, jnp.exp(lg0 * distf), 0.0)
            d1_ref[p] = jnp.where(keep, jnp.exp(lg1 * distf), 0.0)
            lgl = jnp.where(first, lg0, lg1)
            z_ref[p] = jnp.exp(lgl * (idx if rev else (C - 1.0 - idx)))
            xi_ref[p] = jnp.exp(lgl * ((C - idx) if rev else (idx + 1.0)))
            dc_ref[p] = jnp.exp(lgl * float(C))
        r_ref[...] = jnp.zeros_like(r_ref)

    rr = lax.broadcasted_iota(jnp.int32, (LANES, LANES), 0) < HEAD_DIM
    cc = lax.broadcasted_iota(jnp.int32, (LANES, LANES), 1) < HEAD_DIM
    inv_n = 1.0 / HEAD_DIM
    for p in range(groups):
        lanes = slice(p * LANES, (p + 1) * LANES)
        q2 = q_ref[:, lanes]
        k2 = k_ref[:, lanes]
        v2 = v_ref[:, lanes]
        zero = jnp.zeros_like(q2)
        q0 = jnp.where(first, q2, zero)
        q1 = jnp.where(first, zero, q2)
        a0 = (_nt_dot(q0, k2) * d0_ref[p]).astype(BF16)
        a1 = (_nt_dot(q1, k2) * d1_ref[p]).astype(BF16)
        state = r_ref[p]
        cross = _dot(q2, state.astype(BF16)) * xi_ref[p]
        o = jnp.where(first, _dot(a0, v2), _dot(a1, v2)) + cross

        kz = (k2.astype(F32) * z_ref[p]).T.astype(BF16)
        r_ref[p] = jnp.where(rr == cc, state * dc_ref[p] + _dot(kz, v2), 0.0)

        s_first = jnp.sum(jnp.where(first, o, 0.0), axis=-1, keepdims=True)
        s_second = jnp.sum(jnp.where(first, 0.0, o), axis=-1, keepdims=True)
        d = o - jnp.where(first, s_first, s_second) * inv_n
        dd = d * d
        v_first = jnp.sum(jnp.where(first, dd, 0.0), axis=-1, keepdims=True)
        v_second = jnp.sum(jnp.where(first, 0.0, dd), axis=-1, keepdims=True)
        var = jnp.where(first, v_first, v_second) * inv_n
        gate = g_ref[:, lanes].astype(F32)
        y = gate * jax.nn.sigmoid(gate) * (d * lax.rsqrt(var + GN_EPS))
        if prev_ref is not None:
            y = y + prev_ref[:, lanes].astype(F32)
        o_ref[:, lanes] = y.astype(o_ref.dtype)


def _retention_pass(h, log_gamma, prev, rev):
    t = h.shape[0]
    C = RET_CHUNK
    nc = t // C
    groups = RET_W // LANES

    def col_spec(group):
        return pl.BlockSpec((C, RET_W), lambda c, group=group: ((nc - 1 - c) if rev else c, group))

    in_specs = [
        pl.BlockSpec(memory_space=pltpu.SMEM),
        col_spec(0), col_spec(1), col_spec(2), col_spec(4 if rev else 3),
    ]
    args = [log_gamma, h, h, h, h]
    if not rev:
        in_specs.append(col_spec(0))
        args.append(prev)
    return pl.pallas_call(
        functools.partial(_ret_kernel, rev=rev),
        out_shape=jax.ShapeDtypeStruct((t, RET_W), BF16),
        grid=(nc,),
        in_specs=in_specs,
        out_specs=col_spec(0),
        scratch_shapes=[
            pltpu.VMEM((groups, LANES, LANES), F32),
            pltpu.VMEM((groups, C, C), F32),
            pltpu.VMEM((groups, C, C), F32),
            pltpu.VMEM((groups, C, LANES), F32),
            pltpu.VMEM((groups, C, LANES), F32),
            pltpu.VMEM((groups, 1, LANES), F32),
        ],
        compiler_params=_params(("arbitrary",)),
        name="retention_bwd" if rev else "retention_fwd",
    )(*args)


def _retention(h, decay_logits):
    log_gamma = jax.nn.log_sigmoid(decay_logits.astype(F32))
    y_bwd = _retention_pass(h, log_gamma, None, rev=True)
    return _retention_pass(h, log_gamma, y_bwd, rev=False)


def _dil_kernel(q_ref, k_ref, v_ref, o_ref, lse_ref, *, length):
    first = _first_head_lanes((1, LANES))
    row = lax.broadcasted_iota(jnp.int32, (DIL_TQ, DIL_TK), 0)
    col = lax.broadcasted_iota(jnp.int32, (DIL_TQ, DIL_TK), 1)
    for sub in range(DIL_SUBTILES):
        i = pl.program_id(2) * DIL_SUBTILES + sub
        rows = slice(sub * DIL_TQ, (sub + 1) * DIL_TQ)
        start = jnp.clip(i * DIL_TQ - DIL_HALF, 0, length - DIL_TK)
        start = pl.multiple_of(start, DIL_HALF)
        kw = k_ref[pl.ds(start, DIL_TK), :]
        vw = v_ref[pl.ds(start, DIL_TK), :]
        valid = jnp.abs((start + col) - (i * DIL_TQ + row)) <= DIL_HALF
        q2 = q_ref[rows, :]
        zero = jnp.zeros_like(q2)

        def one_head(qh):
            s = jnp.where(valid, _nt_dot(qh, kw), NEG_INF)
            m = jnp.max(s, axis=-1, keepdims=True)
            e = jnp.exp(s - m)
            den = jnp.sum(e, axis=-1, keepdims=True)
            return _dot(e.astype(BF16), vw) / den, m + jnp.log(den)

        o0, l0 = one_head(jnp.where(first, q2, zero))
        o1, l1 = one_head(jnp.where(first, zero, q2))
        o_ref[rows, :] = jnp.where(first, o0, o1).astype(o_ref.dtype)
        lse_ref[rows, :] = jnp.where(first, l0, l1)


def _dilated_branch(h_cls, col_base, t):
    dil, length, _ = h_cls.shape
    groups = DIL_W // LANES
    tq = DIL_TQ * DIL_SUBTILES
    o, lse = pl.pallas_call(
        functools.partial(_dil_kernel, length=length),
        out_shape=(jax.ShapeDtypeStruct((length, dil * DIL_W), BF16),
                   jax.ShapeDtypeStruct((length, dil * DIL_W), F32)),
        grid=(dil, groups, length // tq),
        in_specs=[
            pl.BlockSpec((None, tq, LANES), lambda r, p, i: (r, i, col_base + p)),
            pl.BlockSpec((None, length, LANES), lambda r, p, i: (r, 0, col_base + groups + p)),
            pl.BlockSpec((None, length, LANES), lambda r, p, i: (r, 0, col_base + 2 * groups + p)),
        ],
        out_specs=(
            pl.BlockSpec((tq, LANES), lambda r, p, i: (i, r * groups + p)),
            pl.BlockSpec((tq, LANES), lambda r, p, i: (i, r * groups + p)),
        ),
        compiler_params=_params(("parallel", "parallel", "parallel")),
        name=f"dilated_attn_d{dil}",
    )(h_cls, h_cls, h_cls)
    return o.reshape(t, DIL_W), lse.reshape(t, DIL_W)


def _out_even_kernel(x_ref, yr_ref, o1_ref, o2_ref, o3_ref, l1_ref, l2_ref, l3_ref,
                     w_ref, g_ref, b_ref, out_ref):
    l1, l2, l3 = l1_ref[...], l2_ref[...], l3_ref[...]
    m = jnp.maximum(jnp.maximum(l1, l2), l3)
    e1, e2, e3 = jnp.exp(l1 - m), jnp.exp(l2 - m), jnp.exp(l3 - m)
    den = e1 + e2 + e3
    y_dil = (e1 / den) * o1_ref[...].astype(F32) + (e2 / den) * o2_ref[...].astype(F32) \
        + (e3 / den) * o3_ref[...].astype(F32)
    mix = _dot(yr_ref[...], w_ref[:RET_W, :]) + _dot(y_dil.astype(BF16), w_ref[RET_W:, :])
    z = DEEPNORM_ALPHA * x_ref[...] + mix
    out_ref[...] = _layer_norm(z, g_ref[...], b_ref[...])


def _out_even(x, y_ret, outs, lses, w_bf, g, b):
    t = x.shape[0]
    tm = PROJ_TM
    half = pl.BlockSpec((tm, RET_W), lambda i: (i, 0))
    vec = pl.BlockSpec((1, D_MODEL), lambda i: (0, 0))
    return pl.pallas_call(
        _out_even_kernel,
        out_shape=jax.ShapeDtypeStruct((t, D_MODEL), F32),
        grid=(t // tm,),
        in_specs=[pl.BlockSpec((tm, D_MODEL), lambda i: (i, 0))] + [half] * 7
        + [pl.BlockSpec((D_MODEL, D_MODEL), lambda i: (0, 0)), vec, vec],
        out_specs=pl.BlockSpec((tm, D_MODEL), lambda i: (i, 0)),
        compiler_params=_params(("parallel",)),
        name="out_proj_even",
    )(x, y_ret, *outs, *lses, w_bf, g.reshape(1, -1), b.reshape(1, -1))


def _out_odd_kernel(x_ref, y_ref, w_ref, g_ref, b_ref, out_ref):
    z = DEEPNORM_ALPHA * x_ref[...] + _dot(y_ref[...], w_ref[...])
    out_ref[...] = _layer_norm(z, g_ref[...], b_ref[...])


def _out_odd(x, y, w_bf, g, b):
    t = x.shape[0]
    tm = PROJ_TM
    full = pl.BlockSpec((tm, D_MODEL), lambda i: (i, 0))
    vec = pl.BlockSpec((1, D_MODEL), lambda i: (0, 0))
    return pl.pallas_call(
        _out_odd_kernel,
        out_shape=jax.ShapeDtypeStruct((t, D_MODEL), F32),
        grid=(t // tm,),
        in_specs=[full, full, pl.BlockSpec((D_MODEL, D_MODEL), lambda i: (0, 0)), vec, vec],
        out_specs=full,
        compiler_params=_params(("parallel",)),
        name="out_proj_odd",
    )(x, y, w_bf, g.reshape(1, -1), b.reshape(1, -1))


def _even_mixer(x, w_in, w_out, decay_logits, g, b, cos_t, sin_t):
    t = x.shape[0]
    h, *h_classes = _proj_even(x, w_in.astype(BF16), cos_t, sin_t)
    y_ret = _retention(h, decay_logits)
    outs, lses = [], []
    for h_cls, col_base in [(h.reshape(1, t, EVEN_IN), 5 * RET_W // LANES)] + [(hc, 0) for hc in h_classes]:
        o, lse = _dilated_branch(h_cls, col_base, t)
        outs.append(o)
        lses.append(lse)
    return _out_even(x, y_ret, outs, lses, w_out.astype(BF16), g, b)


def _na_bias_table(rpb, rows):
    n_blk = rows // NA_ROWS
    blocks = np.array([0, 1, 2, n_blk - 2, n_blk - 1])
    kstart = np.clip(NA_ROWS * blocks - NA_KH // 2, 0, rows - NA_KROWS)
    r = NA_ROWS * blocks[:, None] + np.arange(NA_ROWS)[None, :]
    r0 = np.clip(r - NA_KH // 2, 0, rows - NA_KH)
    kr = kstart[:, None] + np.arange(NA_KROWS)[None, :]
    row_ok = (kr[:, None, :] >= r0[:, :, None]) & (kr[:, None, :] < r0[:, :, None] + NA_KH)
    dr = kr[:, None, :] - r[:, :, None] + (NA_KH - 1)
    c = np.arange(GRID_W)
    c0 = np.clip(c - NA_KW // 2, 0, GRID_W - NA_KW)
    kc = np.arange(GRID_W)
    col_ok = (kc[None, :] >= c0[:, None]) & (kc[None, :] < c0[:, None] + NA_KW)
    pad = GRID_W - NA_KW
    rp = jnp.pad(rpb.astype(F32), ((0, 0), (0, 0), (pad, pad)))
    by_col = jnp.stack([rp[:, :, GRID_W - 1 - ci:2 * GRID_W - 1 - ci] for ci in range(GRID_W)], axis=2)
    by_col = jnp.where(col_ok[None, None], by_col, NEG_INF)
    outside = jnp.full((N_NA_HEADS, GRID_W, GRID_W), NEG_INF, F32)
    patterns = []
    for pi in range(5):
        q_rows = []
        for qi in range(NA_ROWS):
            blocks_ = [by_col[:, int(dr[pi, qi, ki])] if row_ok[pi, qi, ki] else outside
                       for ki in range(NA_KROWS)]
            q_rows.append(jnp.concatenate(blocks_, axis=-1))
        patterns.append(jnp.concatenate(q_rows, axis=-2))
    return jnp.stack(patterns, axis=0)


def _na_kernel(q_ref, k_ref, v_ref, bias_ref, o_ref, *, rows):
    n_blk = rows // NA_ROWS
    tq = NA_ROWS * GRID_W
    nk = NA_KROWS * GRID_W
    first = _first_head_lanes((1, LANES))
    for sub in range(NA_SUBBLOCKS):
        rb = pl.program_id(1) * NA_SUBBLOCKS + sub
        qrows = slice(sub * tq, (sub + 1) * tq)
        kstart = jnp.clip(NA_ROWS * rb - NA_KH // 2, 0, rows - NA_KROWS) * GRID_W
        kstart = pl.multiple_of(kstart, GRID_W)
        kw = k_ref[pl.ds(kstart, nk), :]
        vw = v_ref[pl.ds(kstart, nk), :]
        pattern = jnp.clip(rb, 0, 2) + jnp.clip(rb - (n_blk - 3), 0, 2)
        q2 = q_ref[qrows, :]
        zero = jnp.zeros_like(q2)

        def one_head(qh, bias):
            s = _nt_dot(qh, kw) + bias
            e = jnp.exp(s - jnp.max(s, axis=-1, keepdims=True))
            return _dot(e.astype(BF16), vw) / jnp.sum(e, axis=-1, keepdims=True)

        o0 = one_head(jnp.where(first, q2, zero), bias_ref[pattern, 0])
        o1 = one_head(jnp.where(first, zero, q2), bias_ref[pattern, 1])
        o_ref[qrows, :] = jnp.where(first, o0, o1).astype(o_ref.dtype)


def _neighbourhood_attention(h, rpb):
    t = h.shape[0]
    rows = t // GRID_W
    n_blk = rows // NA_ROWS
    groups = D_MODEL // LANES
    tq = NA_SUBBLOCKS * NA_ROWS * GRID_W
    bias = _na_bias_table(rpb, rows)
    return pl.pallas_call(
        functools.partial(_na_kernel, rows=rows),
        out_shape=jax.ShapeDtypeStruct((t, D_MODEL), BF16),
        grid=(groups, n_blk // NA_SUBBLOCKS),
        in_specs=[
            pl.BlockSpec((tq, LANES), lambda p, rb: (rb, p)),
            pl.BlockSpec((t, LANES), lambda p, rb: (0, groups + p)),
            pl.BlockSpec((t, LANES), lambda p, rb: (0, 2 * groups + p)),
            pl.BlockSpec((bias.shape[0], HEADS_PER_GROUP) + bias.shape[2:], lambda p, rb: (0, p, 0, 0)),
        ],
        out_specs=pl.BlockSpec((tq, LANES), lambda p, rb: (rb, p)),
        compiler_params=_params(("parallel", "parallel")),
        name="neighbourhood_attn",
    )(h, h, h, bias)


def _odd_mixer(x, w_in, w_out, rpb, g, b):
    h = _proj_odd(x, w_in.astype(BF16))
    y = _neighbourhood_attention(h, rpb)
    return _out_odd(x, y, w_out.astype(BF16), g, b)


def _router_kernel(x_ref, wt_ref, b_ref, idx_ref, gate_ref, rank_ref, count_ref):
    tm = x_ref.shape[0]

    @pl.when(pl.program_id(0) == 0)
    def _init():
        count_ref[...] = jnp.zeros_like(count_ref)

    logits = lax.dot_general(wt_ref[...], x_ref[...], (((1,), (1,)), ((), ())),
                             precision=lax.Precision.HIGHEST, preferred_element_type=F32)
    logits = logits + b_ref[...]
    expert = lax.broadcasted_iota(jnp.int32, logits.shape, 0)
    vals, idxs = [], []
    for _ in range(TOP_K):
        m = jnp.max(logits, axis=0, keepdims=True)
        sel = jnp.min(jnp.where(logits == m, expert, N_EXPERTS), axis=0, keepdims=True)
        vals.append(m)
        idxs.append(sel)
        logits = jnp.where(expert == sel, -jnp.inf, logits)
    exps = [jnp.exp(v - vals[0]) for v in vals]
    den = exps[0] + exps[1] + exps[2] + exps[3]
    for k in range(TOP_K):
        idx_ref[k:k + 1, :] = idxs[k]
        gate_ref[k:k + 1, :] = exps[k] / den

    onehots = [(expert == idxs[k]).astype(F32) for k in range(TOP_K)]
    per_tok = onehots[0] + onehots[1] + onehots[2] + onehots[3]
    earlier = lax.broadcasted_iota(jnp.int32, (tm, tm), 0) < lax.broadcasted_iota(jnp.int32, (tm, tm), 1)
    before = count_ref[...] + _dot(per_tok.astype(BF16), earlier.astype(BF16))
    for k in range(TOP_K):
        rank_ref[k:k + 1, :] = jnp.sum(onehots[k] * before, axis=0, keepdims=True).astype(jnp.int32)
    count_ref[...] += jnp.sum(per_tok, axis=1, keepdims=True)


def _router(x, router_w, router_b):
    t = x.shape[0]
    tm = ROUTER_TM
    row = pl.BlockSpec((TOP_K, tm), lambda i: (0, i))
    return pl.pallas_call(
        _router_kernel,
        out_shape=(jax.ShapeDtypeStruct((TOP_K, t), jnp.int32),
                   jax.ShapeDtypeStruct((TOP_K, t), F32),
                   jax.ShapeDtypeStruct((TOP_K, t), jnp.int32),
                   jax.ShapeDtypeStruct((N_EXPERTS, 1), F32)),
        grid=(t // tm,),
        in_specs=[
            pl.BlockSpec((tm, D_MODEL), lambda i: (i, 0)),
            pl.BlockSpec((N_EXPERTS, D_MODEL), lambda i: (0, 0)),
            pl.BlockSpec((N_EXPERTS, 1), lambda i: (0, 0)),
        ],
        out_specs=(row, row, row, pl.BlockSpec((N_EXPERTS, 1), lambda i: (0, 0))),
        compiler_params=_params(("arbitrary",)),
        name="moe_router",
    )(x, router_w.T, router_b.reshape(-1, 1))


def _slot_layout(top_i, rank, counts):
    t = top_i.shape[1]
    bm = MOE_BM
    experts = jnp.arange(N_EXPERTS, dtype=jnp.int32)
    counts = counts.reshape(-1).astype(jnp.int32)
    pcounts = (counts + bm - 1) // bm * bm
    pend = jnp.cumsum(pcounts)
    pstart = pend - pcounts
    pos = rank + jnp.sum(jnp.where(top_i[:, :, None] == experts, pstart, 0), axis=-1)
    n_blk = (TOP_K * t) // bm + N_EXPERTS
    blk_start = jnp.arange(n_blk, dtype=jnp.int32) * bm
    blk_e = jnp.minimum(jnp.sum((pend[None, :] <= blk_start[:, None]).astype(jnp.int32), axis=1),
                        N_EXPERTS - 1)
    n_used = (pend[-1] // bm).astype(jnp.int32).reshape(1)
    start = jnp.cumsum(counts) - counts
    sorted_idx = rank + jnp.sum(jnp.where(top_i[:, :, None] == experts, start, 0), axis=-1)
    blk_row0 = start[blk_e] + blk_start - pstart[blk_e]
    return pos.astype(jnp.int32), sorted_idx.astype(jnp.int32), blk_e, blk_row0.astype(jnp.int32), n_used, n_blk


def _row_copy(src, src_row, dst, dst_row, sem):
    return pltpu.make_async_copy(src.at[pl.ds(src_row, 1)], dst.at[pl.ds(dst_row, 1)], sem)


def _expert_kernel(sorted_idx_ref, blk_e_ref, blk_row0_ref, n_used_ref, x_hbm, wu_ref, bu_ref, wd_ref, bd_ref,
                   o_ref, wu_bf_ref, wd_bf_ref, xbuf0_ref, xbuf1_ref, inv_ref, gather_sem, *, n_tok):
    bm = MOE_BM
    b = pl.program_id(0)
    n_used = n_used_ref[0]
    used = b < n_used
    half = LANES // 2
    n_groups = D_FF // LANES
    n_assign = sorted_idx_ref.shape[0]

    @pl.when(b == 0)
    def _invert_permutation():
        def body(group, carry):
            for u in range(ROW_UNROLL):
                f = group * ROW_UNROLL + u
                inv_ref[sorted_idx_ref[f]] = f
            return carry

        lax.fori_loop(0, n_assign // ROW_UNROLL, body, 0)

    new_expert = jnp.logical_or(b == 0, blk_e_ref[b] != blk_e_ref[jnp.maximum(b - 1, 0)])
    xbufs = (xbuf0_ref, xbuf1_ref)

    def gather_rows(blk, parity, rows):
        row0 = blk_row0_ref[blk]
        for r in rows:
            flat = inv_ref[jnp.minimum(row0 + r, n_assign - 1)]
            _row_copy(x_hbm, flat & (n_tok - 1), xbufs[parity], r, gather_sem.at[parity]).start()

    def wait_rows(parity):
        pltpu.make_async_copy(x_hbm.at[pl.ds(0, bm)], xbufs[parity], gather_sem.at[parity]).wait()

    @pl.when(b == 0)
    def _first_rows():
        gather_rows(0, 0, range(bm))

    @pl.when(jnp.logical_and(used, new_expert))
    def _stage_weights():
        wu_bf_ref[...] = wu_ref[...].astype(BF16)
        row = lax.broadcasted_iota(jnp.int32, (LANES, LANES), 0)
        col = lax.broadcasted_iota(jnp.int32, (LANES, LANES), 1)
        select = (col == row // 2 + half * (row % 2)).astype(BF16)
        for g in range(n_groups):
            rows = slice(g * LANES, (g + 1) * LANES)
            wd_bf_ref[rows, :] = _dot(select, wd_ref[rows, :].astype(BF16)).astype(BF16)

    def compute(parity):
        wait_rows(parity)
        next_blk = jnp.minimum(b + 1, n_used - 1)
        per_group = bm // n_groups
        xb = xbufs[parity][...].astype(BF16)
        even = lax.broadcasted_iota(jnp.int32, (1, LANES), 1) % 2 == 0
        packed = []
        for g in range(n_groups):
            cols = slice(2 * g * LANES, (2 * g + 2) * LANES)
            h = _dot(xb, wu_bf_ref[:, cols]) + bu_ref[:, cols]
            lo = h[:, :LANES]
            hi = h[:, LANES:]
            x_glu = jnp.minimum(jnp.where(even, lo, pltpu.roll(hi, 1, 1)), SWIGLU_LIMIT)
            x_lin = jnp.clip(jnp.where(even, pltpu.roll(lo, LANES - 1, 1), hi), -SWIGLU_LIMIT, SWIGLU_LIMIT)
            packed.append((x_glu * jax.nn.sigmoid(SWIGLU_ALPHA * x_glu) * (x_lin + 1.0)).astype(BF16))
            gather_rows(next_blk, 1 - parity, range(g * per_group, (g + 1) * per_group))
        act = jnp.concatenate(packed, axis=1)
        o_ref[...] = _dot(act, wd_bf_ref[...]) + bd_ref[...]

        @pl.when(b == n_used - 1)
        def _drain():
            wait_rows(1 - parity)

    for parity in range(2):
        pl.when(jnp.logical_and(used, b % 2 == parity))(functools.partial(compute, parity))

    @pl.when(jnp.logical_not(used))
    def _unused():
        o_ref[...] = jnp.zeros_like(o_ref)


def _experts(x, sorted_idx_flat, blk_e, blk_row0, n_used, n_blk, layer, w_up, b_up, w_down, b_down):
    bm = MOE_BM
    t = x.shape[0]
    assert t & (t - 1) == 0

    def wspec(shape):
        return pl.BlockSpec((None, None) + shape, lambda b, s, e, r, n: (layer, e[b], 0, 0))

    return pl.pallas_call(
        functools.partial(_expert_kernel, n_tok=t),
        out_shape=jax.ShapeDtypeStruct((n_blk * bm, D_MODEL), F32),
        grid_spec=pltpu.PrefetchScalarGridSpec(
            num_scalar_prefetch=4,
            grid=(n_blk,),
            in_specs=[
                pl.BlockSpec(memory_space=pl.ANY),
                wspec((D_MODEL, 2 * D_FF)), wspec((1, 2 * D_FF)),
                wspec((D_FF, D_MODEL)), wspec((1, D_MODEL)),
            ],
            out_specs=pl.BlockSpec((bm, D_MODEL), lambda b, s, e, r, n: (b, 0)),
            scratch_shapes=[pltpu.VMEM((D_MODEL, 2 * D_FF), BF16), pltpu.VMEM((D_FF, D_MODEL), BF16),
                            pltpu.VMEM((bm, D_MODEL), F32), pltpu.VMEM((bm, D_MODEL), F32),
                            pltpu.SMEM(sorted_idx_flat.shape, jnp.int32), pltpu.SemaphoreType.DMA((2,))],
        ),
        compiler_params=_params(("arbitrary",)),
        name="moe_experts",
    )(sorted_idx_flat, blk_e, blk_row0, n_used, x, w_up, b_up, w_down, b_down)


def _combine_kernel(pos_ref, y_hbm, x_ref, gate_ref, g_ref, b_ref, o_ref, buf_ref, sem, *, n_tok, tm):
    i = pl.program_id(0)
    slot = i % 2

    def gather_tile(tile, buf_slot):
        base = tile * tm

        def issue(group, carry):
            j0 = pl.multiple_of(group * ROW_UNROLL, ROW_UNROLL)
            for u in range(ROW_UNROLL):
                for k in range(TOP_K):
                    _row_copy(y_hbm, pos_ref[k * n_tok + base + j0 + u], buf_ref.at[buf_slot, k], j0 + u,
                              sem.at[buf_slot]).start(priority=k % 2)
            return carry

        lax.fori_loop(0, tm // ROW_UNROLL, issue, 0)

    @pl.when(i == 0)
    def _first_tile():
        gather_tile(0, 0)

    @pl.when(i + 1 < pl.num_programs(0))
    def _next_tile():
        gather_tile(i + 1, 1 - slot)

    for k in range(TOP_K):
        pltpu.make_async_copy(y_hbm.at[pl.ds(0, tm)], buf_ref.at[slot, k], sem.at[slot]).wait()
    gates = gate_ref[...]
    ffn = gates[:, 0:1] * buf_ref[slot, 0]
    for k in range(1, TOP_K):
        ffn = ffn + gates[:, k:k + 1] * buf_ref[slot, k]
    z = DEEPNORM_ALPHA * x_ref[...] + ffn
    o_ref[...] = _layer_norm(z, g_ref[...], b_ref[...])


def _combine(x, y_slots, pos_flat, gates_tk, g, b):
    t = x.shape[0]
    tm = COMBINE_TM
    full = pl.BlockSpec((tm, D_MODEL), lambda i, p: (i, 0))
    vec = pl.BlockSpec((1, D_MODEL), lambda i, p: (0, 0))
    return pl.pallas_call(
        functools.partial(_combine_kernel, n_tok=t, tm=tm),
        out_shape=jax.ShapeDtypeStruct((t, D_MODEL), F32),
        grid_spec=pltpu.PrefetchScalarGridSpec(
            num_scalar_prefetch=1,
            grid=(t // tm,),
            in_specs=[pl.BlockSpec(memory_space=pl.ANY), full,
                      pl.BlockSpec((tm, TOP_K), lambda i, p: (i, 0)), vec, vec],
            out_specs=full,
            scratch_shapes=[pltpu.VMEM((2, TOP_K, tm, D_MODEL), F32), pltpu.SemaphoreType.DMA((2,))],
        ),
        compiler_params=_params(("arbitrary",)),
        name="moe_combine",
    )(pos_flat, y_slots, x, gates_tk, g.reshape(1, -1), b.reshape(1, -1))


def _moe_layer(x, layer, router_w, router_b, w_up, b_up, w_down, b_down, g, b):
    top_i, gates, rank, counts = _router(x, router_w, router_b)
    pos, sorted_idx, blk_e, blk_row0, n_used, n_blk = _slot_layout(top_i, rank, counts)
    pos_flat = pos.reshape(-1)
    y_slots = _experts(x, sorted_idx.reshape(-1), blk_e, blk_row0, n_used, n_blk, layer,
                       w_up, b_up[:, :, None, :], w_down, b_down[:, :, None, :])
    return _combine(x, y_slots, pos_flat, gates.T, g, b)


def kernel(x, ab_w_in, ab_w_out, ret_decay, c_w_in, c_w_out, c_rpb, ln_g, ln_b,
           router_w, router_b, exp_w_up, exp_b_up, exp_w_down, exp_b_down):
    b_, t_, d_ = x.shape
    assert b_ == 1 and d_ == D_MODEL
    xs = x.reshape(t_, d_)
    cos_t, sin_t = _rope_tables(t_)
    for layer in range(DEPTH):
        j = layer // 2
        if layer % 2 == 0:
            xs = _even_mixer(xs, ab_w_in[j], ab_w_out[j], ret_decay[j], ln_g[layer, 0], ln_b[layer, 0],
                             cos_t, sin_t)
        else:
            xs = _odd_mixer(xs, c_w_in[j], c_w_out[j], c_rpb[j], ln_g[layer, 0], ln_b[layer, 0])
        xs = _moe_layer(xs, layer, router_w[layer], router_b[layer], exp_w_up, exp_b_up,
                        exp_w_down, exp_b_down, ln_g[layer, 1], ln_b[layer, 1])
    return xs.reshape(b_, t_, d_)
```

```python
import functools

import jax
import jax.numpy as jnp
import numpy as np
from jax import lax
from jax.experimental import pallas as pl
from jax.experimental.pallas import tpu as pltpu

F32 = jnp.float32
BF16 = jnp.bfloat16

D_MODEL = 1024
HEAD_DIM = 64
LANES = 128
HEADS_PER_GROUP = LANES // HEAD_DIM
N_RET_HEADS = 8
N_DIL_HEADS = 8
N_NA_HEADS = 16
RET_W = N_RET_HEADS * HEAD_DIM
DIL_W = N_DIL_HEADS * HEAD_DIM
EVEN_IN = 5 * RET_W + 3 * DIL_W
ODD_IN = 3 * D_MODEL
ROPE_THETA = 10000.0
DIL_CONFIGS = ((128, 1), (512, 4), (2048, 16))
DIL_HALF = 64
GRID_W = 64
NA_KH = 8
NA_KW = 16
N_EXPERTS = 32
TOP_K = 4
D_FF = D_MODEL
SWIGLU_LIMIT = 7.0
SWIGLU_ALPHA = 1.702
LN_EPS = 1e-5
GN_EPS = 1e-6
NEG_INF = -1e30
DEPTH = 4
DEEPNORM_ALPHA = (2.0 * DEPTH) ** 0.25
QK_SCALE = HEAD_DIM ** -0.5

VMEM_LIMIT_BYTES = 56 * 1024 * 1024

PROJ_TM = 512
RET_CHUNK = 256
DIL_TQ = 128
DIL_TK = DIL_TQ + 2 * DIL_HALF
DIL_SUBTILES = 8
NA_ROWS = 2
NA_KROWS = NA_ROWS + NA_KH - 1
NA_SUBBLOCKS = 8
ROUTER_TM = 512
MOE_BM = 512
COMBINE_TM = 256
ROW_UNROLL = 8


def _params(sem, **kw):
    return pltpu.CompilerParams(dimension_semantics=sem, vmem_limit_bytes=VMEM_LIMIT_BYTES, **kw)


def _nt_dot(a, b):
    return lax.dot_general(a, b, (((1,), (1,)), ((), ())), preferred_element_type=F32)


def _dot(a, b):
    return jnp.dot(a, b, preferred_element_type=F32)


def _layer_norm(z, g, b):
    mu = jnp.mean(z, axis=-1, keepdims=True)
    d = z - mu
    var = jnp.mean(d * d, axis=-1, keepdims=True)
    return d * lax.rsqrt(var + LN_EPS) * g + b


def _first_head_lanes(shape):
    return lax.broadcasted_iota(jnp.int32, shape, len(shape) - 1) < HEAD_DIM


def _rope_tables(t):
    pos = jnp.arange(t, dtype=F32)
    inv = ROPE_THETA ** (-jnp.arange(0, HEAD_DIM, 2, dtype=F32) / HEAD_DIM)
    ang = pos[:, None] * inv[None, :]
    cos, sin = jnp.cos(ang), jnp.sin(ang)
    return (jnp.concatenate([cos, cos, cos, cos], axis=1),
            jnp.concatenate([-sin, sin, -sin, sin], axis=1))


def _rope(h, c, s):
    half = HEAD_DIM // 2
    lane = lax.broadcasted_iota(jnp.int32, h.shape, 1)
    first_half = (lane // half) % 2 == 0
    partner = jnp.where(first_half, pltpu.roll(h, LANES - half, 1), pltpu.roll(h, half, 1))
    return h * c + partner * s


def _proj_even_kernel(x_ref, w_ref, c_ref, s_ref, o_ref, *class_refs):
    tm = x_ref.shape[0]
    xb = x_ref[...].astype(BF16)
    c = c_ref[...]
    s = s_ref[...]
    rope_groups = {0: 1.0, 1: QK_SCALE, 5: QK_SCALE, 6: 1.0}
    for g in range(EVEN_IN // RET_W):
        h = _dot(xb, w_ref[:, g * RET_W:(g + 1) * RET_W])
        for j in range(RET_W // LANES):
            hj = h[:, j * LANES:(j + 1) * LANES]
            if g in rope_groups:
                hj = _rope(hj, c, s) * rope_groups[g]
            col = g * RET_W + j * LANES
            o_ref[:, col:col + LANES] = hj.astype(BF16)
    h_dil = o_ref[:, 5 * RET_W:]
    row = lax.broadcasted_iota(jnp.int32, (tm, tm), 0)
    col = lax.broadcasted_iota(jnp.int32, (tm, tm), 1)
    for cls_ref, (_, dil) in zip(class_refs, DIL_CONFIGS[1:]):
        per_class = tm // dil
        select = (col == dil * (row % per_class) + row // per_class).astype(BF16)
        regrouped = _dot(select, h_dil).astype(BF16)
        for r in range(dil):
            cls_ref[r] = regrouped[r * per_class:(r + 1) * per_class]


def _proj_even(x, w_bf, cos_t, sin_t):
    t = x.shape[0]
    tm = PROJ_TM
    dil_w = 3 * DIL_W
    class_shapes = [jax.ShapeDtypeStruct((dil, t // dil, dil_w), BF16) for _, dil in DIL_CONFIGS[1:]]
    class_specs = [pl.BlockSpec((dil, tm // dil, dil_w), lambda i: (0, i, 0)) for _, dil in DIL_CONFIGS[1:]]
    return pl.pallas_call(
        _proj_even_kernel,
        out_shape=[jax.ShapeDtypeStruct((t, EVEN_IN), BF16)] + class_shapes,
        grid=(t // tm,),
        in_specs=[
            pl.BlockSpec((tm, D_MODEL), lambda i: (i, 0)),
            pl.BlockSpec((D_MODEL, EVEN_IN), lambda i: (0, 0)),
            pl.BlockSpec((tm, LANES), lambda i: (i, 0)),
            pl.BlockSpec((tm, LANES), lambda i: (i, 0)),
        ],
        out_specs=[pl.BlockSpec((tm, EVEN_IN), lambda i: (i, 0))] + class_specs,
        compiler_params=_params(("parallel",)),
        name="proj_even",
    )(x, w_bf, cos_t, sin_t)


def _proj_odd_kernel(x_ref, w_ref, o_ref):
    xb = x_ref[...].astype(BF16)
    for g in range(3):
        h = _dot(xb, w_ref[:, g * D_MODEL:(g + 1) * D_MODEL])
        if g == 0:
            h = h * QK_SCALE
        o_ref[:, g * D_MODEL:(g + 1) * D_MODEL] = h.astype(BF16)


def _proj_odd(x, w_bf):
    t = x.shape[0]
    tm = PROJ_TM
    return pl.pallas_call(
        _proj_odd_kernel,
        out_shape=jax.ShapeDtypeStruct((t, ODD_IN), BF16),
        grid=(t // tm,),
        in_specs=[
            pl.BlockSpec((tm, D_MODEL), lambda i: (i, 0)),
            pl.BlockSpec((D_MODEL, ODD_IN), lambda i: (0, 0)),
        ],
        out_specs=pl.BlockSpec((tm, ODD_IN), lambda i: (i, 0)),
        compiler_params=_params(("parallel",)),
        name="proj_odd",
    )(x, w_bf)


def _ret_kernel(lg_ref, q_ref, k_ref, v_ref, g_ref, *rest, rev):
    if rev:
        o_ref, r_ref, d0_ref, d1_ref, z_ref, xi_ref, dc_ref = rest
        prev_ref = None
    else:
        prev_ref, o_ref, r_ref, d0_ref, d1_ref, z_ref, xi_ref, dc_ref = rest
    C = RET_CHUNK
    groups = RET_W // LANES
    c = pl.program_id(0)
    first = _first_head_lanes((1, LANES))
    direction = 1 if rev else 0

    @pl.when(c == 0)
    def _init():
        row = lax.broadcasted_iota(jnp.int32, (C, C), 0)
        col = lax.broadcasted_iota(jnp.int32, (C, C), 1)
        dist = (col - row) if rev else (row - col)
        keep = (dist > 0) if rev else (dist >= 0)
        distf = jnp.maximum(dist, 0).astype(F32)
        idx = lax.broadcasted_iota(jnp.int32, (C, LANES), 0).astype(F32)
        for p in range(groups):
            lg0 = lg_ref[direction, 2 * p]
            lg1 = lg_ref[direction, 2 * p + 1]
            d0_ref[p] = jnp.where(keep, jnp.exp(lg0 * distf), 0.0)
            d1_ref[p] = jnp.where(keep, jnp.exp(lg1 * distf), 0.0)
            lgl = jnp.where(first, lg0, lg1)
            z_ref[p] = jnp.exp(lgl * (idx if rev else (C - 1.0 - idx)))
            xi_ref[p] = jnp.exp(lgl * ((C - idx) if rev else (idx + 1.0)))
            dc_ref[p] = jnp.exp(lgl * float(C))
        r_ref[...] = jnp.zeros_like(r_ref)

    rr = lax.broadcasted_iota(jnp.int32, (LANES, LANES), 0) < HEAD_DIM
    cc = lax.broadcasted_iota(jnp.int32, (LANES, LANES), 1) < HEAD_DIM
    inv_n = 1.0 / HEAD_DIM
    for p in range(groups):
        lanes = slice(p * LANES, (p + 1) * LANES)
        q2 = q_ref[:, lanes]
        k2 = k_ref[:, lanes]
        v2 = v_ref[:, lanes]
        zero = jnp.zeros_like(q2)
        q0 = jnp.where(first, q2, zero)
        q1 = jnp.where(first, zero, q2)
        a0 = (_nt_dot(q0, k2) * d0_ref[p]).astype(BF16)
        a1 = (_nt_dot(q1, k2) * d1_ref[p]).astype(BF16)
        state = r_ref[p]
        cross = _dot(q2, state.astype(BF16)) * xi_ref[p]
        o = jnp.where(first, _dot(a0, v2), _dot(a1, v2)) + cross

        kz = (k2.astype(F32) * z_ref[p]).T.astype(BF16)
        r_ref[p] = jnp.where(rr == cc, state * dc_ref[p] + _dot(kz, v2), 0.0)

        s_first = jnp.sum(jnp.where(first, o, 0.0), axis=-1, keepdims=True)
        s_second = jnp.sum(jnp.where(first, 0.0, o), axis=-1, keepdims=True)
        d = o - jnp.where(first, s_first, s_second) * inv_n
        dd = d * d
        v_first = jnp.sum(jnp.where(first, dd, 0.0), axis=-1, keepdims=True)
        v_second = jnp.sum(jnp.where(first, 0.0, dd), axis=-1, keepdims=True)
        var = jnp.where(first, v_first, v_second) * inv_n
        gate = g_ref[:, lanes].astype(F32)
        y = gate * jax.nn.sigmoid(gate) * (d * lax.rsqrt(var + GN_EPS))
        if prev_ref is not None:
            y = y + prev_ref[:, lanes].astype(F32)
        o_ref[:, lanes] = y.astype(o_ref.dtype)


def _retention_pass(h, log_gamma, prev, rev):
    t = h.shape[0]
    C = RET_CHUNK
    nc = t // C
    groups = RET_W // LANES

    def col_spec(group):
        return pl.BlockSpec((C, RET_W), lambda c, group=group: ((nc - 1 - c) if rev else c, group))

    in_specs = [
        pl.BlockSpec(memory_space=pltpu.SMEM),
        col_spec(0), col_spec(1), col_spec(2), col_spec(4 if rev else 3),
    ]
    args = [log_gamma, h, h, h, h]
    if not rev:
        in_specs.append(col_spec(0))
        args.append(prev)
    return pl.pallas_call(
        functools.partial(_ret_kernel, rev=rev),
        out_shape=jax.ShapeDtypeStruct((t, RET_W), BF16),
        grid=(nc,),
        in_specs=in_specs,
        out_specs=col_spec(0),
        scratch_shapes=[
            pltpu.VMEM((groups, LANES, LANES), F32),
            pltpu.VMEM((groups, C, C), F32),
            pltpu.VMEM((groups, C, C), F32),
            pltpu.VMEM((groups, C, LANES), F32),
            pltpu.VMEM((groups, C, LANES), F32),
            pltpu.VMEM((groups, 1, LANES), F32),
        ],
        compiler_params=_params(("arbitrary",)),
        name="retention_bwd" if rev else "retention_fwd",
    )(*args)


def _retention(h, decay_logits):
    log_gamma = jax.nn.log_sigmoid(decay_logits.astype(F32))
    y_bwd = _retention_pass(h, log_gamma, None, rev=True)
    return _retention_pass(h, log_gamma, y_bwd, rev=False)


def _dil_kernel(q_ref, k_ref, v_ref, o_ref, lse_ref, *, length):
    first = _first_head_lanes((1, LANES))
    row = lax.broadcasted_iota(jnp.int32, (DIL_TQ, DIL_TK), 0)
    col = lax.broadcasted_iota(jnp.int32, (DIL_TQ, DIL_TK), 1)
    for sub in range(DIL_SUBTILES):
        i = pl.program_id(2) * DIL_SUBTILES + sub
        rows = slice(sub * DIL_TQ, (sub + 1) * DIL_TQ)
        start = jnp.clip(i * DIL_TQ - DIL_HALF, 0, length - DIL_TK)
        start = pl.multiple_of(start, DIL_HALF)
        kw = k_ref[pl.ds(start, DIL_TK), :]
        vw = v_ref[pl.ds(start, DIL_TK), :]
        valid = jnp.abs((start + col) - (i * DIL_TQ + row)) <= DIL_HALF
        q2 = q_ref[rows, :]
        zero = jnp.zeros_like(q2)

        def one_head(qh):
            s = jnp.where(valid, _nt_dot(qh, kw), NEG_INF)
            m = jnp.max(s, axis=-1, keepdims=True)
            e = jnp.exp(s - m)
            den = jnp.sum(e, axis=-1, keepdims=True)
            return _dot(e.astype(BF16), vw) / den, m + jnp.log(den)

        o0, l0 = one_head(jnp.where(first, q2, zero))
        o1, l1 = one_head(jnp.where(first, zero, q2))
        o_ref[rows, :] = jnp.where(first, o0, o1).astype(o_ref.dtype)
        lse_ref[rows, :] = jnp.where(first, l0, l1)


def _dilated_branch(h_cls, col_base, t):
    dil, length, _ = h_cls.shape
    groups = DIL_W // LANES
    tq = DIL_TQ * DIL_SUBTILES
    o, lse = pl.pallas_call(
        functools.partial(_dil_kernel, length=length),
        out_shape=(jax.ShapeDtypeStruct((length, dil * DIL_W), BF16),
                   jax.ShapeDtypeStruct((length, dil * DIL_W), F32)),
        grid=(dil, groups, length // tq),
        in_specs=[
            pl.BlockSpec((None, tq, LANES), lambda r, p, i: (r, i, col_base + p)),
            pl.BlockSpec((None, length, LANES), lambda r, p, i: (r, 0, col_base + groups + p)),
            pl.BlockSpec((None, length, LANES), lambda r, p, i: (r, 0, col_base + 2 * groups + p)),
        ],
        out_specs=(
            pl.BlockSpec((tq, LANES), lambda r, p, i: (i, r * groups + p)),
            pl.BlockSpec((tq, LANES), lambda r, p, i: (i, r * groups + p)),
        ),
        compiler_params=_params(("parallel", "parallel", "parallel")),
        name=f"dilated_attn_d{dil}",
    )(h_cls, h_cls, h_cls)
    return o.reshape(t, DIL_W), lse.reshape(t, DIL_W)


def _out_even_kernel(x_ref, yr_ref, o1_ref, o2_ref, o3_ref, l1_ref, l2_ref, l3_ref,
                     w_ref, g_ref, b_ref, out_ref):
    l1, l2, l3 = l1_ref[...], l2_ref[...], l3_ref[...]
    m = jnp.maximum(jnp.maximum(l1, l2), l3)
    e1, e2, e3 = jnp.exp(l1 - m), jnp.exp(l2 - m), jnp.exp(l3 - m)
    den = e1 + e2 + e3
    y_dil = (e1 / den) * o1_ref[...].astype(F32) + (e2 / den) * o2_ref[...].astype(F32) \
        + (e3 / den) * o3_ref[...].astype(F32)
    mix = _dot(yr_ref[...], w_ref[:RET_W, :]) + _dot(y_dil.astype(BF16), w_ref[RET_W:, :])
    z = DEEPNORM_ALPHA * x_ref[...] + mix
    out_ref[...] = _layer_norm(z, g_ref[...], b_ref[...])


def _out_even(x, y_ret, outs, lses, w_bf, g, b):
    t = x.shape[0]
    tm = PROJ_TM
    half = pl.BlockSpec((tm, RET_W), lambda i: (i, 0))
    vec = pl.BlockSpec((1, D_MODEL), lambda i: (0, 0))
    return pl.pallas_call(
        _out_even_kernel,
        out_shape=jax.ShapeDtypeStruct((t, D_MODEL), F32),
        grid=(t // tm,),
        in_specs=[pl.BlockSpec((tm, D_MODEL), lambda i: (i, 0))] + [half] * 7
        + [pl.BlockSpec((D_MODEL, D_MODEL), lambda i: (0, 0)), vec, vec],
        out_specs=pl.BlockSpec((tm, D_MODEL), lambda i: (i, 0)),
        compiler_params=_params(("parallel",)),
        name="out_proj_even",
    )(x, y_ret, *outs, *lses, w_bf, g.reshape(1, -1), b.reshape(1, -1))


def _out_odd_kernel(x_ref, y_ref, w_ref, g_ref, b_ref, out_ref):
    z = DEEPNORM_ALPHA * x_ref[...] + _dot(y_ref[...], w_ref[...])
    out_ref[...] = _layer_norm(z, g_ref[...], b_ref[...])


def _out_odd(x, y, w_bf, g, b):
    t = x.shape[0]
    tm = PROJ_TM
    full = pl.BlockSpec((tm, D_MODEL), lambda i: (i, 0))
    vec = pl.BlockSpec((1, D_MODEL), lambda i: (0, 0))
    return pl.pallas_call(
        _out_odd_kernel,
        out_shape=jax.ShapeDtypeStruct((t, D_MODEL), F32),
        grid=(t // tm,),
        in_specs=[full, full, pl.BlockSpec((D_MODEL, D_MODEL), lambda i: (0, 0)), vec, vec],
        out_specs=full,
        compiler_params=_params(("parallel",)),
        name="out_proj_odd",
    )(x, y, w_bf, g.reshape(1, -1), b.reshape(1, -1))


def _even_mixer(x, w_in, w_out, decay_logits, g, b, cos_t, sin_t):
    t = x.shape[0]
    h, *h_classes = _proj_even(x, w_in.astype(BF16), cos_t, sin_t)
    y_ret = _retention(h, decay_logits)
    outs, lses = [], []
    for h_cls, col_base in [(h.reshape(1, t, EVEN_IN), 5 * RET_W // LANES)] + [(hc, 0) for hc in h_classes]:
        o, lse = _dilated_branch(h_cls, col_base, t)
        outs.append(o)
        lses.append(lse)
    return _out_even(x, y_ret, outs, lses, w_out.astype(BF16), g, b)


def _na_bias_table(rpb, rows):
    n_blk = rows // NA_ROWS
    blocks = np.array([0, 1, 2, n_blk - 2, n_blk - 1])
    kstart = np.clip(NA_ROWS * blocks - NA_KH // 2, 0, rows - NA_KROWS)
    r = NA_ROWS * blocks[:, None] + np.arange(NA_ROWS)[None, :]
    r0 = np.clip(r - NA_KH // 2, 0, rows - NA_KH)
    kr = kstart[:, None] + np.arange(NA_KROWS)[None, :]
    row_ok = (kr[:, None, :] >= r0[:, :, None]) & (kr[:, None, :] < r0[:, :, None] + NA_KH)
    dr = kr[:, None, :] - r[:, :, None] + (NA_KH - 1)
    c = np.arange(GRID_W)
    c0 = np.clip(c - NA_KW // 2, 0, GRID_W - NA_KW)
    kc = np.arange(GRID_W)
    col_ok = (kc[None, :] >= c0[:, None]) & (kc[None, :] < c0[:, None] + NA_KW)
    pad = GRID_W - NA_KW
    rp = jnp.pad(rpb.astype(F32), ((0, 0), (0, 0), (pad, pad)))
    by_col = jnp.stack([rp[:, :, GRID_W - 1 - ci:2 * GRID_W - 1 - ci] for ci in range(GRID_W)], axis=2)
    by_col = jnp.where(col_ok[None, None], by_col, NEG_INF)
    outside = jnp.full((N_NA_HEADS, GRID_W, GRID_W), NEG_INF, F32)
    patterns = []
    for pi in range(5):
        q_rows = []
        for qi in range(NA_ROWS):
            blocks_ = [by_col[:, int(dr[pi, qi, ki])] if row_ok[pi, qi, ki] else outside
                       for ki in range(NA_KROWS)]
            q_rows.append(jnp.concatenate(blocks_, axis=-1))
        patterns.append(jnp.concatenate(q_rows, axis=-2))
    return jnp.stack(patterns, axis=0)


def _na_kernel(q_ref, k_ref, v_ref, bias_ref, o_ref, *, rows):
    n_blk = rows // NA_ROWS
    tq = NA_ROWS * GRID_W
    nk = NA_KROWS * GRID_W
    first = _first_head_lanes((1, LANES))
    for sub in range(NA_SUBBLOCKS):
        rb = pl.program_id(1) * NA_SUBBLOCKS + sub
        qrows = slice(sub * tq, (sub + 1) * tq)
        kstart = jnp.clip(NA_ROWS * rb - NA_KH // 2, 0, rows - NA_KROWS) * GRID_W
        kstart = pl.multiple_of(kstart, GRID_W)
        kw = k_ref[pl.ds(kstart, nk), :]
        vw = v_ref[pl.ds(kstart, nk), :]
        pattern = jnp.clip(rb, 0, 2) + jnp.clip(rb - (n_blk - 3), 0, 2)
        q2 = q_ref[qrows, :]
        zero = jnp.zeros_like(q2)

        def one_head(qh, bias):
            s = _nt_dot(qh, kw) + bias
            e = jnp.exp(s - jnp.max(s, axis=-1, keepdims=True))
            return _dot(e.astype(BF16), vw) / jnp.sum(e, axis=-1, keepdims=True)

        o0 = one_head(jnp.where(first, q2, zero), bias_ref[pattern, 0])
        o1 = one_head(jnp.where(first, zero, q2), bias_ref[pattern, 1])
        o_ref[qrows, :] = jnp.where(first, o0, o1).astype(o_ref.dtype)


def _neighbourhood_attention(h, rpb):
    t = h.shape[0]
    rows = t // GRID_W
    n_blk = rows // NA_ROWS
    groups = D_MODEL // LANES
    tq = NA_SUBBLOCKS * NA_ROWS * GRID_W
    bias = _na_bias_table(rpb, rows)
    return pl.pallas_call(
        functools.partial(_na_kernel, rows=rows),
        out_shape=jax.ShapeDtypeStruct((t, D_MODEL), BF16),
        grid=(groups, n_blk // NA_SUBBLOCKS),
        in_specs=[
            pl.BlockSpec((tq, LANES), lambda p, rb: (rb, p)),
            pl.BlockSpec((t, LANES), lambda p, rb: (0, groups + p)),
            pl.BlockSpec((t, LANES), lambda p, rb: (0, 2 * groups + p)),
            pl.BlockSpec((bias.shape[0], HEADS_PER_GROUP) + bias.shape[2:], lambda p, rb: (0, p, 0, 0)),
        ],
        out_specs=pl.BlockSpec((tq, LANES), lambda p, rb: (rb, p)),
        compiler_params=_params(("parallel", "parallel")),
        name="neighbourhood_attn",
    )(h, h, h, bias)


def _odd_mixer(x, w_in, w_out, rpb, g, b):
    h = _proj_odd(x, w_in.astype(BF16))
    y = _neighbourhood_attention(h, rpb)
    return _out_odd(x, y, w_out.astype(BF16), g, b)


def _router_kernel(x_ref, wt_ref, b_ref, idx_ref, gate_ref, rank_ref, count_ref):
    tm = x_ref.shape[0]

    @pl.when(pl.program_id(0) == 0)
    def _init():
        count_ref[...] = jnp.zeros_like(count_ref)

    logits = lax.dot_general(wt_ref[...], x_ref[...], (((1,), (1,)), ((), ())),
                             precision=lax.Precision.HIGHEST, preferred_element_type=F32)
    logits = logits + b_ref[...]
    expert = lax.broadcasted_iota(jnp.int32, logits.shape, 0)
    vals, idxs = [], []
    for _ in range(TOP_K):
        m = jnp.max(logits, axis=0, keepdims=True)
        sel = jnp.min(jnp.where(logits == m, expert, N_EXPERTS), axis=0, keepdims=True)
        vals.append(m)
        idxs.append(sel)
        logits = jnp.where(expert == sel, -jnp.inf, logits)
    exps = [jnp.exp(v - vals[0]) for v in vals]
    den = exps[0] + exps[1] + exps[2] + exps[3]
    for k in range(TOP_K):
        idx_ref[k:k + 1, :] = idxs[k]
        gate_ref[k:k + 1, :] = exps[k] / den

    onehots = [(expert == idxs[k]).astype(F32) for k in range(TOP_K)]
    per_tok = onehots[0] + onehots[1] + onehots[2] + onehots[3]
    earlier = lax.broadcasted_iota(jnp.int32, (tm, tm), 0) < lax.broadcasted_iota(jnp.int32, (tm, tm), 1)
    before = count_ref[...] + _dot(per_tok.astype(BF16), earlier.astype(BF16))
    for k in range(TOP_K):
        rank_ref[k:k + 1, :] = jnp.sum(onehots[k] * before, axis=0, keepdims=True).astype(jnp.int32)
    count_ref[...] += jnp.sum(per_tok, axis=1, keepdims=True)


def _router(x, router_w, router_b):
    t = x.shape[0]
    tm = ROUTER_TM
    row = pl.BlockSpec((TOP_K, tm), lambda i: (0, i))
    return pl.pallas_call(
        _router_kernel,
        out_shape=(jax.ShapeDtypeStruct((TOP_K, t), jnp.int32),
                   jax.ShapeDtypeStruct((TOP_K, t), F32),
                   jax.ShapeDtypeStruct((TOP_K, t), jnp.int32),
                   jax.ShapeDtypeStruct((N_EXPERTS, 1), F32)),
        grid=(t // tm,),
        in_specs=[
            pl.BlockSpec((tm, D_MODEL), lambda i: (i, 0)),
            pl.BlockSpec((N_EXPERTS, D_MODEL), lambda i: (0, 0)),
            pl.BlockSpec((N_EXPERTS, 1), lambda i: (0, 0)),
        ],
        out_specs=(row, row, row, pl.BlockSpec((N_EXPERTS, 1), lambda i: (0, 0))),
        compiler_params=_params(("arbitrary",)),
        name="moe_router",
    )(x, router_w.T, router_b.reshape(-1, 1))


def _slot_layout(top_i, rank, counts):
    t = top_i.shape[1]
    bm = MOE_BM
    experts = jnp.arange(N_EXPERTS, dtype=jnp.int32)
    counts = counts.reshape(-1).astype(jnp.int32)
    pcounts = (counts + bm - 1) // bm * bm
    pend = jnp.cumsum(pcounts)
    pstart = pend - pcounts
    pos = rank + jnp.sum(jnp.where(top_i[:, :, None] == experts, pstart, 0), axis=-1)
    n_blk = (TOP_K * t) // bm + N_EXPERTS
    blk_start = jnp.arange(n_blk, dtype=jnp.int32) * bm
    blk_e = jnp.minimum(jnp.sum((pend[None, :] <= blk_start[:, None]).astype(jnp.int32), axis=1),
                        N_EXPERTS - 1)
    n_used = (pend[-1] // bm).astype(jnp.int32).reshape(1)
    start = jnp.cumsum(counts) - counts
    sorted_idx = rank + jnp.sum(jnp.where(top_i[:, :, None] == experts, start, 0), axis=-1)
    blk_row0 = start[blk_e] + blk_start - pstart[blk_e]
    return pos.astype(jnp.int32), sorted_idx.astype(jnp.int32), blk_e, blk_row0.astype(jnp.int32), n_used, n_blk


def _row_copy(src, src_row, dst, dst_row, sem):
    return pltpu.make_async_copy(src.at[pl.ds(src_row, 1)], dst.at[pl.ds(dst_row, 1)], sem)


def _expert_kernel(sorted_idx_ref, blk_e_ref, blk_row0_ref, n_used_ref, x_hbm, wu_ref, bu_ref, wd_ref, bd_ref,
                   o_ref, wu_bf_ref, wd_bf_ref, xbuf0_ref, xbuf1_ref, inv_ref, gather_sem, *, n_tok):
    bm = MOE_BM
    b = pl.program_id(0)
    n_used = n_used_ref[0]
    used = b < n_used
    half = LANES // 2
    n_groups = D_FF // LANES
    n_assign = sorted_idx_ref.shape[0]

    @pl.when(b == 0)
    def _invert_permutation():
        def body(group, carry):
            for u in range(ROW_UNROLL):
                f = group * ROW_UNROLL + u
                inv_ref[sorted_idx_ref[f]] = f
            return carry

        lax.fori_loop(0, n_assign // ROW_UNROLL, body, 0)

    new_expert = jnp.logical_or(b == 0, blk_e_ref[b] != blk_e_ref[jnp.maximum(b - 1, 0)])
    xbufs = (xbuf0_ref, xbuf1_ref)

    def gather_rows(blk, parity, rows):
        row0 = blk_row0_ref[blk]
        for r in rows:
            flat = inv_ref[jnp.minimum(row0 + r, n_assign - 1)]
            _row_copy(x_hbm, flat & (n_tok - 1), xbufs[parity], r, gather_sem.at[parity]).start(priority=r % 2)

    def wait_rows(parity):
        pltpu.make_async_copy(x_hbm.at[pl.ds(0, bm)], xbufs[parity], gather_sem.at[parity]).wait()

    @pl.when(b == 0)
    def _first_rows():
        gather_rows(0, 0, range(bm))

    @pl.when(jnp.logical_and(used, new_expert))
    def _stage_weights():
        wu_bf_ref[...] = wu_ref[...].astype(BF16)
        row = lax.broadcasted_iota(jnp.int32, (LANES, LANES), 0)
        col = lax.broadcasted_iota(jnp.int32, (LANES, LANES), 1)
        select = (col == row // 2 + half * (row % 2)).astype(BF16)
        for g in range(n_groups):
            rows = slice(g * LANES, (g + 1) * LANES)
            wd_bf_ref[rows, :] = _dot(select, wd_ref[rows, :].astype(BF16)).astype(BF16)

    def compute(parity):
        wait_rows(parity)
        next_blk = jnp.minimum(b + 1, n_used - 1)
        per_group = bm // n_groups
        xb = xbufs[parity][...].astype(BF16)
        even = lax.broadcasted_iota(jnp.int32, (1, LANES), 1) % 2 == 0
        packed = []
        for g in range(n_groups):
            cols = slice(2 * g * LANES, (2 * g + 2) * LANES)
            h = _dot(xb, wu_bf_ref[:, cols]) + bu_ref[:, cols]
            lo = h[:, :LANES]
            hi = h[:, LANES:]
            x_glu = jnp.minimum(jnp.where(even, lo, pltpu.roll(hi, 1, 1)), SWIGLU_LIMIT)
            x_lin = jnp.clip(jnp.where(even, pltpu.roll(lo, LANES - 1, 1), hi), -SWIGLU_LIMIT, SWIGLU_LIMIT)
            packed.append((x_glu * jax.nn.sigmoid(SWIGLU_ALPHA * x_glu) * (x_lin + 1.0)).astype(BF16))
            gather_rows(next_blk, 1 - parity, range(g * per_group, (g + 1) * per_group))
        act = jnp.concatenate(packed, axis=1)
        o_ref[...] = _dot(act, wd_bf_ref[...]) + bd_ref[...]

        @pl.when(b == n_used - 1)
        def _drain():
            wait_rows(1 - parity)

    for parity in range(2):
        pl.when(jnp.logical_and(used, b % 2 == parity))(functools.partial(compute, parity))

    @pl.when(jnp.logical_not(used))
    def _unused():
        o_ref[...] = jnp.zeros_like(o_ref)


def _experts(x, sorted_idx_flat, blk_e, blk_row0, n_used, n_blk, layer, w_up, b_up, w_down, b_down):
    bm = MOE_BM
    t = x.shape[0]
    assert t & (t - 1) == 0

    def wspec(shape):
        return pl.BlockSpec((None, None) + shape, lambda b, s, e, r, n: (layer, e[b], 0, 0))

    return pl.pallas_call(
        functools.partial(_expert_kernel, n_tok=t),
        out_shape=jax.ShapeDtypeStruct((n_blk * bm, D_MODEL), F32),
        grid_spec=pltpu.PrefetchScalarGridSpec(
            num_scalar_prefetch=4,
            grid=(n_blk,),
            in_specs=[
                pl.BlockSpec(memory_space=pl.ANY),
                wspec((D_MODEL, 2 * D_FF)), wspec((1, 2 * D_FF)),
                wspec((D_FF, D_MODEL)), wspec((1, D_MODEL)),
            ],
            out_specs=pl.BlockSpec((bm, D_MODEL), lambda b, s, e, r, n: (b, 0)),
            scratch_shapes=[pltpu.VMEM((D_MODEL, 2 * D_FF), BF16), pltpu.VMEM((D_FF, D_MODEL), BF16),
                            pltpu.VMEM((bm, D_MODEL), F32), pltpu.VMEM((bm, D_MODEL), F32),
                            pltpu.SMEM(sorted_idx_flat.shape, jnp.int32), pltpu.SemaphoreType.DMA((2,))],
        ),
        compiler_params=_params(("arbitrary",)),
        name="moe_experts",
    )(sorted_idx_flat, blk_e, blk_row0, n_used, x, w_up, b_up, w_down, b_down)


def _combine_kernel(pos_ref, y_hbm, x_ref, gate_ref, g_ref, b_ref, o_ref, buf_ref, sem, *, n_tok, tm):
    i = pl.program_id(0)
    slot = i % 2

    def gather_tile(tile, buf_slot):
        base = tile * tm

        def issue(group, carry):
            j0 = pl.multiple_of(group * ROW_UNROLL, ROW_UNROLL)
            for u in range(ROW_UNROLL):
                for k in range(TOP_K):
                    _row_copy(y_hbm, pos_ref[k * n_tok + base + j0 + u], buf_ref.at[buf_slot, k], j0 + u,
                              sem.at[buf_slot]).start(priority=k % 2)
            return carry

        lax.fori_loop(0, tm // ROW_UNROLL, issue, 0)

    @pl.when(i == 0)
    def _first_tile():
        gather_tile(0, 0)

    @pl.when(i + 1 < pl.num_programs(0))
    def _next_tile():
        gather_tile(i + 1, 1 - slot)

    for k in range(TOP_K):
        pltpu.make_async_copy(y_hbm.at[pl.ds(0, tm)], buf_ref.at[slot, k], sem.at[slot]).wait()
    gates = gate_ref[...]
    ffn = gates[:, 0:1] * buf_ref[slot, 0]
    for k in range(1, TOP_K):
        ffn = ffn + gates[:, k:k + 1] * buf_ref[slot, k]
    z = DEEPNORM_ALPHA * x_ref[...] + ffn
    o_ref[...] = _layer_norm(z, g_ref[...], b_ref[...])


def _combine(x, y_slots, pos_flat, gates_tk, g, b):
    t = x.shape[0]
    tm = COMBINE_TM
    full = pl.BlockSpec((tm, D_MODEL), lambda i, p: (i, 0))
    vec = pl.BlockSpec((1, D_MODEL), lambda i, p: (0, 0))
    return pl.pallas_call(
        functools.partial(_combine_kernel, n_tok=t, tm=tm),
        out_shape=jax.ShapeDtypeStruct((t, D_MODEL), F32),
        grid_spec=pltpu.PrefetchScalarGridSpec(
            num_scalar_prefetch=1,
            grid=(t // tm,),
            in_specs=[pl.BlockSpec(memory_space=pl.ANY), full,
                      pl.BlockSpec((tm, TOP_K), lambda i, p: (i, 0)), vec, vec],
            out_specs=full,
            scratch_shapes=[pltpu.VMEM((2, TOP_K, tm, D_MODEL), F32), pltpu.SemaphoreType.DMA((2,))],
        ),
        compiler_params=_params(("arbitrary",)),
        name="moe_combine",
    )(pos_flat, y_slots, x, gates_tk, g.reshape(1, -1), b.reshape(1, -1))


def _moe_layer(x, layer, router_w, router_b, w_up, b_up, w_down, b_down, g, b):
    top_i, gates, rank, counts = _router(x, router_w, router_b)
    pos, sorted_idx, blk_e, blk_row0, n_used, n_blk = _slot_layout(top_i, rank, counts)
    pos_flat = pos.reshape(-1)
    y_slots = _experts(x, sorted_idx.reshape(-1), blk_e, blk_row0, n_used, n_blk, layer,
                       w_up, b_up[:, :, None, :], w_down, b_down[:, :, None, :])
    return _combine(x, y_slots, pos_flat, gates.T, g, b)


def kernel(x, ab_w_in, ab_w_out, ret_decay, c_w_in, c_w_out, c_rpb, ln_g, ln_b,
           router_w, router_b, exp_w_up, exp_b_up, exp_w_down, exp_b_down):
    b_, t_, d_ = x.shape
    assert b_ == 1 and d_ == D_MODEL
    xs = x.reshape(t_, d_)
    cos_t, sin_t = _rope_tables(t_)
    for layer in range(DEPTH):
        j = layer // 2
        if layer % 2 == 0:
            xs = _even_mixer(xs, ab_w_in[j], ab_w_out[j], ret_decay[j], ln_g[layer, 0], ln_b[layer, 0],
                             cos_t, sin_t)
        else:
            xs = _odd_mixer(xs, c_w_in[j], c_w_out[j], c_rpb[j], ln_g[layer, 0], ln_b[layer, 0])
        xs = _moe_layer(xs, layer, router_w[layer], router_b[layer], exp_w_up, exp_b_up,
                        exp_w_down, exp_b_down, ln_g[layer, 1], ln_b[layer, 1])
    return xs.reshape(b_, t_, d_)
```

```python
import functools

import jax
import jax.numpy as jnp
import numpy as np
from jax import lax
from jax.experimental import pallas as pl
from jax.experimental.pallas import tpu as pltpu

F32 = jnp.float32
BF16 = jnp.bfloat16

D_MODEL = 1024
HEAD_DIM = 64
LANES = 128
HEADS_PER_GROUP = LANES // HEAD_DIM
N_RET_HEADS = 8
N_DIL_HEADS = 8
N_NA_HEADS = 16
RET_W = N_RET_HEADS * HEAD_DIM
DIL_W = N_DIL_HEADS * HEAD_DIM
EVEN_IN = 5 * RET_W + 3 * DIL_W
ODD_IN = 3 * D_MODEL
ROPE_THETA = 10000.0
DIL_CONFIGS = ((128, 1), (512, 4), (2048, 16))
DIL_HALF = 64
GRID_W = 64
NA_KH = 8
NA_KW = 16
N_EXPERTS = 32
TOP_K = 4
D_FF = D_MODEL
SWIGLU_LIMIT = 7.0
SWIGLU_ALPHA = 1.702
LN_EPS = 1e-5
GN_EPS = 1e-6
NEG_INF = -1e30
DEPTH = 4
DEEPNORM_ALPHA = (2.0 * DEPTH) ** 0.25
QK_SCALE = HEAD_DIM ** -0.5

VMEM_LIMIT_BYTES = 56 * 1024 * 1024

PROJ_TM = 512
RET_CHUNK = 256
DIL_TQ = 128
DIL_TK = DIL_TQ + 2 * DIL_HALF
DIL_SUBTILES = 8
NA_ROWS = 2
NA_KROWS = NA_ROWS + NA_KH - 1
NA_SUBBLOCKS = 8
ROUTER_TM = 512
MOE_BM = 512
COMBINE_TM = 256
ROW_UNROLL = 8
ROW_BUFFERS = 3


def _params(sem, **kw):
    return pltpu.CompilerParams(dimension_semantics=sem, vmem_limit_bytes=VMEM_LIMIT_BYTES, **kw)


def _nt_dot(a, b):
    return lax.dot_general(a, b, (((1,), (1,)), ((), ())), preferred_element_type=F32)


def _dot(a, b):
    return jnp.dot(a, b, preferred_element_type=F32)


def _layer_norm(z, g, b):
    mu = jnp.mean(z, axis=-1, keepdims=True)
    d = z - mu
    var = jnp.mean(d * d, axis=-1, keepdims=True)
    return d * lax.rsqrt(var + LN_EPS) * g + b


def _first_head_lanes(shape):
    return lax.broadcasted_iota(jnp.int32, shape, len(shape) - 1) < HEAD_DIM


def _rope_tables(t):
    pos = jnp.arange(t, dtype=F32)
    inv = ROPE_THETA ** (-jnp.arange(0, HEAD_DIM, 2, dtype=F32) / HEAD_DIM)
    ang = pos[:, None] * inv[None, :]
    cos, sin = jnp.cos(ang), jnp.sin(ang)
    return (jnp.concatenate([cos, cos, cos, cos], axis=1),
            jnp.concatenate([-sin, sin, -sin, sin], axis=1))


def _rope(h, c, s):
    half = HEAD_DIM // 2
    lane = lax.broadcasted_iota(jnp.int32, h.shape, 1)
    first_half = (lane // half) % 2 == 0
    partner = jnp.where(first_half, pltpu.roll(h, LANES - half, 1), pltpu.roll(h, half, 1))
    return h * c + partner * s


def _proj_even_kernel(x_ref, w_ref, c_ref, s_ref, o_ref, *class_refs):
    tm = x_ref.shape[0]
    xb = x_ref[...].astype(BF16)
    c = c_ref[...]
    s = s_ref[...]
    rope_groups = {0: 1.0, 1: QK_SCALE, 5: QK_SCALE, 6: 1.0}
    for g in range(EVEN_IN // RET_W):
        h = _dot(xb, w_ref[:, g * RET_W:(g + 1) * RET_W])
        for j in range(RET_W // LANES):
            hj = h[:, j * LANES:(j + 1) * LANES]
            if g in rope_groups:
                hj = _rope(hj, c, s) * rope_groups[g]
            col = g * RET_W + j * LANES
            o_ref[:, col:col + LANES] = hj.astype(BF16)
    h_dil = o_ref[:, 5 * RET_W:]
    row = lax.broadcasted_iota(jnp.int32, (tm, tm), 0)
    col = lax.broadcasted_iota(jnp.int32, (tm, tm), 1)
    for cls_ref, (_, dil) in zip(class_refs, DIL_CONFIGS[1:]):
        per_class = tm // dil
        select = (col == dil * (row % per_class) + row // per_class).astype(BF16)
        regrouped = _dot(select, h_dil).astype(BF16)
        for r in range(dil):
            cls_ref[r] = regrouped[r * per_class:(r + 1) * per_class]


def _proj_even(x, w_bf, cos_t, sin_t):
    t = x.shape[0]
    tm = PROJ_TM
    dil_w = 3 * DIL_W
    class_shapes = [jax.ShapeDtypeStruct((dil, t // dil, dil_w), BF16) for _, dil in DIL_CONFIGS[1:]]
    class_specs = [pl.BlockSpec((dil, tm // dil, dil_w), lambda i: (0, i, 0)) for _, dil in DIL_CONFIGS[1:]]
    return pl.pallas_call(
        _proj_even_kernel,
        out_shape=[jax.ShapeDtypeStruct((t, EVEN_IN), BF16)] + class_shapes,
        grid=(t // tm,),
        in_specs=[
            pl.BlockSpec((tm, D_MODEL), lambda i: (i, 0)),
            pl.BlockSpec((D_MODEL, EVEN_IN), lambda i: (0, 0)),
            pl.BlockSpec((tm, LANES), lambda i: (i, 0)),
            pl.BlockSpec((tm, LANES), lambda i: (i, 0)),
        ],
        out_specs=[pl.BlockSpec((tm, EVEN_IN), lambda i: (i, 0))] + class_specs,
        compiler_params=_params(("parallel",)),
        name="proj_even",
    )(x, w_bf, cos_t, sin_t)


def _proj_odd_kernel(x_ref, w_ref, o_ref):
    xb = x_ref[...].astype(BF16)
    for g in range(3):
        h = _dot(xb, w_ref[:, g * D_MODEL:(g + 1) * D_MODEL])
        if g == 0:
            h = h * QK_SCALE
        o_ref[:, g * D_MODEL:(g + 1) * D_MODEL] = h.astype(BF16)


def _proj_odd(x, w_bf):
    t = x.shape[0]
    tm = PROJ_TM
    return pl.pallas_call(
        _proj_odd_kernel,
        out_shape=jax.ShapeDtypeStruct((t, ODD_IN), BF16),
        grid=(t // tm,),
        in_specs=[
            pl.BlockSpec((tm, D_MODEL), lambda i: (i, 0)),
            pl.BlockSpec((D_MODEL, ODD_IN), lambda i: (0, 0)),
        ],
        out_specs=pl.BlockSpec((tm, ODD_IN), lambda i: (i, 0)),
        compiler_params=_params(("parallel",)),
        name="proj_odd",
    )(x, w_bf)


def _ret_kernel(lg_ref, q_ref, k_ref, v_ref, g_ref, *rest, rev):
    if rev:
        o_ref, r_ref, d0_ref, d1_ref, z_ref, xi_ref, dc_ref = rest
        prev_ref = None
    else:
        prev_ref, o_ref, r_ref, d0_ref, d1_ref, z_ref, xi_ref, dc_ref = rest
    C = RET_CHUNK
    groups = RET_W // LANES
    c = pl.program_id(0)
    first = _first_head_lanes((1, LANES))
    direction = 1 if rev else 0

    @pl.when(c == 0)
    def _init():
        row = lax.broadcasted_iota(jnp.int32, (C, C), 0)
        col = lax.broadcasted_iota(jnp.int32, (C, C), 1)
        dist = (col - row) if rev else (row - col)
        keep = (dist > 0) if rev else (dist >= 0)
        distf = jnp.maximum(dist, 0).astype(F32)
        idx = lax.broadcasted_iota(jnp.int32, (C, LANES), 0).astype(F32)
        for p in range(groups):
            lg0 = lg_ref[direction, 2 * p]
            lg1 = lg_ref[direction, 2 * p + 1]
            d0_ref[p] = jnp.where(keep, jnp.exp(lg0 * distf), 0.0)
            d1_ref[p] = jnp.where(keep, jnp.exp(lg1 * distf), 0.0)
            lgl = jnp.where(first, lg0, lg1)
            z_ref[p] = jnp.exp(lgl * (idx if rev else (C - 1.0 - idx)))
            xi_ref[p] = jnp.exp(lgl * ((C - idx) if rev else (idx + 1.0)))
            dc_ref[p] = jnp.exp(lgl * float(C))
        r_ref[...] = jnp.zeros_like(r_ref)

    rr = lax.broadcasted_iota(jnp.int32, (LANES, LANES), 0) < HEAD_DIM
    cc = lax.broadcasted_iota(jnp.int32, (LANES, LANES), 1) < HEAD_DIM
    inv_n = 1.0 / HEAD_DIM
    for p in range(groups):
        lanes = slice(p * LANES, (p + 1) * LANES)
        q2 = q_ref[:, lanes]
        k2 = k_ref[:, lanes]
        v2 = v_ref[:, lanes]
        zero = jnp.zeros_like(q2)
        q0 = jnp.where(first, q2, zero)
        q1 = jnp.where(first, zero, q2)
        a0 = (_nt_dot(q0, k2) * d0_ref[p]).astype(BF16)
        a1 = (_nt_dot(q1, k2) * d1_ref[p]).astype(BF16)
        state = r_ref[p]
        cross = _dot(q2, state.astype(BF16)) * xi_ref[p]
        o = jnp.where(first, _dot(a0, v2), _dot(a1, v2)) + cross

        kz = (k2.astype(F32) * z_ref[p]).T.astype(BF16)
        r_ref[p] = jnp.where(rr == cc, state * dc_ref[p] + _dot(kz, v2), 0.0)

        s_first = jnp.sum(jnp.where(first, o, 0.0), axis=-1, keepdims=True)
        s_second = jnp.sum(jnp.where(first, 0.0, o), axis=-1, keepdims=True)
        d = o - jnp.where(first, s_first, s_second) * inv_n
        dd = d * d
        v_first = jnp.sum(jnp.where(first, dd, 0.0), axis=-1, keepdims=True)
        v_second = jnp.sum(jnp.where(first, 0.0, dd), axis=-1, keepdims=True)
        var = jnp.where(first, v_first, v_second) * inv_n
        gate = g_ref[:, lanes].astype(F32)
        y = gate * jax.nn.sigmoid(gate) * (d * lax.rsqrt(var + GN_EPS))
        if prev_ref is not None:
            y = y + prev_ref[:, lanes].astype(F32)
        o_ref[:, lanes] = y.astype(o_ref.dtype)


def _retention_pass(h, log_gamma, prev, rev):
    t = h.shape[0]
    C = RET_CHUNK
    nc = t // C
    groups = RET_W // LANES

    def col_spec(group):
        return pl.BlockSpec((C, RET_W), lambda c, group=group: ((nc - 1 - c) if rev else c, group))

    in_specs = [
        pl.BlockSpec(memory_space=pltpu.SMEM),
        col_spec(0), col_spec(1), col_spec(2), col_spec(4 if rev else 3),
    ]
    args = [log_gamma, h, h, h, h]
    if not rev:
        in_specs.append(col_spec(0))
        args.append(prev)
    return pl.pallas_call(
        functools.partial(_ret_kernel, rev=rev),
        out_shape=jax.ShapeDtypeStruct((t, RET_W), BF16),
        grid=(nc,),
        in_specs=in_specs,
        out_specs=col_spec(0),
        scratch_shapes=[
            pltpu.VMEM((groups, LANES, LANES), F32),
            pltpu.VMEM((groups, C, C), F32),
            pltpu.VMEM((groups, C, C), F32),
            pltpu.VMEM((groups, C, LANES), F32),
            pltpu.VMEM((groups, C, LANES), F32),
            pltpu.VMEM((groups, 1, LANES), F32),
        ],
        compiler_params=_params(("arbitrary",)),
        name="retention_bwd" if rev else "retention_fwd",
    )(*args)


def _retention(h, decay_logits):
    log_gamma = jax.nn.log_sigmoid(decay_logits.astype(F32))
    y_bwd = _retention_pass(h, log_gamma, None, rev=True)
    return _retention_pass(h, log_gamma, y_bwd, rev=False)


def _dil_kernel(q_ref, k_ref, v_ref, o_ref, lse_ref, *, length):
    first = _first_head_lanes((1, LANES))
    row = lax.broadcasted_iota(jnp.int32, (DIL_TQ, DIL_TK), 0)
    col = lax.broadcasted_iota(jnp.int32, (DIL_TQ, DIL_TK), 1)
    for sub in range(DIL_SUBTILES):
        i = pl.program_id(2) * DIL_SUBTILES + sub
        rows = slice(sub * DIL_TQ, (sub + 1) * DIL_TQ)
        start = jnp.clip(i * DIL_TQ - DIL_HALF, 0, length - DIL_TK)
        start = pl.multiple_of(start, DIL_HALF)
        kw = k_ref[pl.ds(start, DIL_TK), :]
        vw = v_ref[pl.ds(start, DIL_TK), :]
        valid = jnp.abs((start + col) - (i * DIL_TQ + row)) <= DIL_HALF
        q2 = q_ref[rows, :]
        zero = jnp.zeros_like(q2)

        def one_head(qh):
            s = jnp.where(valid, _nt_dot(qh, kw), NEG_INF)
            m = jnp.max(s, axis=-1, keepdims=True)
            e = jnp.exp(s - m)
            den = jnp.sum(e, axis=-1, keepdims=True)
            return _dot(e.astype(BF16), vw) / den, m + jnp.log(den)

        o0, l0 = one_head(jnp.where(first, q2, zero))
        o1, l1 = one_head(jnp.where(first, zero, q2))
        o_ref[rows, :] = jnp.where(first, o0, o1).astype(o_ref.dtype)
        lse_ref[rows, :] = jnp.where(first, l0, l1)


def _dilated_branch(h_cls, col_base, t):
    dil, length, _ = h_cls.shape
    groups = DIL_W // LANES
    tq = DIL_TQ * DIL_SUBTILES
    o, lse = pl.pallas_call(
        functools.partial(_dil_kernel, length=length),
        out_shape=(jax.ShapeDtypeStruct((length, dil * DIL_W), BF16),
                   jax.ShapeDtypeStruct((length, dil * DIL_W), F32)),
        grid=(dil, groups, length // tq),
        in_specs=[
            pl.BlockSpec((None, tq, LANES), lambda r, p, i: (r, i, col_base + p)),
            pl.BlockSpec((None, length, LANES), lambda r, p, i: (r, 0, col_base + groups + p)),
            pl.BlockSpec((None, length, LANES), lambda r, p, i: (r, 0, col_base + 2 * groups + p)),
        ],
        out_specs=(
            pl.BlockSpec((tq, LANES), lambda r, p, i: (i, r * groups + p)),
            pl.BlockSpec((tq, LANES), lambda r, p, i: (i, r * groups + p)),
        ),
        compiler_params=_params(("parallel", "parallel", "parallel")),
        name=f"dilated_attn_d{dil}",
    )(h_cls, h_cls, h_cls)
    return o.reshape(t, DIL_W), lse.reshape(t, DIL_W)


def _out_even_kernel(x_ref, yr_ref, o1_ref, o2_ref, o3_ref, l1_ref, l2_ref, l3_ref,
                     w_ref, g_ref, b_ref, out_ref):
    l1, l2, l3 = l1_ref[...], l2_ref[...], l3_ref[...]
    m = jnp.maximum(jnp.maximum(l1, l2), l3)
    e1, e2, e3 = jnp.exp(l1 - m), jnp.exp(l2 - m), jnp.exp(l3 - m)
    den = e1 + e2 + e3
    y_dil = (e1 / den) * o1_ref[...].astype(F32) + (e2 / den) * o2_ref[...].astype(F32) \
        + (e3 / den) * o3_ref[...].astype(F32)
    mix = _dot(yr_ref[...], w_ref[:RET_W, :]) + _dot(y_dil.astype(BF16), w_ref[RET_W:, :])
    z = DEEPNORM_ALPHA * x_ref[...] + mix
    out_ref[...] = _layer_norm(z, g_ref[...], b_ref[...])


def _out_even(x, y_ret, outs, lses, w_bf, g, b):
    t = x.shape[0]
    tm = PROJ_TM
    half = pl.BlockSpec((tm, RET_W), lambda i: (i, 0))
    vec = pl.BlockSpec((1, D_MODEL), lambda i: (0, 0))
    return pl.pallas_call(
        _out_even_kernel,
        out_shape=jax.ShapeDtypeStruct((t, D_MODEL), F32),
        grid=(t // tm,),
        in_specs=[pl.BlockSpec((tm, D_MODEL), lambda i: (i, 0))] + [half] * 7
        + [pl.BlockSpec((D_MODEL, D_MODEL), lambda i: (0, 0)), vec, vec],
        out_specs=pl.BlockSpec((tm, D_MODEL), lambda i: (i, 0)),
        compiler_params=_params(("parallel",)),
        name="out_proj_even",
    )(x, y_ret, *outs, *lses, w_bf, g.reshape(1, -1), b.reshape(1, -1))


def _out_odd_kernel(x_ref, y_ref, w_ref, g_ref, b_ref, out_ref):
    z = DEEPNORM_ALPHA * x_ref[...] + _dot(y_ref[...], w_ref[...])
    out_ref[...] = _layer_norm(z, g_ref[...], b_ref[...])


def _out_odd(x, y, w_bf, g, b):
    t = x.shape[0]
    tm = PROJ_TM
    full = pl.BlockSpec((tm, D_MODEL), lambda i: (i, 0))
    vec = pl.BlockSpec((1, D_MODEL), lambda i: (0, 0))
    return pl.pallas_call(
        _out_odd_kernel,
        out_shape=jax.ShapeDtypeStruct((t, D_MODEL), F32),
        grid=(t // tm,),
        in_specs=[full, full, pl.BlockSpec((D_MODEL, D_MODEL), lambda i: (0, 0)), vec, vec],
        out_specs=full,
        compiler_params=_params(("parallel",)),
        name="out_proj_odd",
    )(x, y, w_bf, g.reshape(1, -1), b.reshape(1, -1))


def _even_mixer(x, w_in, w_out, decay_logits, g, b, cos_t, sin_t):
    t = x.shape[0]
    h, *h_classes = _proj_even(x, w_in.astype(BF16), cos_t, sin_t)
    y_ret = _retention(h, decay_logits)
    outs, lses = [], []
    for h_cls, col_base in [(h.reshape(1, t, EVEN_IN), 5 * RET_W // LANES)] + [(hc, 0) for hc in h_classes]:
        o, lse = _dilated_branch(h_cls, col_base, t)
        outs.append(o)
        lses.append(lse)
    return _out_even(x, y_ret, outs, lses, w_out.astype(BF16), g, b)


def _na_bias_table(rpb, rows):
    n_blk = rows // NA_ROWS
    blocks = np.array([0, 1, 2, n_blk - 2, n_blk - 1])
    kstart = np.clip(NA_ROWS * blocks - NA_KH // 2, 0, rows - NA_KROWS)
    r = NA_ROWS * blocks[:, None] + np.arange(NA_ROWS)[None, :]
    r0 = np.clip(r - NA_KH // 2, 0, rows - NA_KH)
    kr = kstart[:, None] + np.arange(NA_KROWS)[None, :]
    row_ok = (kr[:, None, :] >= r0[:, :, None]) & (kr[:, None, :] < r0[:, :, None] + NA_KH)
    dr = kr[:, None, :] - r[:, :, None] + (NA_KH - 1)
    c = np.arange(GRID_W)
    c0 = np.clip(c - NA_KW // 2, 0, GRID_W - NA_KW)
    kc = np.arange(GRID_W)
    col_ok = (kc[None, :] >= c0[:, None]) & (kc[None, :] < c0[:, None] + NA_KW)
    pad = GRID_W - NA_KW
    rp = jnp.pad(rpb.astype(F32), ((0, 0), (0, 0), (pad, pad)))
    by_col = jnp.stack([rp[:, :, GRID_W - 1 - ci:2 * GRID_W - 1 - ci] for ci in range(GRID_W)], axis=2)
    by_col = jnp.where(col_ok[None, None], by_col, NEG_INF)
    outside = jnp.full((N_NA_HEADS, GRID_W, GRID_W), NEG_INF, F32)
    patterns = []
    for pi in range(5):
        q_rows = []
        for qi in range(NA_ROWS):
            blocks_ = [by_col[:, int(dr[pi, qi, ki])] if row_ok[pi, qi, ki] else outside
                       for ki in range(NA_KROWS)]
            q_rows.append(jnp.concatenate(blocks_, axis=-1))
        patterns.append(jnp.concatenate(q_rows, axis=-2))
    return jnp.stack(patterns, axis=0)


def _na_kernel(q_ref, k_ref, v_ref, bias_ref, o_ref, *, rows):
    n_blk = rows // NA_ROWS
    tq = NA_ROWS * GRID_W
    nk = NA_KROWS * GRID_W
    first = _first_head_lanes((1, LANES))
    for sub in range(NA_SUBBLOCKS):
        rb = pl.program_id(1) * NA_SUBBLOCKS + sub
        qrows = slice(sub * tq, (sub + 1) * tq)
        kstart = jnp.clip(NA_ROWS * rb - NA_KH // 2, 0, rows - NA_KROWS) * GRID_W
        kstart = pl.multiple_of(kstart, GRID_W)
        kw = k_ref[pl.ds(kstart, nk), :]
        vw = v_ref[pl.ds(kstart, nk), :]
        pattern = jnp.clip(rb, 0, 2) + jnp.clip(rb - (n_blk - 3), 0, 2)
        q2 = q_ref[qrows, :]
        zero = jnp.zeros_like(q2)

        def one_head(qh, bias):
            s = _nt_dot(qh, kw) + bias
            e = jnp.exp(s - jnp.max(s, axis=-1, keepdims=True))
            return _dot(e.astype(BF16), vw) / jnp.sum(e, axis=-1, keepdims=True)

        o0 = one_head(jnp.where(first, q2, zero), bias_ref[pattern, 0])
        o1 = one_head(jnp.where(first, zero, q2), bias_ref[pattern, 1])
        o_ref[qrows, :] = jnp.where(first, o0, o1).astype(o_ref.dtype)


def _neighbourhood_attention(h, rpb):
    t = h.shape[0]
    rows = t // GRID_W
    n_blk = rows // NA_ROWS
    groups = D_MODEL // LANES
    tq = NA_SUBBLOCKS * NA_ROWS * GRID_W
    bias = _na_bias_table(rpb, rows)
    return pl.pallas_call(
        functools.partial(_na_kernel, rows=rows),
        out_shape=jax.ShapeDtypeStruct((t, D_MODEL), BF16),
        grid=(groups, n_blk // NA_SUBBLOCKS),
        in_specs=[
            pl.BlockSpec((tq, LANES), lambda p, rb: (rb, p)),
            pl.BlockSpec((t, LANES), lambda p, rb: (0, groups + p)),
            pl.BlockSpec((t, LANES), lambda p, rb: (0, 2 * groups + p)),
            pl.BlockSpec((bias.shape[0], HEADS_PER_GROUP) + bias.shape[2:], lambda p, rb: (0, p, 0, 0)),
        ],
        out_specs=pl.BlockSpec((tq, LANES), lambda p, rb: (rb, p)),
        compiler_params=_params(("parallel", "parallel")),
        name="neighbourhood_attn",
    )(h, h, h, bias)


def _odd_mixer(x, w_in, w_out, rpb, g, b):
    h = _proj_odd(x, w_in.astype(BF16))
    y = _neighbourhood_attention(h, rpb)
    return _out_odd(x, y, w_out.astype(BF16), g, b)


def _router_kernel(x_ref, wt_ref, b_ref, idx_ref, gate_ref, rank_ref, count_ref):
    tm = x_ref.shape[0]

    @pl.when(pl.program_id(0) == 0)
    def _init():
        count_ref[...] = jnp.zeros_like(count_ref)

    logits = lax.dot_general(wt_ref[...], x_ref[...], (((1,), (1,)), ((), ())),
                             precision=lax.Precision.HIGHEST, preferred_element_type=F32)
    logits = logits + b_ref[...]
    expert = lax.broadcasted_iota(jnp.int32, logits.shape, 0)
    vals, idxs = [], []
    for _ in range(TOP_K):
        m = jnp.max(logits, axis=0, keepdims=True)
        sel = jnp.min(jnp.where(logits == m, expert, N_EXPERTS), axis=0, keepdims=True)
        vals.append(m)
        idxs.append(sel)
        logits = jnp.where(expert == sel, -jnp.inf, logits)
    exps = [jnp.exp(v - vals[0]) for v in vals]
    den = exps[0] + exps[1] + exps[2] + exps[3]
    for k in range(TOP_K):
        idx_ref[k:k + 1, :] = idxs[k]
        gate_ref[k:k + 1, :] = exps[k] / den

    onehots = [(expert == idxs[k]).astype(F32) for k in range(TOP_K)]
    per_tok = onehots[0] + onehots[1] + onehots[2] + onehots[3]
    earlier = lax.broadcasted_iota(jnp.int32, (tm, tm), 0) < lax.broadcasted_iota(jnp.int32, (tm, tm), 1)
    before = count_ref[...] + _dot(per_tok.astype(BF16), earlier.astype(BF16))
    for k in range(TOP_K):
        rank_ref[k:k + 1, :] = jnp.sum(onehots[k] * before, axis=0, keepdims=True).astype(jnp.int32)
    count_ref[...] += jnp.sum(per_tok, axis=1, keepdims=True)


def _router(x, router_w, router_b):
    t = x.shape[0]
    tm = ROUTER_TM
    row = pl.BlockSpec((TOP_K, tm), lambda i: (0, i))
    return pl.pallas_call(
        _router_kernel,
        out_shape=(jax.ShapeDtypeStruct((TOP_K, t), jnp.int32),
                   jax.ShapeDtypeStruct((TOP_K, t), F32),
                   jax.ShapeDtypeStruct((TOP_K, t), jnp.int32),
                   jax.ShapeDtypeStruct((N_EXPERTS, 1), F32)),
        grid=(t // tm,),
        in_specs=[
            pl.BlockSpec((tm, D_MODEL), lambda i: (i, 0)),
            pl.BlockSpec((N_EXPERTS, D_MODEL), lambda i: (0, 0)),
            pl.BlockSpec((N_EXPERTS, 1), lambda i: (0, 0)),
        ],
        out_specs=(row, row, row, pl.BlockSpec((N_EXPERTS, 1), lambda i: (0, 0))),
        compiler_params=_params(("arbitrary",)),
        name="moe_router",
    )(x, router_w.T, router_b.reshape(-1, 1))


def _slot_layout(top_i, rank, counts):
    t = top_i.shape[1]
    bm = MOE_BM
    experts = jnp.arange(N_EXPERTS, dtype=jnp.int32)
    counts = counts.reshape(-1).astype(jnp.int32)
    pcounts = (counts + bm - 1) // bm * bm
    pend = jnp.cumsum(pcounts)
    pstart = pend - pcounts
    pos = rank + jnp.sum(jnp.where(top_i[:, :, None] == experts, pstart, 0), axis=-1)
    n_blk = (TOP_K * t) // bm + N_EXPERTS
    blk_start = jnp.arange(n_blk, dtype=jnp.int32) * bm
    blk_e = jnp.minimum(jnp.sum((pend[None, :] <= blk_start[:, None]).astype(jnp.int32), axis=1),
                        N_EXPERTS - 1)
    n_used = (pend[-1] // bm).astype(jnp.int32).reshape(1)
    start = jnp.cumsum(counts) - counts
    sorted_idx = rank + jnp.sum(jnp.where(top_i[:, :, None] == experts, start, 0), axis=-1)
    blk_row0 = start[blk_e] + blk_start - pstart[blk_e]
    return pos.astype(jnp.int32), sorted_idx.astype(jnp.int32), blk_e, blk_row0.astype(jnp.int32), n_used, n_blk


def _row_copy(src, src_row, dst, dst_row, sem):
    return pltpu.make_async_copy(src.at[pl.ds(src_row, 1)], dst.at[pl.ds(dst_row, 1)], sem)


def _expert_kernel(sorted_idx_ref, blk_e_ref, blk_row0_ref, n_used_ref, x_hbm, wu_ref, bu_ref, wd_ref, bd_ref,
                   o_ref, wu_bf_ref, wd_bf_ref, *rest, n_tok):
    xbufs = rest[:ROW_BUFFERS]
    inv_ref, gather_sem = rest[ROW_BUFFERS:]
    bm = MOE_BM
    b = pl.program_id(0)
    n_used = n_used_ref[0]
    used = b < n_used
    half = LANES // 2
    n_groups = D_FF // LANES
    n_assign = sorted_idx_ref.shape[0]

    @pl.when(b == 0)
    def _invert_permutation():
        def body(group, carry):
            for u in range(ROW_UNROLL):
                f = group * ROW_UNROLL + u
                inv_ref[sorted_idx_ref[f]] = f
            return carry

        lax.fori_loop(0, n_assign // ROW_UNROLL, body, 0)

    new_expert = jnp.logical_or(b == 0, blk_e_ref[b] != blk_e_ref[jnp.maximum(b - 1, 0)])

    def gather_rows(blk, parity, rows):
        row0 = blk_row0_ref[blk]
        for r in rows:
            flat = inv_ref[jnp.minimum(row0 + r, n_assign - 1)]
            _row_copy(x_hbm, flat & (n_tok - 1), xbufs[parity], r, gather_sem.at[parity]).start(priority=r % 2)

    def wait_rows(parity):
        pltpu.make_async_copy(x_hbm.at[pl.ds(0, bm)], xbufs[parity], gather_sem.at[parity]).wait()

    @pl.when(b == 0)
    def _first_rows():
        for ahead in range(ROW_BUFFERS - 1):
            gather_rows(jnp.minimum(ahead, n_used - 1), ahead, range(bm))

    @pl.when(jnp.logical_and(used, new_expert))
    def _stage_weights():
        wu_bf_ref[...] = wu_ref[...].astype(BF16)
        row = lax.broadcasted_iota(jnp.int32, (LANES, LANES), 0)
        col = lax.broadcasted_iota(jnp.int32, (LANES, LANES), 1)
        select = (col == row // 2 + half * (row % 2)).astype(BF16)
        for g in range(n_groups):
            rows = slice(g * LANES, (g + 1) * LANES)
            wd_bf_ref[rows, :] = _dot(select, wd_ref[rows, :].astype(BF16)).astype(BF16)

    def compute(parity):
        wait_rows(parity)
        next_blk = jnp.minimum(b + ROW_BUFFERS - 1, n_used - 1)
        next_parity = (parity + ROW_BUFFERS - 1) % ROW_BUFFERS
        per_group = bm // n_groups
        xb = xbufs[parity][...].astype(BF16)
        even = lax.broadcasted_iota(jnp.int32, (1, LANES), 1) % 2 == 0
        packed = []
        for g in range(n_groups):
            cols = slice(2 * g * LANES, (2 * g + 2) * LANES)
            h = _dot(xb, wu_bf_ref[:, cols]) + bu_ref[:, cols]
            lo = h[:, :LANES]
            hi = h[:, LANES:]
            x_glu = jnp.minimum(jnp.where(even, lo, pltpu.roll(hi, 1, 1)), SWIGLU_LIMIT)
            x_lin = jnp.clip(jnp.where(even, pltpu.roll(lo, LANES - 1, 1), hi), -SWIGLU_LIMIT, SWIGLU_LIMIT)
            packed.append((x_glu * jax.nn.sigmoid(SWIGLU_ALPHA * x_glu) * (x_lin + 1.0)).astype(BF16))
            gather_rows(next_blk, next_parity, range(g * per_group, (g + 1) * per_group))
        act = jnp.concatenate(packed, axis=1)
        o_ref[...] = _dot(act, wd_bf_ref[...]) + bd_ref[...]

        @pl.when(b == n_used - 1)
        def _drain():
            for other in range(1, ROW_BUFFERS):
                wait_rows((parity + other) % ROW_BUFFERS)

    for parity in range(ROW_BUFFERS):
        pl.when(jnp.logical_and(used, b % ROW_BUFFERS == parity))(functools.partial(compute, parity))

    @pl.when(jnp.logical_not(used))
    def _unused():
        o_ref[...] = jnp.zeros_like(o_ref)


def _experts(x, sorted_idx_flat, blk_e, blk_row0, n_used, n_blk, layer, w_up, b_up, w_down, b_down):
    bm = MOE_BM
    t = x.shape[0]
    assert t & (t - 1) == 0

    def wspec(shape):
        return pl.BlockSpec((None, None) + shape, lambda b, s, e, r, n: (layer, e[b], 0, 0))

    return pl.pallas_call(
        functools.partial(_expert_kernel, n_tok=t),
        out_shape=jax.ShapeDtypeStruct((n_blk * bm, D_MODEL), F32),
        grid_spec=pltpu.PrefetchScalarGridSpec(
            num_scalar_prefetch=4,
            grid=(n_blk,),
            in_specs=[
                pl.BlockSpec(memory_space=pl.ANY),
                wspec((D_MODEL, 2 * D_FF)), wspec((1, 2 * D_FF)),
                wspec((D_FF, D_MODEL)), wspec((1, D_MODEL)),
            ],
            out_specs=pl.BlockSpec((bm, D_MODEL), lambda b, s, e, r, n: (b, 0)),
            scratch_shapes=[pltpu.VMEM((D_MODEL, 2 * D_FF), BF16), pltpu.VMEM((D_FF, D_MODEL), BF16)]
            + [pltpu.VMEM((bm, D_MODEL), F32)] * ROW_BUFFERS
            + [pltpu.SMEM(sorted_idx_flat.shape, jnp.int32), pltpu.SemaphoreType.DMA((ROW_BUFFERS,))],
        ),
        compiler_params=_params(("arbitrary",)),
        name="moe_experts",
    )(sorted_idx_flat, blk_e, blk_row0, n_used, x, w_up, b_up, w_down, b_down)


def _combine_kernel(pos_ref, y_hbm, x_ref, gate_ref, g_ref, b_ref, o_ref, buf_ref, sem, *, n_tok, tm):
    i = pl.program_id(0)
    slot = i % 2

    def gather_tile(tile, buf_slot):
        base = tile * tm

        def issue(group, carry):
            j0 = pl.multiple_of(group * ROW_UNROLL, ROW_UNROLL)
            for u in range(ROW_UNROLL):
                for k in range(TOP_K):
                    _row_copy(y_hbm, pos_ref[k * n_tok + base + j0 + u], buf_ref.at[buf_slot, k], j0 + u,
                              sem.at[buf_slot]).start(priority=k % 2)
            return carry

        lax.fori_loop(0, tm // ROW_UNROLL, issue, 0)

    @pl.when(i == 0)
    def _first_tile():
        gather_tile(0, 0)

    @pl.when(i + 1 < pl.num_programs(0))
    def _next_tile():
        gather_tile(i + 1, 1 - slot)

    for k in range(TOP_K):
        pltpu.make_async_copy(y_hbm.at[pl.ds(0, tm)], buf_ref.at[slot, k], sem.at[slot]).wait()
    gates = gate_ref[...]
    ffn = gates[:, 0:1] * buf_ref[slot, 0]
    for k in range(1, TOP_K):
        ffn = ffn + gates[:, k:k + 1] * buf_ref[slot, k]
    z = DEEPNORM_ALPHA * x_ref[...] + ffn
    o_ref[...] = _layer_norm(z, g_ref[...], b_ref[...])


def _combine(x, y_slots, pos_flat, gates_tk, g, b):
    t = x.shape[0]
    tm = COMBINE_TM
    full = pl.BlockSpec((tm, D_MODEL), lambda i, p: (i, 0))
    vec = pl.BlockSpec((1, D_MODEL), lambda i, p: (0, 0))
    return pl.pallas_call(
        functools.partial(_combine_kernel, n_tok=t, tm=tm),
        out_shape=jax.ShapeDtypeStruct((t, D_MODEL), F32),
        grid_spec=pltpu.PrefetchScalarGridSpec(
            num_scalar_prefetch=1,
            grid=(t // tm,),
            in_specs=[pl.BlockSpec(memory_space=pl.ANY), full,
                      pl.BlockSpec((tm, TOP_K), lambda i, p: (i, 0)), vec, vec],
            out_specs=full,
            scratch_shapes=[pltpu.VMEM((2, TOP_K, tm, D_MODEL), F32), pltpu.SemaphoreType.DMA((2,))],
        ),
        compiler_params=_params(("arbitrary",)),
        name="moe_combine",
    )(pos_flat, y_slots, x, gates_tk, g.reshape(1, -1), b.reshape(1, -1))


def _moe_layer(x, layer, router_w, router_b, w_up, b_up, w_down, b_down, g, b):
    top_i, gates, rank, counts = _router(x, router_w, router_b)
    pos, sorted_idx, blk_e, blk_row0, n_used, n_blk = _slot_layout(top_i, rank, counts)
    pos_flat = pos.reshape(-1)
    y_slots = _experts(x, sorted_idx.reshape(-1), blk_e, blk_row0, n_used, n_blk, layer,
                       w_up, b_up[:, :, None, :], w_down, b_down[:, :, None, :])
    return _combine(x, y_slots, pos_flat, gates.T, g, b)


def kernel(x, ab_w_in, ab_w_out, ret_decay, c_w_in, c_w_out, c_rpb, ln_g, ln_b,
           router_w, router_b, exp_w_up, exp_b_up, exp_w_down, exp_b_down):
    b_, t_, d_ = x.shape
    assert b_ == 1 and d_ == D_MODEL
    xs = x.reshape(t_, d_)
    cos_t, sin_t = _rope_tables(t_)
    for layer in range(DEPTH):
        j = layer // 2
        if layer % 2 == 0:
            xs = _even_mixer(xs, ab_w_in[j], ab_w_out[j], ret_decay[j], ln_g[layer, 0], ln_b[layer, 0],
                             cos_t, sin_t)
        else:
            xs = _odd_mixer(xs, c_w_in[j], c_w_out[j], c_rpb[j], ln_g[layer, 0], ln_b[layer, 0])
        xs = _moe_layer(xs, layer, router_w[layer], router_b[layer], exp_w_up, exp_b_up,
                        exp_w_down, exp_b_down, ln_g[layer, 1], ln_b[layer, 1])
    return xs.reshape(b_, t_, d_)
```

```python
import functools

import jax
import jax.numpy as jnp
import numpy as np
from jax import lax
from jax.experimental import pallas as pl
from jax.experimental.pallas import tpu as pltpu

F32 = jnp.float32
BF16 = jnp.bfloat16

D_MODEL = 1024
HEAD_DIM = 64
LANES = 128
HEADS_PER_GROUP = LANES // HEAD_DIM
N_RET_HEADS = 8
N_DIL_HEADS = 8
N_NA_HEADS = 16
RET_W = N_RET_HEADS * HEAD_DIM
DIL_W = N_DIL_HEADS * HEAD_DIM
EVEN_IN = 5 * RET_W + 3 * DIL_W
ODD_IN = 3 * D_MODEL
ROPE_THETA = 10000.0
DIL_CONFIGS = ((128, 1), (512, 4), (2048, 16))
DIL_HALF = 64
GRID_W = 64
NA_KH = 8
NA_KW = 16
N_EXPERTS = 32
TOP_K = 4
D_FF = D_MODEL
SWIGLU_LIMIT = 7.0
SWIGLU_ALPHA = 1.702
LN_EPS = 1e-5
GN_EPS = 1e-6
NEG_INF = -1e30
DEPTH = 4
DEEPNORM_ALPHA = (2.0 * DEPTH) ** 0.25
QK_SCALE = HEAD_DIM ** -0.5

VMEM_LIMIT_BYTES = 56 * 1024 * 1024

PROJ_TM = 512
RET_CHUNK = 256
DIL_TQ = 128
DIL_TK = DIL_TQ + 2 * DIL_HALF
DIL_SUBTILES = 8
NA_ROWS = 2
NA_KROWS = NA_ROWS + NA_KH - 1
NA_SUBBLOCKS = 8
ROUTER_TM = 512
MOE_BM = 512
COMBINE_TM = 512
ROW_UNROLL = 8
ROW_BUFFERS = 3


def _params(sem, **kw):
    return pltpu.CompilerParams(dimension_semantics=sem, vmem_limit_bytes=VMEM_LIMIT_BYTES, **kw)


def _nt_dot(a, b):
    return lax.dot_general(a, b, (((1,), (1,)), ((), ())), preferred_element_type=F32)


def _dot(a, b):
    return jnp.dot(a, b, preferred_element_type=F32)


def _layer_norm(z, g, b):
    mu = jnp.mean(z, axis=-1, keepdims=True)
    d = z - mu
    var = jnp.mean(d * d, axis=-1, keepdims=True)
    return d * lax.rsqrt(var + LN_EPS) * g + b


def _first_head_lanes(shape):
    return lax.broadcasted_iota(jnp.int32, shape, len(shape) - 1) < HEAD_DIM


def _rope_tables(t):
    pos = jnp.arange(t, dtype=F32)
    inv = ROPE_THETA ** (-jnp.arange(0, HEAD_DIM, 2, dtype=F32) / HEAD_DIM)
    ang = pos[:, None] * inv[None, :]
    cos, sin = jnp.cos(ang), jnp.sin(ang)
    return (jnp.concatenate([cos, cos, cos, cos], axis=1),
            jnp.concatenate([-sin, sin, -sin, sin], axis=1))


def _rope(h, c, s):
    half = HEAD_DIM // 2
    lane = lax.broadcasted_iota(jnp.int32, h.shape, 1)
    first_half = (lane // half) % 2 == 0
    partner = jnp.where(first_half, pltpu.roll(h, LANES - half, 1), pltpu.roll(h, half, 1))
    return h * c + partner * s


def _proj_even_kernel(x_ref, w_ref, c_ref, s_ref, o_ref, *class_refs):
    tm = x_ref.shape[0]
    xb = x_ref[...].astype(BF16)
    c = c_ref[...]
    s = s_ref[...]
    rope_groups = {0: 1.0, 1: QK_SCALE, 5: QK_SCALE, 6: 1.0}
    for g in range(EVEN_IN // RET_W):
        h = _dot(xb, w_ref[:, g * RET_W:(g + 1) * RET_W])
        for j in range(RET_W // LANES):
            hj = h[:, j * LANES:(j + 1) * LANES]
            if g in rope_groups:
                hj = _rope(hj, c, s) * rope_groups[g]
            col = g * RET_W + j * LANES
            o_ref[:, col:col + LANES] = hj.astype(BF16)
    h_dil = o_ref[:, 5 * RET_W:]
    row = lax.broadcasted_iota(jnp.int32, (tm, tm), 0)
    col = lax.broadcasted_iota(jnp.int32, (tm, tm), 1)
    for cls_ref, (_, dil) in zip(class_refs, DIL_CONFIGS[1:]):
        per_class = tm // dil
        select = (col == dil * (row % per_class) + row // per_class).astype(BF16)
        regrouped = _dot(select, h_dil).astype(BF16)
        for r in range(dil):
            cls_ref[r] = regrouped[r * per_class:(r + 1) * per_class]


def _proj_even(x, w_bf, cos_t, sin_t):
    t = x.shape[0]
    tm = PROJ_TM
    dil_w = 3 * DIL_W
    class_shapes = [jax.ShapeDtypeStruct((dil, t // dil, dil_w), BF16) for _, dil in DIL_CONFIGS[1:]]
    class_specs = [pl.BlockSpec((dil, tm // dil, dil_w), lambda i: (0, i, 0)) for _, dil in DIL_CONFIGS[1:]]
    return pl.pallas_call(
        _proj_even_kernel,
        out_shape=[jax.ShapeDtypeStruct((t, EVEN_IN), BF16)] + class_shapes,
        grid=(t // tm,),
        in_specs=[
            pl.BlockSpec((tm, D_MODEL), lambda i: (i, 0)),
            pl.BlockSpec((D_MODEL, EVEN_IN), lambda i: (0, 0)),
            pl.BlockSpec((tm, LANES), lambda i: (i, 0)),
            pl.BlockSpec((tm, LANES), lambda i: (i, 0)),
        ],
        out_specs=[pl.BlockSpec((tm, EVEN_IN), lambda i: (i, 0))] + class_specs,
        compiler_params=_params(("parallel",)),
        name="proj_even",
    )(x, w_bf, cos_t, sin_t)


def _proj_odd_kernel(x_ref, w_ref, o_ref):
    xb = x_ref[...].astype(BF16)
    for g in range(3):
        h = _dot(xb, w_ref[:, g * D_MODEL:(g + 1) * D_MODEL])
        if g == 0:
            h = h * QK_SCALE
        o_ref[:, g * D_MODEL:(g + 1) * D_MODEL] = h.astype(BF16)


def _proj_odd(x, w_bf):
    t = x.shape[0]
    tm = PROJ_TM
    return pl.pallas_call(
        _proj_odd_kernel,
        out_shape=jax.ShapeDtypeStruct((t, ODD_IN), BF16),
        grid=(t // tm,),
        in_specs=[
            pl.BlockSpec((tm, D_MODEL), lambda i: (i, 0)),
            pl.BlockSpec((D_MODEL, ODD_IN), lambda i: (0, 0)),
        ],
        out_specs=pl.BlockSpec((tm, ODD_IN), lambda i: (i, 0)),
        compiler_params=_params(("parallel",)),
        name="proj_odd",
    )(x, w_bf)


def _ret_kernel(lg_ref, q_ref, k_ref, v_ref, g_ref, *rest, rev):
    if rev:
        o_ref, r_ref, d0_ref, d1_ref, z_ref, xi_ref, dc_ref = rest
        prev_ref = None
    else:
        prev_ref, o_ref, r_ref, d0_ref, d1_ref, z_ref, xi_ref, dc_ref = rest
    C = RET_CHUNK
    groups = RET_W // LANES
    c = pl.program_id(0)
    first = _first_head_lanes((1, LANES))
    direction = 1 if rev else 0

    @pl.when(c == 0)
    def _init():
        row = lax.broadcasted_iota(jnp.int32, (C, C), 0)
        col = lax.broadcasted_iota(jnp.int32, (C, C), 1)
        dist = (col - row) if rev else (row - col)
        keep = (dist > 0) if rev else (dist >= 0)
        distf = jnp.maximum(dist, 0).astype(F32)
        idx = lax.broadcasted_iota(jnp.int32, (C, LANES), 0).astype(F32)
        for p in range(groups):
            lg0 = lg_ref[direction, 2 * p]
            lg1 = lg_ref[direction, 2 * p + 1]
            d0_ref[p] = jnp.where(keep, jnp.exp(lg0 * distf), 0.0)
            d1_ref[p] = jnp.where(keep, jnp.exp(lg1 * distf), 0.0)
            lgl = jnp.where(first, lg0, lg1)
            z_ref[p] = jnp.exp(lgl * (idx if rev else (C - 1.0 - idx)))
            xi_ref[p] = jnp.exp(lgl * ((C - idx) if rev else (idx + 1.0)))
            dc_ref[p] = jnp.exp(lgl * float(C))
        r_ref[...] = jnp.zeros_like(r_ref)

    rr = lax.broadcasted_iota(jnp.int32, (LANES, LANES), 0) < HEAD_DIM
    cc = lax.broadcasted_iota(jnp.int32, (LANES, LANES), 1) < HEAD_DIM
    inv_n = 1.0 / HEAD_DIM
    for p in range(groups):
        lanes = slice(p * LANES, (p + 1) * LANES)
        q2 = q_ref[:, lanes]
        k2 = k_ref[:, lanes]
        v2 = v_ref[:, lanes]
        zero = jnp.zeros_like(q2)
        q0 = jnp.where(first, q2, zero)
        q1 = jnp.where(first, zero, q2)
        a0 = (_nt_dot(q0, k2) * d0_ref[p]).astype(BF16)
        a1 = (_nt_dot(q1, k2) * d1_ref[p]).astype(BF16)
        state = r_ref[p]
        cross = _dot(q2, state.astype(BF16)) * xi_ref[p]
        o = jnp.where(first, _dot(a0, v2), _dot(a1, v2)) + cross

        kz = (k2.astype(F32) * z_ref[p]).T.astype(BF16)
        r_ref[p] = jnp.where(rr == cc, state * dc_ref[p] + _dot(kz, v2), 0.0)

        s_first = jnp.sum(jnp.where(first, o, 0.0), axis=-1, keepdims=True)
        s_second = jnp.sum(jnp.where(first, 0.0, o), axis=-1, keepdims=True)
        d = o - jnp.where(first, s_first, s_second) * inv_n
        dd = d * d
        v_first = jnp.sum(jnp.where(first, dd, 0.0), axis=-1, keepdims=True)
        v_second = jnp.sum(jnp.where(first, 0.0, dd), axis=-1, keepdims=True)
        var = jnp.where(first, v_first, v_second) * inv_n
        gate = g_ref[:, lanes].astype(F32)
        y = gate * jax.nn.sigmoid(gate) * (d * lax.rsqrt(var + GN_EPS))
        if prev_ref is not None:
            y = y + prev_ref[:, lanes].astype(F32)
        o_ref[:, lanes] = y.astype(o_ref.dtype)


def _retention_pass(h, log_gamma, prev, rev):
    t = h.shape[0]
    C = RET_CHUNK
    nc = t // C
    groups = RET_W // LANES

    def col_spec(group):
        return pl.BlockSpec((C, RET_W), lambda c, group=group: ((nc - 1 - c) if rev else c, group))

    in_specs = [
        pl.BlockSpec(memory_space=pltpu.SMEM),
        col_spec(0), col_spec(1), col_spec(2), col_spec(4 if rev else 3),
    ]
    args = [log_gamma, h, h, h, h]
    if not rev:
        in_specs.append(col_spec(0))
        args.append(prev)
    return pl.pallas_call(
        functools.partial(_ret_kernel, rev=rev),
        out_shape=jax.ShapeDtypeStruct((t, RET_W), BF16),
        grid=(nc,),
        in_specs=in_specs,
        out_specs=col_spec(0),
        scratch_shapes=[
            pltpu.VMEM((groups, LANES, LANES), F32),
            pltpu.VMEM((groups, C, C), F32),
            pltpu.VMEM((groups, C, C), F32),
            pltpu.VMEM((groups, C, LANES), F32),
            pltpu.VMEM((groups, C, LANES), F32),
            pltpu.VMEM((groups, 1, LANES), F32),
        ],
        compiler_params=_params(("arbitrary",)),
        name="retention_bwd" if rev else "retention_fwd",
    )(*args)


def _retention(h, decay_logits):
    log_gamma = jax.nn.log_sigmoid(decay_logits.astype(F32))
    y_bwd = _retention_pass(h, log_gamma, None, rev=True)
    return _retention_pass(h, log_gamma, y_bwd, rev=False)


def _dil_kernel(q_ref, k_ref, v_ref, o_ref, lse_ref, *, length):
    first = _first_head_lanes((1, LANES))
    row = lax.broadcasted_iota(jnp.int32, (DIL_TQ, DIL_TK), 0)
    col = lax.broadcasted_iota(jnp.int32, (DIL_TQ, DIL_TK), 1)
    for sub in range(DIL_SUBTILES):
        i = pl.program_id(2) * DIL_SUBTILES + sub
        rows = slice(sub * DIL_TQ, (sub + 1) * DIL_TQ)
        start = jnp.clip(i * DIL_TQ - DIL_HALF, 0, length - DIL_TK)
        start = pl.multiple_of(start, DIL_HALF)
        kw = k_ref[pl.ds(start, DIL_TK), :]
        vw = v_ref[pl.ds(start, DIL_TK), :]
        valid = jnp.abs((start + col) - (i * DIL_TQ + row)) <= DIL_HALF
        q2 = q_ref[rows, :]
        zero = jnp.zeros_like(q2)

        def one_head(qh):
            s = jnp.where(valid, _nt_dot(qh, kw), NEG_INF)
            m = jnp.max(s, axis=-1, keepdims=True)
            e = jnp.exp(s - m)
            den = jnp.sum(e, axis=-1, keepdims=True)
            return _dot(e.astype(BF16), vw) / den, m + jnp.log(den)

        o0, l0 = one_head(jnp.where(first, q2, zero))
        o1, l1 = one_head(jnp.where(first, zero, q2))
        o_ref[rows, :] = jnp.where(first, o0, o1).astype(o_ref.dtype)
        lse_ref[rows, :] = jnp.where(first, l0, l1)


def _dilated_branch(h_cls, col_base, t):
    dil, length, _ = h_cls.shape
    groups = DIL_W // LANES
    tq = DIL_TQ * DIL_SUBTILES
    o, lse = pl.pallas_call(
        functools.partial(_dil_kernel, length=length),
        out_shape=(jax.ShapeDtypeStruct((length, dil * DIL_W), BF16),
                   jax.ShapeDtypeStruct((length, dil * DIL_W), F32)),
        grid=(dil, groups, length // tq),
        in_specs=[
            pl.BlockSpec((None, tq, LANES), lambda r, p, i: (r, i, col_base + p)),
            pl.BlockSpec((None, length, LANES), lambda r, p, i: (r, 0, col_base + groups + p)),
            pl.BlockSpec((None, length, LANES), lambda r, p, i: (r, 0, col_base + 2 * groups + p)),
        ],
        out_specs=(
            pl.BlockSpec((tq, LANES), lambda r, p, i: (i, r * groups + p)),
            pl.BlockSpec((tq, LANES), lambda r, p, i: (i, r * groups + p)),
        ),
        compiler_params=_params(("parallel", "parallel", "parallel")),
        name=f"dilated_attn_d{dil}",
    )(h_cls, h_cls, h_cls)
    return o.reshape(t, DIL_W), lse.reshape(t, DIL_W)


def _out_even_kernel(x_ref, yr_ref, o1_ref, o2_ref, o3_ref, l1_ref, l2_ref, l3_ref,
                     w_ref, g_ref, b_ref, out_ref):
    l1, l2, l3 = l1_ref[...], l2_ref[...], l3_ref[...]
    m = jnp.maximum(jnp.maximum(l1, l2), l3)
    e1, e2, e3 = jnp.exp(l1 - m), jnp.exp(l2 - m), jnp.exp(l3 - m)
    den = e1 + e2 + e3
    y_dil = (e1 / den) * o1_ref[...].astype(F32) + (e2 / den) * o2_ref[...].astype(F32) \
        + (e3 / den) * o3_ref[...].astype(F32)
    mix = _dot(yr_ref[...], w_ref[:RET_W, :]) + _dot(y_dil.astype(BF16), w_ref[RET_W:, :])
    z = DEEPNORM_ALPHA * x_ref[...] + mix
    out_ref[...] = _layer_norm(z, g_ref[...], b_ref[...])


def _out_even(x, y_ret, outs, lses, w_bf, g, b):
    t = x.shape[0]
    tm = PROJ_TM
    half = pl.BlockSpec((tm, RET_W), lambda i: (i, 0))
    vec = pl.BlockSpec((1, D_MODEL), lambda i: (0, 0))
    return pl.pallas_call(
        _out_even_kernel,
        out_shape=jax.ShapeDtypeStruct((t, D_MODEL), F32),
        grid=(t // tm,),
        in_specs=[pl.BlockSpec((tm, D_MODEL), lambda i: (i, 0))] + [half] * 7
        + [pl.BlockSpec((D_MODEL, D_MODEL), lambda i: (0, 0)), vec, vec],
        out_specs=pl.BlockSpec((tm, D_MODEL), lambda i: (i, 0)),
        compiler_params=_params(("parallel",)),
        name="out_proj_even",
    )(x, y_ret, *outs, *lses, w_bf, g.reshape(1, -1), b.reshape(1, -1))


def _out_odd_kernel(x_ref, y_ref, w_ref, g_ref, b_ref, out_ref):
    z = DEEPNORM_ALPHA * x_ref[...] + _dot(y_ref[...], w_ref[...])
    out_ref[...] = _layer_norm(z, g_ref[...], b_ref[...])


def _out_odd(x, y, w_bf, g, b):
    t = x.shape[0]
    tm = PROJ_TM
    full = pl.BlockSpec((tm, D_MODEL), lambda i: (i, 0))
    vec = pl.BlockSpec((1, D_MODEL), lambda i: (0, 0))
    return pl.pallas_call(
        _out_odd_kernel,
        out_shape=jax.ShapeDtypeStruct((t, D_MODEL), F32),
        grid=(t // tm,),
        in_specs=[full, full, pl.BlockSpec((D_MODEL, D_MODEL), lambda i: (0, 0)), vec, vec],
        out_specs=full,
        compiler_params=_params(("parallel",)),
        name="out_proj_odd",
    )(x, y, w_bf, g.reshape(1, -1), b.reshape(1, -1))


def _even_mixer(x, w_in, w_out, decay_logits, g, b, cos_t, sin_t):
    t = x.shape[0]
    h, *h_classes = _proj_even(x, w_in.astype(BF16), cos_t, sin_t)
    y_ret = _retention(h, decay_logits)
    outs, lses = [], []
    for h_cls, col_base in [(h.reshape(1, t, EVEN_IN), 5 * RET_W // LANES)] + [(hc, 0) for hc in h_classes]:
        o, lse = _dilated_branch(h_cls, col_base, t)
        outs.append(o)
        lses.append(lse)
    return _out_even(x, y_ret, outs, lses, w_out.astype(BF16), g, b)


def _na_bias_table(rpb, rows):
    n_blk = rows // NA_ROWS
    blocks = np.array([0, 1, 2, n_blk - 2, n_blk - 1])
    kstart = np.clip(NA_ROWS * blocks - NA_KH // 2, 0, rows - NA_KROWS)
    r = NA_ROWS * blocks[:, None] + np.arange(NA_ROWS)[None, :]
    r0 = np.clip(r - NA_KH // 2, 0, rows - NA_KH)
    kr = kstart[:, None] + np.arange(NA_KROWS)[None, :]
    row_ok = (kr[:, None, :] >= r0[:, :, None]) & (kr[:, None, :] < r0[:, :, None] + NA_KH)
    dr = kr[:, None, :] - r[:, :, None] + (NA_KH - 1)
    c = np.arange(GRID_W)
    c0 = np.clip(c - NA_KW // 2, 0, GRID_W - NA_KW)
    kc = np.arange(GRID_W)
    col_ok = (kc[None, :] >= c0[:, None]) & (kc[None, :] < c0[:, None] + NA_KW)
    pad = GRID_W - NA_KW
    rp = jnp.pad(rpb.astype(F32), ((0, 0), (0, 0), (pad, pad)))
    by_col = jnp.stack([rp[:, :, GRID_W - 1 - ci:2 * GRID_W - 1 - ci] for ci in range(GRID_W)], axis=2)
    by_col = jnp.where(col_ok[None, None], by_col, NEG_INF)
    outside = jnp.full((N_NA_HEADS, GRID_W, GRID_W), NEG_INF, F32)
    patterns = []
    for pi in range(5):
        q_rows = []
        for qi in range(NA_ROWS):
            blocks_ = [by_col[:, int(dr[pi, qi, ki])] if row_ok[pi, qi, ki] else outside
                       for ki in range(NA_KROWS)]
            q_rows.append(jnp.concatenate(blocks_, axis=-1))
        patterns.append(jnp.concatenate(q_rows, axis=-2))
    return jnp.stack(patterns, axis=0)


def _na_kernel(q_ref, k_ref, v_ref, bias_ref, o_ref, *, rows):
    n_blk = rows // NA_ROWS
    tq = NA_ROWS * GRID_W
    nk = NA_KROWS * GRID_W
    first = _first_head_lanes((1, LANES))
    for sub in range(NA_SUBBLOCKS):
        rb = pl.program_id(1) * NA_SUBBLOCKS + sub
        qrows = slice(sub * tq, (sub + 1) * tq)
        kstart = jnp.clip(NA_ROWS * rb - NA_KH // 2, 0, rows - NA_KROWS) * GRID_W
        kstart = pl.multiple_of(kstart, GRID_W)
        kw = k_ref[pl.ds(kstart, nk), :]
        vw = v_ref[pl.ds(kstart, nk), :]
        pattern = jnp.clip(rb, 0, 2) + jnp.clip(rb - (n_blk - 3), 0, 2)
        q2 = q_ref[qrows, :]
        zero = jnp.zeros_like(q2)

        def one_head(qh, bias):
            s = _nt_dot(qh, kw) + bias
            e = jnp.exp(s - jnp.max(s, axis=-1, keepdims=True))
            return _dot(e.astype(BF16), vw) / jnp.sum(e, axis=-1, keepdims=True)

        o0 = one_head(jnp.where(first, q2, zero), bias_ref[pattern, 0])
        o1 = one_head(jnp.where(first, zero, q2), bias_ref[pattern, 1])
        o_ref[qrows, :] = jnp.where(first, o0, o1).astype(o_ref.dtype)


def _neighbourhood_attention(h, rpb):
    t = h.shape[0]
    rows = t // GRID_W
    n_blk = rows // NA_ROWS
    groups = D_MODEL // LANES
    tq = NA_SUBBLOCKS * NA_ROWS * GRID_W
    bias = _na_bias_table(rpb, rows)
    return pl.pallas_call(
        functools.partial(_na_kernel, rows=rows),
        out_shape=jax.ShapeDtypeStruct((t, D_MODEL), BF16),
        grid=(groups, n_blk // NA_SUBBLOCKS),
        in_specs=[
            pl.BlockSpec((tq, LANES), lambda p, rb: (rb, p)),
            pl.BlockSpec((t, LANES), lambda p, rb: (0, groups + p)),
            pl.BlockSpec((t, LANES), lambda p, rb: (0, 2 * groups + p)),
            pl.BlockSpec((bias.shape[0], HEADS_PER_GROUP) + bias.shape[2:], lambda p, rb: (0, p, 0, 0)),
        ],
        out_specs=pl.BlockSpec((tq, LANES), lambda p, rb: (rb, p)),
        compiler_params=_params(("parallel", "parallel")),
        name="neighbourhood_attn",
    )(h, h, h, bias)


def _odd_mixer(x, w_in, w_out, rpb, g, b):
    h = _proj_odd(x, w_in.astype(BF16))
    y = _neighbourhood_attention(h, rpb)
    return _out_odd(x, y, w_out.astype(BF16), g, b)


def _router_kernel(x_ref, wt_ref, b_ref, idx_ref, gate_ref, rank_ref, count_ref):
    tm = x_ref.shape[0]

    @pl.when(pl.program_id(0) == 0)
    def _init():
        count_ref[...] = jnp.zeros_like(count_ref)

    logits = lax.dot_general(wt_ref[...], x_ref[...], (((1,), (1,)), ((), ())),
                             precision=lax.Precision.HIGHEST, preferred_element_type=F32)
    logits = logits + b_ref[...]
    expert = lax.broadcasted_iota(jnp.int32, logits.shape, 0)
    vals, idxs = [], []
    for _ in range(TOP_K):
        m = jnp.max(logits, axis=0, keepdims=True)
        sel = jnp.min(jnp.where(logits == m, expert, N_EXPERTS), axis=0, keepdims=True)
        vals.append(m)
        idxs.append(sel)
        logits = jnp.where(expert == sel, -jnp.inf, logits)
    exps = [jnp.exp(v - vals[0]) for v in vals]
    den = exps[0] + exps[1] + exps[2] + exps[3]
    for k in range(TOP_K):
        idx_ref[k:k + 1, :] = idxs[k]
        gate_ref[k:k + 1, :] = exps[k] / den

    onehots = [(expert == idxs[k]).astype(F32) for k in range(TOP_K)]
    per_tok = onehots[0] + onehots[1] + onehots[2] + onehots[3]
    earlier = lax.broadcasted_iota(jnp.int32, (tm, tm), 0) < lax.broadcasted_iota(jnp.int32, (tm, tm), 1)
    before = count_ref[...] + _dot(per_tok.astype(BF16), earlier.astype(BF16))
    for k in range(TOP_K):
        rank_ref[k:k + 1, :] = jnp.sum(onehots[k] * before, axis=0, keepdims=True).astype(jnp.int32)
    count_ref[...] += jnp.sum(per_tok, axis=1, keepdims=True)


def _router(x, router_w, router_b):
    t = x.shape[0]
    tm = ROUTER_TM
    row = pl.BlockSpec((TOP_K, tm), lambda i: (0, i))
    return pl.pallas_call(
        _router_kernel,
        out_shape=(jax.ShapeDtypeStruct((TOP_K, t), jnp.int32),
                   jax.ShapeDtypeStruct((TOP_K, t), F32),
                   jax.ShapeDtypeStruct((TOP_K, t), jnp.int32),
                   jax.ShapeDtypeStruct((N_EXPERTS, 1), F32)),
        grid=(t // tm,),
        in_specs=[
            pl.BlockSpec((tm, D_MODEL), lambda i: (i, 0)),
            pl.BlockSpec((N_EXPERTS, D_MODEL), lambda i: (0, 0)),
            pl.BlockSpec((N_EXPERTS, 1), lambda i: (0, 0)),
        ],
        out_specs=(row, row, row, pl.BlockSpec((N_EXPERTS, 1), lambda i: (0, 0))),
        compiler_params=_params(("arbitrary",)),
        name="moe_router",
    )(x, router_w.T, router_b.reshape(-1, 1))


def _slot_layout(top_i, rank, counts):
    t = top_i.shape[1]
    bm = MOE_BM
    experts = jnp.arange(N_EXPERTS, dtype=jnp.int32)
    counts = counts.reshape(-1).astype(jnp.int32)
    pcounts = (counts + bm - 1) // bm * bm
    pend = jnp.cumsum(pcounts)
    pstart = pend - pcounts
    pos = rank + jnp.sum(jnp.where(top_i[:, :, None] == experts, pstart, 0), axis=-1)
    n_blk = (TOP_K * t) // bm + N_EXPERTS
    blk_start = jnp.arange(n_blk, dtype=jnp.int32) * bm
    blk_e = jnp.minimum(jnp.sum((pend[None, :] <= blk_start[:, None]).astype(jnp.int32), axis=1),
                        N_EXPERTS - 1)
    n_used = (pend[-1] // bm).astype(jnp.int32).reshape(1)
    start = jnp.cumsum(counts) - counts
    sorted_idx = rank + jnp.sum(jnp.where(top_i[:, :, None] == experts, start, 0), axis=-1)
    blk_row0 = start[blk_e] + blk_start - pstart[blk_e]
    return pos.astype(jnp.int32), sorted_idx.astype(jnp.int32), blk_e, blk_row0.astype(jnp.int32), n_used, n_blk


def _row_copy(src, src_row, dst, dst_row, sem):
    return pltpu.make_async_copy(src.at[pl.ds(src_row, 1)], dst.at[pl.ds(dst_row, 1)], sem)


def _expert_kernel(sorted_idx_ref, blk_e_ref, blk_row0_ref, n_used_ref, x_hbm, wu_ref, bu_ref, wd_ref, bd_ref,
                   o_ref, wu_bf_ref, wd_bf_ref, *rest, n_tok):
    xbufs = rest[:ROW_BUFFERS]
    inv_ref, gather_sem = rest[ROW_BUFFERS:]
    bm = MOE_BM
    b = pl.program_id(0)
    n_used = n_used_ref[0]
    used = b < n_used
    half = LANES // 2
    n_groups = D_FF // LANES
    n_assign = sorted_idx_ref.shape[0]

    @pl.when(b == 0)
    def _invert_permutation():
        def body(group, carry):
            for u in range(ROW_UNROLL):
                f = group * ROW_UNROLL + u
                inv_ref[sorted_idx_ref[f]] = f
            return carry

        lax.fori_loop(0, n_assign // ROW_UNROLL, body, 0)

    new_expert = jnp.logical_or(b == 0, blk_e_ref[b] != blk_e_ref[jnp.maximum(b - 1, 0)])

    def gather_rows(blk, parity, rows):
        row0 = blk_row0_ref[blk]
        for r in rows:
            flat = inv_ref[jnp.minimum(row0 + r, n_assign - 1)]
            _row_copy(x_hbm, flat & (n_tok - 1), xbufs[parity], r, gather_sem.at[parity]).start(priority=r % 2)

    def wait_rows(parity):
        pltpu.make_async_copy(x_hbm.at[pl.ds(0, bm)], xbufs[parity], gather_sem.at[parity]).wait()

    @pl.when(b == 0)
    def _first_rows():
        for ahead in range(ROW_BUFFERS - 1):
            gather_rows(jnp.minimum(ahead, n_used - 1), ahead, range(bm))

    @pl.when(jnp.logical_and(used, new_expert))
    def _stage_weights():
        wu_bf_ref[...] = wu_ref[...].astype(BF16)
        row = lax.broadcasted_iota(jnp.int32, (LANES, LANES), 0)
        col = lax.broadcasted_iota(jnp.int32, (LANES, LANES), 1)
        select = (col == row // 2 + half * (row % 2)).astype(BF16)
        for g in range(n_groups):
            rows = slice(g * LANES, (g + 1) * LANES)
            wd_bf_ref[rows, :] = _dot(select, wd_ref[rows, :].astype(BF16)).astype(BF16)

    def compute(parity):
        wait_rows(parity)
        next_blk = jnp.minimum(b + ROW_BUFFERS - 1, n_used - 1)
        next_parity = (parity + ROW_BUFFERS - 1) % ROW_BUFFERS
        per_group = bm // n_groups
        xb = xbufs[parity][...].astype(BF16)
        even = lax.broadcasted_iota(jnp.int32, (1, LANES), 1) % 2 == 0
        packed = []
        for g in range(n_groups):
            cols = slice(2 * g * LANES, (2 * g + 2) * LANES)
            h = _dot(xb, wu_bf_ref[:, cols]) + bu_ref[:, cols]
            lo = h[:, :LANES]
            hi = h[:, LANES:]
            x_glu = jnp.minimum(jnp.where(even, lo, pltpu.roll(hi, 1, 1)), SWIGLU_LIMIT)
            x_lin = jnp.clip(jnp.where(even, pltpu.roll(lo, LANES - 1, 1), hi), -SWIGLU_LIMIT, SWIGLU_LIMIT)
            packed.append((x_glu * jax.nn.sigmoid(SWIGLU_ALPHA * x_glu) * (x_lin + 1.0)).astype(BF16))
            gather_rows(next_blk, next_parity, range(g * per_group, (g + 1) * per_group))
        act = jnp.concatenate(packed, axis=1)
        o_ref[...] = _dot(act, wd_bf_ref[...]) + bd_ref[...]

        @pl.when(b == n_used - 1)
        def _drain():
            for other in range(1, ROW_BUFFERS):
                wait_rows((parity + other) % ROW_BUFFERS)

    for parity in range(ROW_BUFFERS):
        pl.when(jnp.logical_and(used, b % ROW_BUFFERS == parity))(functools.partial(compute, parity))

    @pl.when(jnp.logical_not(used))
    def _unused():
        o_ref[...] = jnp.zeros_like(o_ref)


def _experts(x, sorted_idx_flat, blk_e, blk_row0, n_used, n_blk, layer, w_up, b_up, w_down, b_down):
    bm = MOE_BM
    t = x.shape[0]
    assert t & (t - 1) == 0

    def wspec(shape):
        return pl.BlockSpec((None, None) + shape, lambda b, s, e, r, n: (layer, e[b], 0, 0))

    return pl.pallas_call(
        functools.partial(_expert_kernel, n_tok=t),
        out_shape=jax.ShapeDtypeStruct((n_blk * bm, D_MODEL), F32),
        grid_spec=pltpu.PrefetchScalarGridSpec(
            num_scalar_prefetch=4,
            grid=(n_blk,),
            in_specs=[
                pl.BlockSpec(memory_space=pl.ANY),
                wspec((D_MODEL, 2 * D_FF)), wspec((1, 2 * D_FF)),
                wspec((D_FF, D_MODEL)), wspec((1, D_MODEL)),
            ],
            out_specs=pl.BlockSpec((bm, D_MODEL), lambda b, s, e, r, n: (b, 0)),
            scratch_shapes=[pltpu.VMEM((D_MODEL, 2 * D_FF), BF16), pltpu.VMEM((D_FF, D_MODEL), BF16)]
            + [pltpu.VMEM((bm, D_MODEL), F32)] * ROW_BUFFERS
            + [pltpu.SMEM(sorted_idx_flat.shape, jnp.int32), pltpu.SemaphoreType.DMA((ROW_BUFFERS,))],
        ),
        compiler_params=_params(("arbitrary",)),
        name="moe_experts",
    )(sorted_idx_flat, blk_e, blk_row0, n_used, x, w_up, b_up, w_down, b_down)


def _combine_kernel(pos_ref, y_hbm, x_ref, gate_ref, g_ref, b_ref, o_ref, buf_ref, sem, *, n_tok, tm):
    i = pl.program_id(0)
    slot = i % 2

    def gather_tile(tile, buf_slot):
        base = tile * tm

        def issue(group, carry):
            j0 = pl.multiple_of(group * ROW_UNROLL, ROW_UNROLL)
            for u in range(ROW_UNROLL):
                for k in range(TOP_K):
                    _row_copy(y_hbm, pos_ref[k * n_tok + base + j0 + u], buf_ref.at[buf_slot, k], j0 + u,
                              sem.at[buf_slot]).start(priority=k % 2)
            return carry

        lax.fori_loop(0, tm // ROW_UNROLL, issue, 0)

    @pl.when(i == 0)
    def _first_tile():
        gather_tile(0, 0)

    @pl.when(i + 1 < pl.num_programs(0))
    def _next_tile():
        gather_tile(i + 1, 1 - slot)

    for k in range(TOP_K):
        pltpu.make_async_copy(y_hbm.at[pl.ds(0, tm)], buf_ref.at[slot, k], sem.at[slot]).wait()
    gates = gate_ref[...]
    ffn = gates[:, 0:1] * buf_ref[slot, 0]
    for k in range(1, TOP_K):
        ffn = ffn + gates[:, k:k + 1] * buf_ref[slot, k]
    z = DEEPNORM_ALPHA * x_ref[...] + ffn
    o_ref[...] = _layer_norm(z, g_ref[...], b_ref[...])


def _combine(x, y_slots, pos_flat, gates_tk, g, b):
    t = x.shape[0]
    tm = COMBINE_TM
    full = pl.BlockSpec((tm, D_MODEL), lambda i, p: (i, 0))
    vec = pl.BlockSpec((1, D_MODEL), lambda i, p: (0, 0))
    return pl.pallas_call(
        functools.partial(_combine_kernel, n_tok=t, tm=tm),
        out_shape=jax.ShapeDtypeStruct((t, D_MODEL), F32),
        grid_spec=pltpu.PrefetchScalarGridSpec(
            num_scalar_prefetch=1,
            grid=(t // tm,),
            in_specs=[pl.BlockSpec(memory_space=pl.ANY), full,
                      pl.BlockSpec((tm, TOP_K), lambda i, p: (i, 0)), vec, vec],
            out_specs=full,
            scratch_shapes=[pltpu.VMEM((2, TOP_K, tm, D_MODEL), F32), pltpu.SemaphoreType.DMA((2,))],
        ),
        compiler_params=_params(("arbitrary",)),
        name="moe_combine",
    )(pos_flat, y_slots, x, gates_tk, g.reshape(1, -1), b.reshape(1, -1))


def _moe_layer(x, layer, router_w, router_b, w_up, b_up, w_down, b_down, g, b):
    top_i, gates, rank, counts = _router(x, router_w, router_b)
    pos, sorted_idx, blk_e, blk_row0, n_used, n_blk = _slot_layout(top_i, rank, counts)
    pos_flat = pos.reshape(-1)
    y_slots = _experts(x, sorted_idx.reshape(-1), blk_e, blk_row0, n_used, n_blk, layer,
                       w_up, b_up[:, :, None, :], w_down, b_down[:, :, None, :])
    return _combine(x, y_slots, pos_flat, gates.T, g, b)


def kernel(x, ab_w_in, ab_w_out, ret_decay, c_w_in, c_w_out, c_rpb, ln_g, ln_b,
           router_w, router_b, exp_w_up, exp_b_up, exp_w_down, exp_b_down):
    b_, t_, d_ = x.shape
    assert b_ == 1 and d_ == D_MODEL
    xs = x.reshape(t_, d_)
    cos_t, sin_t = _rope_tables(t_)
    for layer in range(DEPTH):
        j = layer // 2
        if layer % 2 == 0:
            xs = _even_mixer(xs, ab_w_in[j], ab_w_out[j], ret_decay[j], ln_g[layer, 0], ln_b[layer, 0],
                             cos_t, sin_t)
        else:
            xs = _odd_mixer(xs, c_w_in[j], c_w_out[j], c_rpb[j], ln_g[layer, 0], ln_b[layer, 0])
        xs = _moe_layer(xs, layer, router_w[layer], router_b[layer], exp_w_up, exp_b_up,
                        exp_w_down, exp_b_down, ln_g[layer, 1], ln_b[layer, 1])
    return xs.reshape(b_, t_, d_)
```
